```python
import jax
import jax.numpy as jnp
from jax import lax
import numpy as np

D_MODEL = 2048
BATCH = 4
SEQ = 8192
DEPTH = 4
DEC_BATCH = 8
DEC_SEQ = 64
PAST_LEN = 2048

CHUNK = 64
Q_BLOCK = 128
N_EVEN = (DEPTH + 1) // 2
N_ODD = DEPTH // 2
RMS_EPS = 1e-6
ROPE_BASE = 10000.0

RET_HEADS = 8
RET_DK = 128
RET_DV = 128
RET_QK_W = RET_HEADS * RET_DK
RET_V_W = RET_HEADS * RET_DV

MLA_HEADS = 8
MLA_NOPE = 128
MLA_ROPE = 64
MLA_V = 128
MLA_Q_LORA = 768
MLA_KV_LORA = 512

EVEN_SPLITS = (RET_QK_W, 2 * RET_QK_W, 2 * RET_QK_W + RET_V_W, 2 * RET_QK_W + 2 * RET_V_W,
               2 * RET_QK_W + 2 * RET_V_W + MLA_Q_LORA,
               2 * RET_QK_W + 2 * RET_V_W + MLA_Q_LORA + MLA_KV_LORA)
EVEN_IN = EVEN_SPLITS[-1] + MLA_ROPE
EVEN_OUT = RET_V_W + MLA_HEADS * MLA_V

M_HEADS = 8
M_DK = D_MODEL // (2 * M_HEADS)
M_DV = D_MODEL // M_HEADS
M_QK_W = M_HEADS * M_DK
M_V_W = M_HEADS * M_DV
ODD_SPLITS = (M_QK_W, 2 * M_QK_W, 2 * M_QK_W + M_V_W, 2 * M_QK_W + 2 * M_V_W)
ODD_IN = ODD_SPLITS[-1] + 2 * M_HEADS
ODD_OUT = M_V_W
F_BIAS_LO = 3.0
F_BIAS_HI = 6.0

MEM_TOKENS = 256
X_HEADS = 4
X_DH = 128
X_W = X_HEADS * X_DH
D_FF = 4 * D_MODEL

kernel_name = 'hybrid_streaming_encoder_step'


def rms_norm(x, g):
    x32 = x.astype(jnp.float32)
    y = x32 * lax.rsqrt(jnp.mean(x32 * x32, axis=-1, keepdims=True) + RMS_EPS)
    return (y * g.astype(jnp.float32)).astype(x.dtype)


def head_norm(x, g, center):
    B, S, H, d = x.shape
    x32 = x.astype(jnp.float32)
    if center:
        x32 = x32 - jnp.mean(x32, axis=-1, keepdims=True)
    y = x32 * lax.rsqrt(jnp.mean(x32 * x32, axis=-1, keepdims=True) + RMS_EPS)
    return (y.reshape(B, S, H * d) * g.astype(jnp.float32)).astype(x.dtype)


def rope(x, pos):
    half = x.shape[-1] // 2
    inv = jnp.power(ROPE_BASE, -jnp.arange(half, dtype=jnp.float32) / half)
    ang = pos.astype(jnp.float32)[:, None] * inv[None, :]
    ang = ang.reshape(ang.shape[:1] + (1,) * (x.ndim - 3) + (half,))
    cos, sin = jnp.cos(ang), jnp.sin(ang)
    x32 = x.astype(jnp.float32)
    x1, x2 = x32[..., :half], x32[..., half:]
    return jnp.concatenate([x1 * cos - x2 * sin, x2 * cos + x1 * sin], axis=-1).astype(x.dtype)


def to_chunks(a, L):
    B, S, H = a.shape[:3]
    a = a.reshape((B, S // L, L, H) + a.shape[3:])
    return jnp.transpose(a, (1, 0, 3, 2) + tuple(range(4, a.ndim)))


def from_chunks(a, B, S):
    a = jnp.transpose(a, (1, 0, 3, 2) + tuple(range(4, a.ndim)))
    return a.reshape((B, S) + a.shape[3:])


def retention_chunked(q, k, v, s0):
    B, S, H, _ = q.shape
    L = min(CHUNK, S)
    f32 = jnp.float32
    qc, kc, vc = (to_chunks(a.astype(f32), L) for a in (q, k, v))
    log_g = jnp.log1p(-jnp.exp2(-5.0 - jnp.arange(H, dtype=f32)))
    t = jnp.arange(L, dtype=f32)
    diff = t[:, None] - t[None, :]
    intra = jnp.where(diff >= 0, jnp.exp(log_g[:, None, None] * jnp.maximum(diff, 0.0)), 0.0)
    q_dec = jnp.exp(log_g[:, None] * (t + 1.0))[..., None]
    k_dec = jnp.exp(log_g[:, None] * (L - 1.0 - t))[..., None]
    c_dec = jnp.exp(log_g * L)[:, None, None]

    def step(state, inp):
        qb, kb, vb = inp
        sc = jnp.einsum('bhld,bhmd->bhlm', qb, kb) * intra
        out = jnp.einsum('bhlm,bhmv->bhlv', sc, vb) + q_dec * jnp.einsum('bhld,bhdv->bhlv', qb, state)
        state = c_dec * state + jnp.einsum('bhld,bhlv->bhdv', kb * k_dec, vb)
        return state, out

    s_fin, out = lax.scan(step, s0.astype(f32), (qc, kc, vc))
    return from_chunks(out, B, S).astype(q.dtype), s_fin.astype(s0.dtype)


def mlstm_chunked(q, k, v, i_pre, f_pre, c0, n0, m0):
    B, S, H, _ = q.shape
    L = min(CHUNK, S)
    f32 = jnp.float32
    qc, kc, vc = (to_chunks(a.astype(f32), L) for a in (q, k, v))
    ic = to_chunks(i_pre.astype(f32), L)
    fc = to_chunks(jax.nn.log_sigmoid(f_pre.astype(f32)), L)
    causal = jnp.tril(jnp.ones((L, L), dtype=bool))

    def step(carry, inp):
        c, n, m = carry
        qb, kb, vb, ib, lfb = inp
        b = jnp.cumsum(lfb, axis=-1)
        log_inter = b + m[..., None]
        log_intra = jnp.where(causal, b[..., :, None] - b[..., None, :] + ib[..., None, :], -jnp.inf)
        m_t = jnp.maximum(log_inter, jnp.max(log_intra, axis=-1))
        w_intra = jnp.exp(log_intra - m_t[..., None])
        w_inter = jnp.exp(log_inter - m_t)
        sc = jnp.einsum('bhld,bhsd->bhls', qb, kb) * w_intra
        num = jnp.einsum('bhls,bhsv->bhlv', sc, vb) + w_inter[..., None] * jnp.einsum('bhvd,bhld->bhlv', c, qb)
        den = jnp.sum(sc, axis=-1) + w_inter * jnp.einsum('bhd,bhld->bhl', n, qb)
        h = num / jnp.maximum(jnp.abs(den), jnp.exp(-m_t))[..., None]
        m_new = m_t[..., -1]
        w_state = jnp.exp(b[..., -1] + m - m_new)
        kw = kb * jnp.exp(b[..., -1:] - b + ib - m_new[..., None])[..., None]
        c = w_state[..., None, None] * c + jnp.einsum('bhlv,bhld->bhvd', vb, kw)
        n = w_state[..., None] * n + jnp.sum(kw, axis=2)
        return (c, n, m_new), h

    (c, n, m), out = lax.scan(step, (c0.astype(f32), n0.astype(f32), m0.astype(f32)), (qc, kc, vc, ic, fc))
    return from_chunks(out, B, S).astype(q.dtype), c.astype(c0.dtype), n.astype(n0.dtype), m.astype(m0.dtype)


def mla_attention(q_nope, q_rope, k_nope, k_rope, v, q_pos, k_pos):
    B, Sq, H, _ = q_nope.shape
    qb = min(Q_BLOCK, Sq)
    nb = Sq // qb
    scale = (MLA_NOPE + MLA_ROPE) ** -0.5
    k_chunk = k_pos // CHUNK

    def block(args):
        qn, qr, qp = args
        s = (jnp.einsum('bqhd,bkhd->bhqk', qn, k_nope)
             + jnp.einsum('bqhr,bkr->bhqk', qr, k_rope)).astype(jnp.float32) * scale
        mask = k_chunk[None, :] <= (qp // CHUNK)[:, None]
        p = jax.nn.softmax(jnp.where(mask, s, -jnp.inf), axis=-1).astype(v.dtype)
        return jnp.einsum('bhqk,bkhv->bqhv', p, v)

    xs = (jnp.moveaxis(q_nope.reshape(B, nb, qb, H, MLA_NOPE), 1, 0),
          jnp.moveaxis(q_rope.reshape(B, nb, qb, H, MLA_ROPE), 1, 0),
          q_pos.reshape(nb, qb))
    out = lax.map(block, xs)
    return jnp.moveaxis(out, 0, 1).reshape(B, Sq, H, MLA_V)


def even_mixer(h, pos, w_in, q_norm_g, kv_norm_g, w_uq, w_ukv, gn_g, w_out, ret_s0, past_ckv, past_kr):
    B, S, _ = h.shape
    rq, rk, rv, rg, cq, ckv, kr = jnp.split(h @ w_in, EVEN_SPLITS, axis=-1)
    rq = rope(rq.reshape(B, S, RET_HEADS, RET_DK), pos)
    rk = rope(rk.reshape(B, S, RET_HEADS, RET_DK), pos) * (RET_DK ** -0.5)
    ret, ret_s = retention_chunked(rq, rk, rv.reshape(B, S, RET_HEADS, RET_DV), ret_s0)
    ret = jax.nn.silu(rg) * head_norm(ret, gn_g, center=True)
    q = (rms_norm(cq, q_norm_g) @ w_uq).reshape(B, S, MLA_HEADS, MLA_NOPE + MLA_ROPE)
    q_nope, q_rope = q[..., :MLA_NOPE], rope(q[..., MLA_NOPE:], pos)
    ckv = rms_norm(ckv, kv_norm_g)
    kr = rope(kr, pos)
    if past_ckv is None:
        all_ckv, all_kr, k_pos = ckv, kr, pos
    else:
        all_ckv = jnp.concatenate([past_ckv.astype(ckv.dtype), ckv], axis=1)
        all_kr = jnp.concatenate([past_kr.astype(kr.dtype), kr], axis=1)
        k_pos = jnp.concatenate([jnp.arange(past_ckv.shape[1], dtype=jnp.int32), pos])
    kv = (all_ckv @ w_ukv).reshape(B, all_ckv.shape[1], MLA_HEADS, MLA_NOPE + MLA_V)
    att = mla_attention(q_nope, q_rope, kv[..., :MLA_NOPE], all_kr, kv[..., MLA_NOPE:], pos, k_pos)
    out = jnp.concatenate([ret, att.reshape(B, S, MLA_HEADS * MLA_V)], axis=-1) @ w_out
    return out, ret_s, ckv, kr


def odd_mixer(h, w_in, b_gates, norm_g, w_out, c0, n0, m0):
    B, S, _ = h.shape
    q, k, v, o, gates = jnp.split(h @ w_in, ODD_SPLITS, axis=-1)
    gates = gates + b_gates
    hh, c, n, m = mlstm_chunked(q.reshape(B, S, M_HEADS, M_DK) * (M_DK ** -0.5),
                                k.reshape(B, S, M_HEADS, M_DK),
                                v.reshape(B, S, M_HEADS, M_DV),
                                gates[..., :M_HEADS], gates[..., M_HEADS:], c0, n0, m0)
    out = (head_norm(hh, norm_g, center=False) * jax.nn.sigmoid(o)) @ w_out
    return out, c, n, m


def mem_kv(mem, g, w_k, w_v):
    B, T, _ = mem.shape
    mn = rms_norm(mem, g)
    return (mn @ w_k).reshape(B, T, X_HEADS, X_DH), (mn @ w_v).reshape(B, T, X_HEADS, X_DH)


def cross_attention(h, mem_k, mem_v, w_q, w_o):
    B, S, _ = h.shape
    q = (h @ w_q).reshape(B, S, X_HEADS, X_DH)
    s = jnp.einsum('bqhd,bkhd->bhqk', q, mem_k.astype(q.dtype)).astype(jnp.float32) * (X_DH ** -0.5)
    p = jax.nn.softmax(s, axis=-1).astype(h.dtype)
    o = jnp.einsum('bhqk,bkhd->bqhd', p, mem_v.astype(h.dtype)).reshape(B, S, X_W)
    return o @ w_o


def squared_relu_mlp(h, w1, w2):
    return jnp.square(jax.nn.relu(h @ w1)) @ w2


def run_trunk(x, pos, mem_k, mem_v, ret_s0, c0, n0, m0, past_ckv, past_kr, p):
    ckvs, krs, rets, cs, ns, ms = [], [], [], [], [], []
    for layer in range(DEPTH):
        g = p['norm_g'][layer]
        h = rms_norm(x, g[0])
        if layer % 2 == 0:
            e = layer // 2
            out, rs, ckv, kr = even_mixer(h, pos, p['w_in_even'][e], p['mla_q_norm_g'][e], p['mla_kv_norm_g'][e],
                                          p['w_uq'][e], p['w_ukv'][e], p['ret_gn_g'][e], p['w_out_even'][e],
                                          ret_s0[e],
                                          None if past_ckv is None else past_ckv[e],
                                          None if past_kr is None else past_kr[e])
            rets.append(rs)
            ckvs.append(ckv)
            krs.append(kr)
        else:
            o = layer // 2
            out, c, n, m = odd_mixer(h, p['w_in_odd'][o], p['b_gates_odd'][o], p['mlstm_norm_g'][o],
                                     p['w_out_odd'][o], c0[o], n0[o], m0[o])
            cs.append(c)
            ns.append(n)
            ms.append(m)
        x = x + rms_norm(out, g[1])
        h = rms_norm(x, g[2])
        x = x + rms_norm(cross_attention(h, mem_k[layer], mem_v[layer], p['w_xq'][layer], p['w_xo'][layer]), g[3])
        h = rms_norm(x, g[4])
        x = x + rms_norm(squared_relu_mlp(h, p['w_mlp1'][layer], p['w_mlp2'][layer]), g[5])
    return x, jnp.stack(ckvs), jnp.stack(krs), jnp.stack(rets), jnp.stack(cs), jnp.stack(ns), jnp.stack(ms)


def setup_inputs(seed: int = 0) -> dict:
    key = jax.random.key(seed)
    ks = jax.random.split(key, 32)
    f32 = jnp.float32
    nrm = lambda k, shape, scale=1.0: scale * jax.random.normal(k, shape, f32)
    gain = lambda k, shape: 1.0 + 0.02 * jax.random.normal(k, shape, f32)
    f_bias = jnp.linspace(F_BIAS_LO, F_BIAS_HI, M_HEADS, dtype=f32)
    b_gates = jnp.concatenate([nrm(ks[30], (N_ODD, M_HEADS), 0.1),
                               f_bias + nrm(ks[31], (N_ODD, M_HEADS), 0.1)], axis=-1)
    return {
        'x_prompt': nrm(ks[0], (BATCH, SEQ, D_MODEL)),
        'x_sample': nrm(ks[1], (DEC_BATCH, DEC_SEQ, D_MODEL)),
        'cache_mla_ckv': nrm(ks[2], (N_EVEN, DEC_BATCH, PAST_LEN, MLA_KV_LORA)),
        'cache_mla_krope': nrm(ks[3], (N_EVEN, DEC_BATCH, PAST_LEN, MLA_ROPE)),
        'state_ret': nrm(ks[4], (N_EVEN, DEC_BATCH, RET_HEADS, RET_DK, RET_DV), 0.5),
        'state_mlstm_C': nrm(ks[5], (N_ODD, DEC_BATCH, M_HEADS, M_DV, M_DK), 0.5),
        'state_mlstm_n': nrm(ks[6], (N_ODD, DEC_BATCH, M_HEADS, M_DK), 0.5),
        'state_mlstm_m': nrm(ks[7], (N_ODD, DEC_BATCH, M_HEADS)),
        'cache_mem_k': nrm(ks[8], (DEPTH, DEC_BATCH, MEM_TOKENS, X_HEADS, X_DH)),
        'cache_mem_v': nrm(ks[9], (DEPTH, DEC_BATCH, MEM_TOKENS, X_HEADS, X_DH)),
        'mem_prompt': nrm(ks[10], (BATCH, MEM_TOKENS, D_MODEL)),
        'norm_g': gain(ks[11], (DEPTH, 6, D_MODEL)),
        'mem_norm_g': gain(ks[12], (DEPTH, D_MODEL)),
        'w_in_even': nrm(ks[13], (N_EVEN, D_MODEL, EVEN_IN), D_MODEL ** -0.5),
        'mla_q_norm_g': gain(ks[14], (N_EVEN, MLA_Q_LORA)),
        'mla_kv_norm_g': gain(ks[15], (N_EVEN, MLA_KV_LORA)),
        'w_uq': nrm(ks[16], (N_EVEN, MLA_Q_LORA, MLA_HEADS * (MLA_NOPE + MLA_ROPE)), MLA_Q_LORA ** -0.5),
        'w_ukv': nrm(ks[17], (N_EVEN, MLA_KV_LORA, MLA_HEADS * (MLA_NOPE + MLA_V)), MLA_KV_LORA ** -0.5),
        'ret_gn_g': gain(ks[18], (N_EVEN, RET_V_W)),
        'w_out_even': nrm(ks[19], (N_EVEN, EVEN_OUT, D_MODEL), EVEN_OUT ** -0.5),
        'w_in_odd': nrm(ks[20], (N_ODD, D_MODEL, ODD_IN), D_MODEL ** -0.5),
        'b_gates_odd': b_gates,
        'mlstm_norm_g': gain(ks[21], (N_ODD, M_V_W)),
        'w_out_odd': nrm(ks[22], (N_ODD, ODD_OUT, D_MODEL), ODD_OUT ** -0.5),
        'w_xq': nrm(ks[23], (DEPTH, D_MODEL, X_W), D_MODEL ** -0.5),
        'w_xk': nrm(ks[24], (DEPTH, D_MODEL, X_W), D_MODEL ** -0.5),
        'w_xv': nrm(ks[25], (DEPTH, D_MODEL, X_W), D_MODEL ** -0.5),
        'w_xo': nrm(ks[26], (DEPTH, X_W, D_MODEL), X_W ** -0.5),
        'w_mlp1': nrm(ks[27], (DEPTH, D_MODEL, D_FF), D_MODEL ** -0.5),
        'w_mlp2': nrm(ks[28], (DEPTH, D_FF, D_MODEL), D_FF ** -0.5),
    }


def reference(x_prompt, x_sample, cache_mla_ckv, cache_mla_krope, state_ret, state_mlstm_C, state_mlstm_n,
              state_mlstm_m, cache_mem_k, cache_mem_v, mem_prompt, norm_g, mem_norm_g, w_in_even, mla_q_norm_g,
              mla_kv_norm_g, w_uq, w_ukv, ret_gn_g, w_out_even, w_in_odd, b_gates_odd, mlstm_norm_g, w_out_odd,
              w_xq, w_xk, w_xv, w_xo, w_mlp1, w_mlp2):
    p = {'norm_g': norm_g, 'w_in_even': w_in_even, 'mla_q_norm_g': mla_q_norm_g, 'mla_kv_norm_g': mla_kv_norm_g,
         'w_uq': w_uq, 'w_ukv': w_ukv, 'ret_gn_g': ret_gn_g, 'w_out_even': w_out_even, 'w_in_odd': w_in_odd,
         'b_gates_odd': b_gates_odd, 'mlstm_norm_g': mlstm_norm_g, 'w_out_odd': w_out_odd,
         'w_xq': w_xq, 'w_xo': w_xo, 'w_mlp1': w_mlp1, 'w_mlp2': w_mlp2}
    dt = x_prompt.dtype
    B, S, _ = x_prompt.shape
    pos_p = jnp.arange(S, dtype=jnp.int32)
    mem_kvs = [mem_kv(mem_prompt, mem_norm_g[l], w_xk[l], w_xv[l]) for l in range(DEPTH)]
    p_mem_k = jnp.stack([kv[0] for kv in mem_kvs])
    p_mem_v = jnp.stack([kv[1] for kv in mem_kvs])
    zero_ret = jnp.zeros((N_EVEN, B, RET_HEADS, RET_DK, RET_DV), dt)
    zero_c = jnp.zeros((N_ODD, B, M_HEADS, M_DV, M_DK), dt)
    zero_n = jnp.zeros((N_ODD, B, M_HEADS, M_DK), dt)
    zero_m = jnp.zeros((N_ODD, B, M_HEADS), dt)
    y_prompt, p_ckv, p_kr, p_ret, p_c, p_n, p_m = run_trunk(
        x_prompt, pos_p, p_mem_k, p_mem_v, zero_ret, zero_c, zero_n, zero_m, None, None, p)
    past_len = cache_mla_ckv.shape[2]
    pos_s = past_len + jnp.arange(x_sample.shape[1], dtype=jnp.int32)
    y_sample, s_ckv, s_kr, s_ret, s_c, s_n, s_m = run_trunk(
        x_sample, pos_s, cache_mem_k, cache_mem_v, state_ret, state_mlstm_C, state_mlstm_n, state_mlstm_m,
        cache_mla_ckv, cache_mla_krope, p)
    return (y_prompt, y_sample, p_ckv, p_kr, p_ret, p_c, p_n, p_m, p_mem_k, p_mem_v,
            s_ckv, s_kr, s_ret, s_c, s_n, s_m)
```

```python
import functools

import jax
import jax.numpy as jnp
from jax import lax
from jax.experimental import pallas as pl
from jax.experimental.pallas import tpu as pltpu

F32 = jnp.float32
BF16 = jnp.bfloat16

RMS_EPS = 1e-6
ROPE_BASE = 10000.0
CHUNK = 64
CHUNK_SHIFT = 6
assert 1 << CHUNK_SHIFT == CHUNK

RET_HEADS, RET_DK, RET_DV = 8, 128, 128
MLA_HEADS, MLA_NOPE, MLA_ROPE, MLA_V = 8, 128, 64, 128
MLA_Q_LORA, MLA_KV_LORA = 768, 512
MLA_QK_PAD = 256
M_HEADS, M_DK, M_DV = 8, 128, 256
X_HEADS, X_DH = 4, 128
LANES = 128

V7X_VMEM_BYTES = 64 * 1024 * 1024
MIB = 1024 * 1024


def _cparams(semantics, vmem_mib):
    assert vmem_mib * MIB < V7X_VMEM_BYTES
    return pltpu.CompilerParams(dimension_semantics=semantics, vmem_limit_bytes=vmem_mib * MIB)


def _pick(n, cands):
    for c in cands:
        if c <= n and n % c == 0:
            return c
    raise ValueError(f"no tile for {n} in {cands}")


def _rms(x, g):
    ms = jnp.mean(x * x, axis=-1, keepdims=True)
    return x * lax.rsqrt(ms + RMS_EPS) * g


def _sigmoid(x):
    return 1.0 / (1.0 + jnp.exp(-x))


def _dot(a, b):
    return jnp.dot(a, b, preferred_element_type=F32)


def _dot_nt(a, b):
    return lax.dot_general(a, b, (((1,), (1,)), ((), ())), preferred_element_type=F32)


def _dot_tn(a, b):
    return lax.dot_general(a, b, (((0,), (0,)), ((), ())), preferred_element_type=F32)


def _rope128(x, cos2, sin2):
    return x * cos2 + pltpu.roll(x, 64, 1) * sin2


def _rope64(x, cos_p, sin_p):
    return x * cos_p + (pltpu.roll(x, 32, 1) - pltpu.roll(x, 96, 1)) * sin_p


def _norm_kernel(x_ref, g_ref, o_ref):
    o_ref[...] = _rms(x_ref[...].astype(F32), g_ref[...]).astype(o_ref.dtype)


def rms_rows(x, g, out_dtype):
    m, k = x.shape
    tm = _pick(m, (512, 256, 128, 64, 32, 16, 8))
    return pl.pallas_call(
        _norm_kernel,
        grid=(m // tm,),
        in_specs=[pl.BlockSpec((tm, k), lambda i: (i, 0)), pl.BlockSpec((1, k), lambda i: (0, 0))],
        out_specs=pl.BlockSpec((tm, k), lambda i: (i, 0)),
        out_shape=jax.ShapeDtypeStruct((m, k), out_dtype),
        compiler_params=_cparams(("parallel",), 32),
        name="rms_rows",
    )(x, g.reshape(1, k).astype(F32))


def _mm_kernel(*refs, has_scale, has_bias, n_rope, tn):
    it = iter(refs)
    h_ref, w_ref = next(it), next(it)
    scale_ref = next(it) if has_scale else None
    bias_ref = next(it) if has_bias else None
    cos_ref, sin_ref = (next(it), next(it)) if n_rope else (None, None)
    o_ref = next(it)
    acc = _dot(h_ref[...], w_ref[...])
    if has_scale:
        acc = acc * scale_ref[...]
    if has_bias:
        acc = acc + bias_ref[...]
    if n_rope:
        j = pl.program_id(1)

        @pl.when(j < n_rope)
        def _():
            c, s = cos_ref[...], sin_ref[...]
            for t in range(tn // LANES):
                sl = slice(t * LANES, (t + 1) * LANES)
                o_ref[:, sl] = _rope128(acc[:, sl], c, s).astype(o_ref.dtype)

        @pl.when(j >= n_rope)
        def _():
            o_ref[...] = acc.astype(o_ref.dtype)
    else:
        o_ref[...] = acc.astype(o_ref.dtype)


def matmul(h, w, out_dtype, *, col_scale=None, col_bias=None, rope=None, rope_cols=0):
    m, k = h.shape
    n = w.shape[1]
    tm = _pick(m, (1024, 512, 256, 128, 64))
    tn = _pick(n, (512, 256, 128))
    args = [h, w]
    in_specs = [pl.BlockSpec((tm, k), lambda i, j: (i, 0)), pl.BlockSpec((k, tn), lambda i, j: (0, j))]
    for vec in (col_scale, col_bias):
        if vec is not None:
            args.append(vec.reshape(1, n).astype(F32))
            in_specs.append(pl.BlockSpec((1, tn), lambda i, j: (0, j)))
    n_rope = 0
    if rope is not None:
        assert rope_cols % tn == 0
        n_rope = rope_cols // tn
        r = rope[0].shape[0]
        tm = _pick(m, tuple(c for c in (1024, 512, 256, 128, 64) if r % c == 0))
        in_specs[0] = pl.BlockSpec((tm, k), lambda i, j: (i, 0))
        nrb = r // tm
        for tab in rope:
            args.append(tab)
            in_specs.append(pl.BlockSpec((tm, LANES), lambda i, j: (i % nrb, 0)))
    kern = functools.partial(_mm_kernel, has_scale=col_scale is not None, has_bias=col_bias is not None,
                             n_rope=n_rope, tn=tn)
    return pl.pallas_call(
        kern,
        grid=(m // tm, n // tn),
        in_specs=in_specs,
        out_specs=pl.BlockSpec((tm, tn), lambda i, j: (i, j)),
        out_shape=jax.ShapeDtypeStruct((m, n), out_dtype),
        compiler_params=_cparams(("parallel", "arbitrary"), 40),
        name="matmul",
    )(*args)


def _kv_expand(ckvn_bf16, krr_bf16, wuk_ref, wuv_ref, kcat_ref, v_ref):
    kn = _dot(ckvn_bf16, wuk_ref[...])
    for hd in range(MLA_HEADS):
        kcat_ref[:, hd * MLA_QK_PAD: hd * MLA_QK_PAD + MLA_NOPE] = (
            kn[:, hd * MLA_NOPE:(hd + 1) * MLA_NOPE].astype(BF16))
        kcat_ref[:, hd * MLA_QK_PAD + MLA_NOPE:(hd + 1) * MLA_QK_PAD] = krr_bf16
    v_ref[...] = _dot(ckvn_bf16, wuv_ref[...]).astype(BF16)


def _even_latent_kernel(h_ref, wlat_ref, gq_ref, gkv_ref, wuq_ref, wuk_ref, wuv_ref, cos_ref, sin_ref,
                        qcat_ref, kcat_ref, v_ref, ckv_ref, kr_ref):
    lat = _dot(h_ref[...], wlat_ref[...])
    cq = lat[:, :MLA_Q_LORA]
    ckv = lat[:, MLA_Q_LORA:MLA_Q_LORA + MLA_KV_LORA]
    krp = lat[:, MLA_Q_LORA + MLA_KV_LORA:]
    c, s = cos_ref[...], sin_ref[...]
    q = _dot(_rms(cq, gq_ref[...]).astype(BF16), wuq_ref[...])
    nope_w = MLA_HEADS * MLA_NOPE
    for hd in range(MLA_HEADS):
        qcat_ref[:, hd * MLA_QK_PAD: hd * MLA_QK_PAD + MLA_NOPE] = (
            q[:, hd * MLA_NOPE:(hd + 1) * MLA_NOPE].astype(BF16))
        qr = q[:, nope_w + hd * LANES: nope_w + (hd + 1) * LANES]
        qcat_ref[:, hd * MLA_QK_PAD + MLA_NOPE:(hd + 1) * MLA_QK_PAD] = _rope64(qr, c, s).astype(BF16)
    ckvn = _rms(ckv, gkv_ref[...])
    ckv_ref[...] = ckvn
    krr = _rope64(krp, c, s)
    kr_ref[...] = krr[:, :MLA_ROPE]
    _kv_expand(ckvn.astype(BF16), krr.astype(BF16), wuk_ref, wuv_ref, kcat_ref, v_ref)


def even_latent(h, wlat, gq, gkv, wuq, wuk, wuv, rope64):
    m, d = h.shape
    r = rope64[0].shape[0]
    tm = _pick(m, tuple(c for c in (256, 128, 64) if r % c == 0))
    nrb = r // tm
    full = lambda a: pl.BlockSpec(a.shape, lambda i: (0,) * a.ndim)
    row = lambda w: pl.BlockSpec((tm, w), lambda i: (i, 0))
    tab = pl.BlockSpec((tm, LANES), lambda i: (i % nrb, 0))
    qk_w = MLA_HEADS * MLA_QK_PAD
    v_w = MLA_HEADS * MLA_V
    return pl.pallas_call(
        _even_latent_kernel,
        grid=(m // tm,),
        in_specs=[row(d), full(wlat), full(gq), full(gkv), full(wuq), full(wuk), full(wuv), tab, tab],
        out_specs=[row(qk_w), row(qk_w), row(v_w), row(MLA_KV_LORA), row(MLA_ROPE)],
        out_shape=[jax.ShapeDtypeStruct((m, qk_w), BF16), jax.ShapeDtypeStruct((m, qk_w), BF16),
                   jax.ShapeDtypeStruct((m, v_w), BF16), jax.ShapeDtypeStruct((m, MLA_KV_LORA), F32),
                   jax.ShapeDtypeStruct((m, MLA_ROPE), F32)],
        compiler_params=_cparams(("parallel",), 48),
        name="even_latent",
    )(h, wlat, gq, gkv, wuq, wuk, wuv, rope64[0], rope64[1])


def _past_kv_kernel(ckv_ref, krp_ref, wuk_ref, wuv_ref, kcat_ref, v_ref):
    _kv_expand(ckv_ref[...].astype(BF16), krp_ref[...].astype(BF16), wuk_ref, wuv_ref, kcat_ref, v_ref)


def past_kv(ckvn, kr_pad, wuk, wuv):
    m = ckvn.shape[0]
    tm = _pick(m, (512, 256, 128, 64))
    full = lambda a: pl.BlockSpec(a.shape, lambda i: (0,) * a.ndim)
    row = lambda w: pl.BlockSpec((tm, w), lambda i: (i, 0))
    qk_w = MLA_HEADS * MLA_QK_PAD
    v_w = MLA_HEADS * MLA_V
    return pl.pallas_call(
        _past_kv_kernel,
        grid=(m // tm,),
        in_specs=[row(MLA_KV_LORA), row(LANES), full(wuk), full(wuv)],
        out_specs=[row(qk_w), row(v_w)],
        out_shape=[jax.ShapeDtypeStruct((m, qk_w), BF16), jax.ShapeDtypeStruct((m, v_w), BF16)],
        compiler_params=_cparams(("parallel",), 32),
        name="past_kv",
    )(ckvn, kr_pad, wuk, wuv)


def _mla_kernel(q_ref, k_ref, v_ref, o_ref, m_scr, l_scr, acc_scr, *, tq, tk, q_off, sk_valid, scale):
    qi = pl.program_id(2)
    q0 = qi * tq
    cq_lo = (q_off + q0) // CHUNK
    cq_hi = (q_off + q0 + tq - 1) // CHUNK
    n_full = jnp.minimum(lax.div((cq_lo + 1) * CHUNK, tk), sk_valid // tk)
    n_vis = lax.div(jnp.minimum((cq_hi + 1) * CHUNK, sk_valid) + tk - 1, tk)
    m_scr[...] = jnp.full(m_scr.shape, -1e30, F32)
    l_scr[...] = jnp.zeros(l_scr.shape, F32)
    acc_scr[...] = jnp.zeros(acc_scr.shape, F32)
    q = q_ref[0]

    def step(ki, masked):
        k0 = pl.multiple_of(ki * tk, tk)
        k = k_ref[0, pl.ds(k0, tk), :]
        v = v_ref[0, pl.ds(k0, tk), :]
        s = _dot_nt(q, k) * scale
        if masked:
            qpos = q_off + q0 + lax.broadcasted_iota(jnp.int32, (tq, tk), 0)
            kpos = k0 + lax.broadcasted_iota(jnp.int32, (tq, tk), 1)
            vis = jnp.logical_and((kpos >> CHUNK_SHIFT) <= (qpos >> CHUNK_SHIFT), kpos < sk_valid)
            s = jnp.where(vis, s, -jnp.inf)
        m_prev = m_scr[...]
        m_new = jnp.maximum(m_prev, jnp.max(s, axis=-1, keepdims=True))
        alpha = jnp.exp(m_prev - m_new)
        p = jnp.exp(s - m_new[:, :1])
        l_scr[...] = alpha * l_scr[...] + jnp.sum(p, axis=-1, keepdims=True)
        acc_scr[...] = alpha * acc_scr[...] + _dot(p.astype(BF16), v)
        m_scr[...] = m_new

    def body_full(ki, carry):
        step(ki, False)
        return carry

    def body_masked(ki, carry):
        step(ki, True)
        return carry

    lax.fori_loop(0, n_full, body_full, 0)
    lax.fori_loop(n_full, n_vis, body_masked, 0)
    o_ref[0] = (acc_scr[...] / l_scr[...]).astype(o_ref.dtype)


def mla_attention(qcat, kcat, v, *, q_off, sk_valid):
    b, sq, _ = qcat.shape
    skp = kcat.shape[1]
    tq = _pick(sq, (512, 256, 128, 64))
    tk = _pick(skp, (512, 256, 128))
    scale = (MLA_NOPE + MLA_ROPE) ** -0.5
    kern = functools.partial(_mla_kernel, tq=tq, tk=tk, q_off=q_off, sk_valid=sk_valid, scale=scale)
    return pl.pallas_call(
        kern,
        grid=(b, MLA_HEADS, sq // tq),
        in_specs=[pl.BlockSpec((1, tq, MLA_QK_PAD), lambda bi, h, qi: (bi, qi, h)),
                  pl.BlockSpec((1, skp, MLA_QK_PAD), lambda bi, h, qi: (bi, 0, h)),
                  pl.BlockSpec((1, skp, MLA_V), lambda bi, h, qi: (bi, 0, h))],
        out_specs=pl.BlockSpec((1, tq, MLA_V), lambda bi, h, qi: (bi, qi, h)),
        out_shape=jax.ShapeDtypeStruct((b, sq, MLA_HEADS * MLA_V), BF16),
        scratch_shapes=[pltpu.VMEM((tq, LANES), F32), pltpu.VMEM((tq, LANES), F32), pltpu.VMEM((tq, MLA_V), F32)],
        compiler_params=_cparams(("parallel", "parallel", "arbitrary"), 40),
        name="mla_attention",
    )(qcat, kcat, v)


def _retention_kernel(lg_ref, qkvg_ref, s0_ref, gn_ref, out_ref, sfin_ref, state_scr, *, lc):
    ci = pl.program_id(1)

    @pl.when(ci == 0)
    def _():
        state_scr[...] = s0_ref[0]

    ti = lax.broadcasted_iota(jnp.int32, (lc, lc), 0)
    si = lax.broadcasted_iota(jnp.int32, (lc, lc), 1)
    causal = ti >= si
    dpos = jnp.maximum(ti - si, 0).astype(F32)
    tcol = lax.broadcasted_iota(jnp.int32, (lc, 1), 0).astype(F32)
    qk_w = RET_HEADS * RET_DK
    v_w = RET_HEADS * RET_DV
    for h in range(RET_HEADS):
        lg = lg_ref[h:h + 1, 0:1]
        q = qkvg_ref[0, :, h * RET_DK:(h + 1) * RET_DK]
        k = qkvg_ref[0, :, qk_w + h * RET_DK: qk_w + (h + 1) * RET_DK]
        v = qkvg_ref[0, :, 2 * qk_w + h * RET_DV: 2 * qk_w + (h + 1) * RET_DV]
        g = qkvg_ref[0, :, 2 * qk_w + v_w + h * RET_DV: 2 * qk_w + v_w + (h + 1) * RET_DV].astype(F32)
        st = state_scr[h]
        sc = _dot_nt(q, k) * jnp.where(causal, jnp.exp(lg * dpos), 0.0)
        out = _dot(sc.astype(BF16), v) + jnp.exp(lg * (tcol + 1.0)) * _dot(q, st.astype(BF16))
        kd = (k.astype(F32) * jnp.exp(lg * (lc - 1.0 - tcol))).astype(BF16)
        state_scr[h] = jnp.exp(lg * lc) * st + _dot_tn(kd, v)
        xc = out - jnp.mean(out, axis=-1, keepdims=True)
        y = xc * lax.rsqrt(jnp.mean(xc * xc, axis=-1, keepdims=True) + RMS_EPS)
        y = y * gn_ref[:, h * RET_DV:(h + 1) * RET_DV]
        out_ref[0, :, h * RET_DV:(h + 1) * RET_DV] = (g * _sigmoid(g) * y).astype(out_ref.dtype)

    @pl.when(ci == pl.num_programs(1) - 1)
    def _():
        sfin_ref[0] = state_scr[...]


def retention(qkvg, s0, gn_g):
    b, s, w = qkvg.shape
    lc = _pick(s, (256, 128, 64))
    log_g = jnp.log1p(-jnp.exp2(-5.0 - jnp.arange(RET_HEADS, dtype=F32)))
    lg_tab = jnp.broadcast_to(log_g[:, None], (RET_HEADS, LANES))
    v_w = RET_HEADS * RET_DV
    kern = functools.partial(_retention_kernel, lc=lc)
    st_spec = pl.BlockSpec((1, RET_HEADS, RET_DK, RET_DV), lambda bi, ci: (bi, 0, 0, 0))
    return pl.pallas_call(
        kern,
        grid=(b, s // lc),
        in_specs=[pl.BlockSpec((RET_HEADS, LANES), lambda bi, ci: (0, 0)),
                  pl.BlockSpec((1, lc, w), lambda bi, ci: (bi, ci, 0)),
                  st_spec,
                  pl.BlockSpec((1, v_w), lambda bi, ci: (0, 0))],
        out_specs=[pl.BlockSpec((1, lc, v_w), lambda bi, ci: (bi, ci, 0)), st_spec],
        out_shape=[jax.ShapeDtypeStruct((b, s, v_w), BF16),
                   jax.ShapeDtypeStruct((b, RET_HEADS, RET_DK, RET_DV), F32)],
        scratch_shapes=[pltpu.VMEM((RET_HEADS, RET_DK, RET_DV), F32)],
        compiler_params=_cparams(("parallel", "arbitrary"), 40),
        name="retention",
    )(lg_tab, qkvg, s0, gn_g.reshape(1, v_w).astype(F32))


def _split3(x):
    hi = x.astype(BF16)
    r1 = x - hi.astype(F32)
    mid = r1.astype(BF16)
    lo = (r1 - mid.astype(F32)).astype(BF16)
    return hi, mid, lo


def _log_sigmoid(x):
    return jnp.minimum(x, 0.0) - jnp.log(1.0 + jnp.exp(-jnp.abs(x)))


def _mlstm_kernel(qkvo_ref, gates_ref, gates_t_ref, c0_ref, n0_ref, m0_ref, ng_ref,
                  out_ref, cfin_ref, nfin_ref, mfin_ref, c_scr, n_scr, m_scr, *, lc):
    ci = pl.program_id(1)

    @pl.when(ci == 0)
    def _():
        c_scr[...] = c0_ref[0]
        n_scr[...] = n0_ref[0]
        m_scr[...] = m0_ref[0]

    gates = gates_ref[0]
    gates_t = gates_t_ref[0]
    ti = lax.broadcasted_iota(jnp.int32, (lc, lc), 0)
    si = lax.broadcasted_iota(jnp.int32, (lc, lc), 1)
    tril = si <= ti
    ones_l = jnp.where(tril, 1.0, 0.0).astype(BF16)
    ones_u = jnp.where(ti <= si, 1.0, 0.0).astype(BF16)
    b_col = sum(_dot(ones_l, part) for part in _split3(_log_sigmoid(gates)))
    b_row = sum(_dot(part, ones_u) for part in _split3(_log_sigmoid(gates_t)))[M_HEADS:2 * M_HEADS]
    i_row = gates_t[0:M_HEADS]
    qk_w = M_HEADS * M_DK
    v_w = M_HEADS * M_DV
    for h in range(M_HEADS):
        bt = b_col[:, M_HEADS + h:M_HEADS + h + 1]
        bs = b_row[h:h + 1, :]
        i_s = i_row[h:h + 1, :]
        i_t = gates[:, h:h + 1]
        m_prev = m_scr[h:h + 1, 0:1]
        q = qkvo_ref[0, :, h * M_DK:(h + 1) * M_DK]
        k = qkvo_ref[0, :, qk_w + h * M_DK: qk_w + (h + 1) * M_DK]
        v = qkvo_ref[0, :, 2 * qk_w + h * M_DV: 2 * qk_w + (h + 1) * M_DV]
        og = qkvo_ref[0, :, 2 * qk_w + v_w + h * M_DV: 2 * qk_w + v_w + (h + 1) * M_DV].astype(F32)
        log_intra = jnp.where(tril, bt - bs + i_s, -jnp.inf)
        log_inter = bt + m_prev
        m_t = jnp.maximum(log_inter, jnp.max(log_intra, axis=-1, keepdims=True))
        w_inter = jnp.exp(log_inter - m_t)
        sc = _dot_nt(q, k) * jnp.exp(log_intra - m_t)
        c_st = c_scr[h]
        n_st = n_scr[h:h + 1, :]
        num = _dot(sc.astype(BF16), v) + w_inter * _dot_nt(q, c_st.astype(BF16))
        den = (jnp.sum(sc, axis=-1, keepdims=True)
               + w_inter * jnp.sum(q.astype(F32) * n_st, axis=-1, keepdims=True))
        hh = num / jnp.maximum(jnp.abs(den), jnp.exp(-m_t))
        y = hh * lax.rsqrt(jnp.mean(hh * hh, axis=-1, keepdims=True) + RMS_EPS)
        y = y * ng_ref[:, h * M_DV:(h + 1) * M_DV]
        out_ref[0, :, h * M_DV:(h + 1) * M_DV] = (y * _sigmoid(og)).astype(out_ref.dtype)
        m_new = m_t[lc - 1:lc, :]
        b_last = bt[lc - 1:lc, :]
        w_state = jnp.exp(b_last + m_prev - m_new)
        kw = k.astype(F32) * jnp.exp(b_last - bt + i_t - m_new)
        c_scr[h] = w_state * c_st + _dot_tn(v, kw.astype(BF16))
        n_scr[h:h + 1, :] = w_state * n_st + jnp.sum(kw, axis=0, keepdims=True)
        m_scr[h:h + 1, :] = jnp.broadcast_to(m_new, (1, LANES))

    @pl.when(ci == pl.num_programs(1) - 1)
    def _():
        cfin_ref[0] = c_scr[...]
        nfin_ref[0] = n_scr[...]
        mfin_ref[0] = m_scr[...]


def mlstm(qkvo, gates, c0, n0, m0, norm_g):
    b, s, w = qkvo.shape
    lc = _pick(s, (256, 128, 64))
    v_w = M_HEADS * M_DV
    gates_t = jnp.swapaxes(gates[:, :, :2 * M_HEADS], 1, 2)
    kern = functools.partial(_mlstm_kernel, lc=lc)
    c_spec = pl.BlockSpec((1, M_HEADS, M_DV, M_DK), lambda bi, ci: (bi, 0, 0, 0))
    n_spec = pl.BlockSpec((1, M_HEADS, M_DK), lambda bi, ci: (bi, 0, 0))
    m_spec = pl.BlockSpec((1, M_HEADS, LANES), lambda bi, ci: (bi, 0, 0))
    return pl.pallas_call(
        kern,
        grid=(b, s // lc),
        in_specs=[pl.BlockSpec((1, lc, w), lambda bi, ci: (bi, ci, 0)),
                  pl.BlockSpec((1, lc, LANES), lambda bi, ci: (bi, ci, 0)),
                  pl.BlockSpec((1, 2 * M_HEADS, lc), lambda bi, ci: (bi, 0, ci)),
                  c_spec, n_spec, m_spec,
                  pl.BlockSpec((1, v_w), lambda bi, ci: (0, 0))],
        out_specs=[pl.BlockSpec((1, lc, v_w), lambda bi, ci: (bi, ci, 0)), c_spec, n_spec, m_spec],
        out_shape=[jax.ShapeDtypeStruct((b, s, v_w), BF16),
                   jax.ShapeDtypeStruct((b, M_HEADS, M_DV, M_DK), F32),
                   jax.ShapeDtypeStruct((b, M_HEADS, M_DK), F32),
                   jax.ShapeDtypeStruct((b, M_HEADS, LANES), F32)],
        scratch_shapes=[pltpu.VMEM((M_HEADS, M_DV, M_DK), F32), pltpu.VMEM((M_HEADS, M_DK), F32),
                        pltpu.VMEM((M_HEADS, LANES), F32)],
        compiler_params=_cparams(("parallel", "arbitrary"), 40),
        name="mlstm",
    )(qkvo, gates, gates_t, c0, n0, m0, norm_g.reshape(1, v_w).astype(F32))


def _residual_out(acc, x_ref, gp_ref, gn_ref, xo_ref, ho_ref):
    xn = x_ref[...] + _rms(acc, gp_ref[...])
    xo_ref[...] = xn
    if ho_ref is not None:
        ho_ref[...] = _rms(xn, gn_ref[...]).astype(ho_ref.dtype)


def _outproj_kernel(*refs, n_in):
    a_refs, w_refs = refs[:n_in], refs[n_in:2 * n_in]
    x_ref, gp_ref, gn_ref, xo_ref, ho_ref = refs[2 * n_in:]
    acc = _dot(a_refs[0][...], w_refs[0][...])
    for a_ref, w_ref in zip(a_refs[1:], w_refs[1:]):
        acc = acc + _dot(a_ref[...], w_ref[...])
    _residual_out(acc, x_ref, gp_ref, gn_ref, xo_ref, ho_ref)


def outproj(acts, ws, x, g_post, g_next):
    m, d = x.shape
    tm = _pick(m, (256, 128, 64))
    n_in = len(acts)
    full = lambda a: pl.BlockSpec(a.shape, lambda i: (0,) * a.ndim)
    row = lambda w: pl.BlockSpec((tm, w), lambda i: (i, 0))
    vec = pl.BlockSpec((1, d), lambda i: (0, 0))
    return pl.pallas_call(
        functools.partial(_outproj_kernel, n_in=n_in),
        grid=(m // tm,),
        in_specs=[row(a.shape[1]) for a in acts] + [full(w) for w in ws] + [row(d), vec, vec],
        out_specs=[row(d), row(d)],
        out_shape=[jax.ShapeDtypeStruct((m, d), F32), jax.ShapeDtypeStruct((m, d), BF16)],
        compiler_params=_cparams(("parallel",), 48),
        name="outproj",
    )(*acts, *ws, x, g_post.reshape(1, d), g_next.reshape(1, d))


def _xattn_kernel(h_ref, wq_ref, mk_ref, mv_ref, wo_ref, x_ref, gp_ref, gn_ref, xo_ref, ho_ref):
    q = _dot(h_ref[...], wq_ref[...]).astype(BF16)
    mk, mv = mk_ref[0], mv_ref[0]
    scale = X_DH ** -0.5
    outs = []
    for hd in range(X_HEADS):
        sl = slice(hd * X_DH, (hd + 1) * X_DH)
        s = _dot_nt(q[:, sl], mk[:, sl]) * scale
        p = jnp.exp(s - jnp.max(s, axis=-1, keepdims=True))
        p = p / jnp.sum(p, axis=-1, keepdims=True)
        outs.append(_dot(p.astype(BF16), mv[:, sl]).astype(BF16))
    o = jnp.concatenate(outs, axis=-1)
    _residual_out(_dot(o, wo_ref[...]), x_ref, gp_ref, gn_ref, xo_ref, ho_ref)


def xattn_block(h, x, mem_k, mem_v, wq, wo, g_post, g_next, seq):
    m, d = x.shape
    tm = _pick(seq, (256, 128, 64))
    per_b = seq // tm
    t, xw = mem_k.shape[1], mem_k.shape[2]
    full = lambda a: pl.BlockSpec(a.shape, lambda i: (0,) * a.ndim)
    row = lambda w: pl.BlockSpec((tm, w), lambda i: (i, 0))
    mem = pl.BlockSpec((1, t, xw), lambda i: (i // per_b, 0, 0))
    vec = pl.BlockSpec((1, d), lambda i: (0, 0))
    return pl.pallas_call(
        _xattn_kernel,
        grid=(m // tm,),
        in_specs=[row(d), full(wq), mem, mem, full(wo), row(d), vec, vec],
        out_specs=[row(d), row(d)],
        out_shape=[jax.ShapeDtypeStruct((m, d), F32), jax.ShapeDtypeStruct((m, d), BF16)],
        compiler_params=_cparams(("parallel",), 48),
        name="xattn_block",
    )(h, wq, mem_k, mem_v, wo, x, g_post.reshape(1, d), g_next.reshape(1, d))


def _mlp_kernel(h_ref, w1_ref, w2_ref, x_ref, gp_ref, gn_ref, xo_ref, *rest, emit_h):
    ho_ref, acc_scr = (rest[0], rest[1]) if emit_h else (None, rest[0])
    f = pl.program_id(1)
    a = jnp.maximum(_dot(h_ref[...], w1_ref[...]), 0.0)
    part = _dot((a * a).astype(BF16), w2_ref[...])

    @pl.when(f == 0)
    def _():
        acc_scr[...] = part

    @pl.when(f > 0)
    def _():
        acc_scr[...] += part

    @pl.when(f == pl.num_programs(1) - 1)
    def _():
        _residual_out(acc_scr[...], x_ref, gp_ref, gn_ref, xo_ref, ho_ref)


def mlp_block(h, x, w1, w2, g_post, g_next):
    m, d = x.shape
    ff = w1.shape[1]
    tm = _pick(m, (512, 256, 128, 64))
    tf = _pick(ff, (512, 256, 128))
    emit_h = g_next is not None
    row = pl.BlockSpec((tm, d), lambda i, f: (i, 0))
    vec = pl.BlockSpec((1, d), lambda i, f: (0, 0))
    out_specs = [row] + ([row] if emit_h else [])
    out_shape = [jax.ShapeDtypeStruct((m, d), F32)] + ([jax.ShapeDtypeStruct((m, d), BF16)] if emit_h else [])
    g_n = (g_next if emit_h else g_post).reshape(1, d)
    res = pl.pallas_call(
        functools.partial(_mlp_kernel, emit_h=emit_h),
        grid=(m // tm, ff // tf),
        in_specs=[row, pl.BlockSpec((d, tf), lambda i, f: (0, f)), pl.BlockSpec((tf, d), lambda i, f: (f, 0)),
                  row, vec, vec],
        out_specs=out_specs,
        out_shape=out_shape,
        scratch_shapes=[pltpu.VMEM((tm, d), F32)],
        compiler_params=_cparams(("parallel", "arbitrary"), 56),
        name="mlp_block",
    )(h, w1, w2, x, g_post.reshape(1, d), g_n)
    return (res[0], res[1]) if emit_h else (res[0], None)


def _rope_tables(pos):
    def angles(half):
        inv = jnp.power(ROPE_BASE, -jnp.arange(half, dtype=F32) / half)
        return pos.astype(F32)[:, None] * inv[None, :]
    a = angles(RET_DK // 2)
    c, s = jnp.cos(a), jnp.sin(a)
    t128 = (jnp.concatenate([c, c], axis=1), jnp.concatenate([-s, s], axis=1))
    a = angles(MLA_ROPE // 2)
    c, s = jnp.cos(a), jnp.sin(a)
    z = jnp.zeros((pos.shape[0], LANES - MLA_ROPE), F32)
    t64 = (jnp.concatenate([c, c, z], axis=1), jnp.concatenate([s, s, z], axis=1))
    return t128, t64


def _prep_weights(w_in_even, w_uq, w_ukv, w_out_even, w_in_odd, b_gates_odd, w_out_odd, w_xq, w_xo, w_mlp1, w_mlp2):
    d = w_in_even.shape[1]
    ret_w = RET_HEADS * RET_DK
    n_qkvg = 2 * ret_w + 2 * RET_HEADS * RET_DV
    n_lat = MLA_Q_LORA + MLA_KV_LORA + MLA_ROPE
    even = []
    for e in range(w_in_even.shape[0]):
        w = w_in_even[e]
        wlat = jnp.pad(w[:, n_qkvg:n_qkvg + n_lat], ((0, 0), (0, LANES - MLA_ROPE)))
        uq = w_uq[e].reshape(MLA_Q_LORA, MLA_HEADS, MLA_NOPE + MLA_ROPE)
        uq_nope = uq[:, :, :MLA_NOPE].reshape(MLA_Q_LORA, MLA_HEADS * MLA_NOPE)
        uq_rope = jnp.pad(uq[:, :, MLA_NOPE:], ((0, 0), (0, 0), (0, LANES - MLA_ROPE)))
        uq_rope = uq_rope.reshape(MLA_Q_LORA, MLA_HEADS * LANES)
        ukv = w_ukv[e].reshape(MLA_KV_LORA, MLA_HEADS, MLA_NOPE + MLA_V)
        even.append(dict(
            w_qkvg=w[:, :n_qkvg].astype(BF16),
            w_lat=wlat.astype(BF16),
            w_uq=jnp.concatenate([uq_nope, uq_rope], axis=1).astype(BF16),
            w_uk=ukv[:, :, :MLA_NOPE].reshape(MLA_KV_LORA, MLA_HEADS * MLA_NOPE).astype(BF16),
            w_uv=ukv[:, :, MLA_NOPE:].reshape(MLA_KV_LORA, MLA_HEADS * MLA_V).astype(BF16),
            w_out_ret=w_out_even[e][:RET_HEADS * RET_DV].astype(BF16),
            w_out_att=w_out_even[e][RET_HEADS * RET_DV:].astype(BF16),
        ))
    m_qk = M_HEADS * M_DK
    n_qkvo = 2 * m_qk + 2 * M_HEADS * M_DV
    odd = []
    for o in range(w_in_odd.shape[0]):
        w = w_in_odd[o]
        odd.append(dict(
            w_qkvo=w[:, :n_qkvo].astype(BF16),
            w_gates=jnp.pad(w[:, n_qkvo:], ((0, 0), (0, LANES - 2 * M_HEADS))).astype(BF16),
            b_gates=jnp.pad(b_gates_odd[o], (0, LANES - 2 * M_HEADS)),
            w_out=w_out_odd[o].astype(BF16),
        ))
    q_scale = jnp.concatenate([jnp.full((m_qk,), M_DK ** -0.5, F32), jnp.ones((n_qkvo - m_qk,), F32)])
    rk_scale = jnp.concatenate([jnp.ones((ret_w,), F32), jnp.full((ret_w,), RET_DK ** -0.5, F32),
                                jnp.ones((n_qkvg - 2 * ret_w,), F32)])
    shared = dict(w_xq=w_xq.astype(BF16), w_xo=w_xo.astype(BF16), w_mlp1=w_mlp1.astype(BF16),
                  w_mlp2=w_mlp2.astype(BF16), q_scale=q_scale, rk_scale=rk_scale, d=d)
    return even, odd, shared


def _run_trunk(x, pos, mem_k, mem_v, ret_s0, c0, n0, m0, past, norm_g, gains, even_w, odd_w, shared):
    b, s, d = x.shape
    m = b * s
    depth = norm_g.shape[0]
    t128, t64 = _rope_tables(pos)
    if m % s or s % 64:
        raise ValueError("unsupported sequence length")
    if s < 256:
        t128 = tuple(jnp.tile(t, (b, 1)) for t in t128)
        t64 = tuple(jnp.tile(t, (b, 1)) for t in t64)
    xf = x.reshape(m, d)
    h = rms_rows(xf, norm_g[0, 0], BF16)
    ckvs, krs, rets, cs, ns, ms = [], [], [], [], [], []
    for layer in range(depth):
        g = norm_g[layer]
        if layer % 2 == 0:
            e = layer // 2
            w = even_w[e]
            qkvg = matmul(h, w["w_qkvg"], BF16, col_scale=shared["rk_scale"], rope=t128,
                          rope_cols=2 * RET_HEADS * RET_DK)
            ret, ret_s = retention(qkvg.reshape(b, s, -1), ret_s0[e], gains["ret_gn_g"][e])
            qcat, kcat, v, ckv, kr = even_latent(h, w["w_lat"], gains["mla_q_norm_g"][e].reshape(1, -1),
                                                 gains["mla_kv_norm_g"][e].reshape(1, -1),
                                                 w["w_uq"], w["w_uk"], w["w_uv"], t64)
            kcat = kcat.reshape(b, s, -1)
            v = v.reshape(b, s, -1)
            if past is None:
                q_off, sk = 0, s
            else:
                p_ckv, p_krp = past[e]
                p_len = p_ckv.shape[1]
                pk, pv = past_kv(p_ckv.reshape(b * p_len, -1), p_krp.reshape(b * p_len, -1), w["w_uk"], w["w_uv"])
                q_off, sk = p_len, p_len + s
                pad = (-sk) % 256
                kcat = jnp.concatenate([pk.reshape(b, p_len, -1), kcat,
                                        jnp.zeros((b, pad, kcat.shape[-1]), BF16)], axis=1)
                v = jnp.concatenate([pv.reshape(b, p_len, -1), v, jnp.zeros((b, pad, v.shape[-1]), BF16)], axis=1)
            att = mla_attention(qcat.reshape(b, s, -1), kcat, v, q_off=q_off, sk_valid=sk)
            xf, h = outproj([ret.reshape(m, -1), att.reshape(m, -1)], [w["w_out_ret"], w["w_out_att"]],
                            xf, g[1], g[2])
            rets.append(ret_s)
            ckvs.append(ckv.reshape(b, s, -1))
            krs.append(kr.reshape(b, s, -1))
        else:
            o = layer // 2
            w = odd_w[o]
            qkvo = matmul(h, w["w_qkvo"], BF16, col_scale=shared["q_scale"])
            gates = matmul(h, w["w_gates"], F32, col_bias=w["b_gates"])
            hh, c, n, mm = mlstm(qkvo.reshape(b, s, -1), gates.reshape(b, s, -1), c0[o], n0[o],
                                 jnp.broadcast_to(m0[o][..., None], m0[o].shape + (LANES,)),
                                 gains["mlstm_norm_g"][o])
            xf, h = outproj([hh.reshape(m, -1)], [w["w_out"]], xf, g[1], g[2])
            cs.append(c)
            ns.append(n)
            ms.append(mm[..., 0])
        xf, h = xattn_block(h, xf, mem_k[layer], mem_v[layer], shared["w_xq"][layer], shared["w_xo"][layer],
                            g[3], g[4], s)
        g_next = norm_g[layer + 1, 0] if layer + 1 < depth else None
        xf, h = mlp_block(h, xf, shared["w_mlp1"][layer], shared["w_mlp2"][layer], g[5], g_next)
    return (xf.reshape(b, s, d), jnp.stack(ckvs), jnp.stack(krs), jnp.stack(rets),
            jnp.stack(cs), jnp.stack(ns), jnp.stack(ms))


def kernel(x_prompt, x_sample, cache_mla_ckv, cache_mla_krope, state_ret, state_mlstm_C, state_mlstm_n, state_mlstm_m, cache_mem_k, cache_mem_v, mem_prompt, norm_g, mem_norm_g, w_in_even, mla_q_norm_g, mla_kv_norm_g, w_uq, w_ukv, ret_gn_g, w_out_even, w_in_odd, b_gates_odd, mlstm_norm_g, w_out_odd, w_xq, w_xk, w_xv, w_xo, w_mlp1, w_mlp2):
    even_w, odd_w, shared = _prep_weights(w_in_even, w_uq, w_ukv, w_out_even, w_in_odd, b_gates_odd, w_out_odd,
                                          w_xq, w_xo, w_mlp1, w_mlp2)
    gains = dict(ret_gn_g=ret_gn_g, mla_q_norm_g=mla_q_norm_g, mla_kv_norm_g=mla_kv_norm_g,
                 mlstm_norm_g=mlstm_norm_g)
    depth = norm_g.shape[0]
    b, s, d = x_prompt.shape
    n_even, n_odd = w_in_even.shape[0], w_in_odd.shape[0]
    xw = X_HEADS * X_DH

    bm, t, _ = mem_prompt.shape
    mem_flat = mem_prompt.reshape(bm * t, d)
    p_mem_k, p_mem_v = [], []
    for layer in range(depth):
        mn = rms_rows(mem_flat, mem_norm_g[layer], BF16)
        kv = matmul(mn, jnp.concatenate([w_xk[layer], w_xv[layer]], axis=1).astype(BF16), F32)
        p_mem_k.append(kv[:, :xw].reshape(bm, t, X_HEADS, X_DH))
        p_mem_v.append(kv[:, xw:].reshape(bm, t, X_HEADS, X_DH))
    p_mem_k, p_mem_v = jnp.stack(p_mem_k), jnp.stack(p_mem_v)

    pos_p = jnp.arange(s, dtype=jnp.int32)
    zeros = lambda *shape: jnp.zeros(shape, F32)
    y_prompt, p_ckv, p_kr, p_ret, p_c, p_n, p_m = _run_trunk(
        x_prompt, pos_p, p_mem_k.reshape(depth, bm, t, xw).astype(BF16), p_mem_v.reshape(depth, bm, t, xw).astype(BF16),
        zeros(n_even, b, RET_HEADS, RET_DK, RET_DV), zeros(n_odd, b, M_HEADS, M_DV, M_DK),
        zeros(n_odd, b, M_HEADS, M_DK), zeros(n_odd, b, M_HEADS), None, norm_g, gains, even_w, odd_w, shared)

    db, ds, _ = x_sample.shape
    past_len = cache_mla_ckv.shape[2]
    pos_s = past_len + jnp.arange(ds, dtype=jnp.int32)
    past = [(cache_mla_ckv[e], jnp.pad(cache_mla_krope[e], ((0, 0), (0, 0), (0, LANES - MLA_ROPE))))
            for e in range(n_even)]
    y_sample, s_ckv, s_kr, s_ret, s_c, s_n, s_m = _run_trunk(
        x_sample, pos_s, cache_mem_k.reshape(depth, db, -1, xw).astype(BF16),
        cache_mem_v.reshape(depth, db, -1, xw).astype(BF16),
        state_ret, state_mlstm_C, state_mlstm_n, state_mlstm_m, past, norm_g, gains, even_w, odd_w, shared)

    return (y_prompt, y_sample, p_ckv, p_kr, p_ret, p_c, p_n, p_m, p_mem_k, p_mem_v,
            s_ckv, s_kr, s_ret, s_c, s_n, s_m)
```

```python
import functools

import jax
import jax.numpy as jnp
from jax import lax
from jax.experimental import pallas as pl
from jax.experimental.pallas import tpu as pltpu

F32 = jnp.float32
BF16 = jnp.bfloat16

RMS_EPS = 1e-6
ROPE_BASE = 10000.0
CHUNK = 64
CHUNK_SHIFT = 6
assert 1 << CHUNK_SHIFT == CHUNK

RET_HEADS, RET_DK, RET_DV = 8, 128, 128
MLA_HEADS, MLA_NOPE, MLA_ROPE, MLA_V = 8, 128, 64, 128
MLA_Q_LORA, MLA_KV_LORA = 768, 512
MLA_QK_PAD = 256
MLA_HEADS_PER_STEP = 2
MLA_Q_PRESCALE = (MLA_NOPE + MLA_ROPE) ** -0.5 * 1.4426950408889634
M_HEADS, M_DK, M_DV = 8, 128, 256
X_HEADS, X_DH = 4, 128
LANES = 128

V7X_VMEM_BYTES = 64 * 1024 * 1024
MIB = 1024 * 1024


def _cparams(semantics, vmem_mib):
    assert vmem_mib * MIB < V7X_VMEM_BYTES
    return pltpu.CompilerParams(dimension_semantics=semantics, vmem_limit_bytes=vmem_mib * MIB)


def _pick(n, cands):
    for c in cands:
        if c <= n and n % c == 0:
            return c
    raise ValueError(f"no tile for {n} in {cands}")


def _rms(x, g):
    ms = jnp.mean(x * x, axis=-1, keepdims=True)
    return x * lax.rsqrt(ms + RMS_EPS) * g


def _sigmoid(x):
    return 1.0 / (1.0 + jnp.exp(-x))


def _dot(a, b):
    return jnp.dot(a, b, preferred_element_type=F32)


def _dot_nt(a, b):
    return lax.dot_general(a, b, (((1,), (1,)), ((), ())), preferred_element_type=F32)


def _dot_tn(a, b):
    return lax.dot_general(a, b, (((0,), (0,)), ((), ())), preferred_element_type=F32)


def _rope128(x, cos2, sin2):
    return x * cos2 + pltpu.roll(x, 64, 1) * sin2


def _rope64(x, cos_p, sin_p):
    return x * cos_p + (pltpu.roll(x, 32, 1) - pltpu.roll(x, 96, 1)) * sin_p


def _norm_kernel(x_ref, g_ref, o_ref):
    o_ref[...] = _rms(x_ref[...].astype(F32), g_ref[...]).astype(o_ref.dtype)


def rms_rows(x, g, out_dtype):
    m, k = x.shape
    tm = _pick(m, (512, 256, 128, 64, 32, 16, 8))
    return pl.pallas_call(
        _norm_kernel,
        grid=(m // tm,),
        in_specs=[pl.BlockSpec((tm, k), lambda i: (i, 0)), pl.BlockSpec((1, k), lambda i: (0, 0))],
        out_specs=pl.BlockSpec((tm, k), lambda i: (i, 0)),
        out_shape=jax.ShapeDtypeStruct((m, k), out_dtype),
        compiler_params=_cparams(("parallel",), 32),
        name="rms_rows",
    )(x, g.reshape(1, k).astype(F32))


def _mm_kernel(*refs, has_scale, has_bias, n_rope, tn):
    it = iter(refs)
    h_ref, w_ref = next(it), next(it)
    scale_ref = next(it) if has_scale else None
    bias_ref = next(it) if has_bias else None
    cos_ref, sin_ref = (next(it), next(it)) if n_rope else (None, None)
    o_ref = next(it)
    acc = _dot(h_ref[...], w_ref[...])
    if has_scale:
        acc = acc * scale_ref[...]
    if has_bias:
        acc = acc + bias_ref[...]
    if n_rope:
        c, s = cos_ref[...], sin_ref[...]
        for t in range(tn // LANES):
            sl = slice(t * LANES, (t + 1) * LANES)
            o_ref[:, sl] = _rope128(acc[:, sl], c, s).astype(o_ref.dtype)
    else:
        o_ref[...] = acc.astype(o_ref.dtype)


def matmul(h, w, out_dtype, *, col_scale=None, col_bias=None, rope=None):
    m, k = h.shape
    n = w.shape[1]
    tm = _pick(m, (1024, 512, 256, 128, 64))
    tn = _pick(n, (512, 256, 128))
    args = [h, w]
    in_specs = [pl.BlockSpec((tm, k), lambda i, j: (i, 0)), pl.BlockSpec((k, tn), lambda i, j: (0, j))]
    for vec in (col_scale, col_bias):
        if vec is not None:
            args.append(vec.reshape(1, n).astype(F32))
            in_specs.append(pl.BlockSpec((1, tn), lambda i, j: (0, j)))
    if rope is not None:
        r = rope[0].shape[0]
        tm = _pick(m, tuple(c for c in (1024, 512, 256, 128, 64) if r % c == 0))
        in_specs[0] = pl.BlockSpec((tm, k), lambda i, j: (i, 0))
        nrb = r // tm
        for tab in rope:
            args.append(tab)
            in_specs.append(pl.BlockSpec((tm, LANES), lambda i, j: (i % nrb, 0)))
    kern = functools.partial(_mm_kernel, has_scale=col_scale is not None, has_bias=col_bias is not None,
                             n_rope=rope is not None, tn=tn)
    return pl.pallas_call(
        kern,
        grid=(m // tm, n // tn),
        in_specs=in_specs,
        out_specs=pl.BlockSpec((tm, tn), lambda i, j: (i, j)),
        out_shape=jax.ShapeDtypeStruct((m, n), out_dtype),
        compiler_params=_cparams(("parallel", "arbitrary"), 40),
        name="matmul",
    )(*args)


def _kv_expand(ckvn_bf16, krr_bf16, wuk_ref, wuv_ref, kcat_ref, v_ref):
    kn = _dot(ckvn_bf16, wuk_ref[...])
    for hd in range(MLA_HEADS):
        kcat_ref[:, hd * MLA_QK_PAD: hd * MLA_QK_PAD + MLA_NOPE] = (
            kn[:, hd * MLA_NOPE:(hd + 1) * MLA_NOPE].astype(BF16))
        kcat_ref[:, hd * MLA_QK_PAD + MLA_NOPE:(hd + 1) * MLA_QK_PAD] = krr_bf16
    v_ref[...] = _dot(ckvn_bf16, wuv_ref[...]).astype(BF16)


def _even_latent_kernel(h_ref, wlat_ref, gq_ref, gkv_ref, wuq_ref, wuk_ref, wuv_ref, cos_ref, sin_ref,
                        qcat_ref, kcat_ref, v_ref, ckv_ref, kr_ref):
    lat = _dot(h_ref[...], wlat_ref[...])
    cq = lat[:, :MLA_Q_LORA]
    ckv = lat[:, MLA_Q_LORA:MLA_Q_LORA + MLA_KV_LORA]
    krp = lat[:, MLA_Q_LORA + MLA_KV_LORA:]
    c, s = cos_ref[...], sin_ref[...]
    q = _dot(_rms(cq, gq_ref[...]).astype(BF16), wuq_ref[...]) * MLA_Q_PRESCALE
    nope_w = MLA_HEADS * MLA_NOPE
    for hd in range(MLA_HEADS):
        qcat_ref[:, hd * MLA_QK_PAD: hd * MLA_QK_PAD + MLA_NOPE] = (
            q[:, hd * MLA_NOPE:(hd + 1) * MLA_NOPE].astype(BF16))
        qr = q[:, nope_w + hd * LANES: nope_w + (hd + 1) * LANES]
        qcat_ref[:, hd * MLA_QK_PAD + MLA_NOPE:(hd + 1) * MLA_QK_PAD] = _rope64(qr, c, s).astype(BF16)
    ckvn = _rms(ckv, gkv_ref[...])
    ckv_ref[...] = ckvn
    krr = _rope64(krp, c, s)
    kr_ref[...] = krr[:, :MLA_ROPE]
    _kv_expand(ckvn.astype(BF16), krr.astype(BF16), wuk_ref, wuv_ref, kcat_ref, v_ref)


def even_latent(h, wlat, gq, gkv, wuq, wuk, wuv, rope64):
    m, d = h.shape
    r = rope64[0].shape[0]
    tm = _pick(m, tuple(c for c in (256, 128, 64) if r % c == 0))
    nrb = r // tm
    full = lambda a: pl.BlockSpec(a.shape, lambda i: (0,) * a.ndim)
    row = lambda w: pl.BlockSpec((tm, w), lambda i: (i, 0))
    tab = pl.BlockSpec((tm, LANES), lambda i: (i % nrb, 0))
    qk_w = MLA_HEADS * MLA_QK_PAD
    v_w = MLA_HEADS * MLA_V
    return pl.pallas_call(
        _even_latent_kernel,
        grid=(m // tm,),
        in_specs=[row(d), full(wlat), full(gq), full(gkv), full(wuq), full(wuk), full(wuv), tab, tab],
        out_specs=[row(qk_w), row(qk_w), row(v_w), row(MLA_KV_LORA), row(MLA_ROPE)],
        out_shape=[jax.ShapeDtypeStruct((m, qk_w), BF16), jax.ShapeDtypeStruct((m, qk_w), BF16),
                   jax.ShapeDtypeStruct((m, v_w), BF16), jax.ShapeDtypeStruct((m, MLA_KV_LORA), F32),
                   jax.ShapeDtypeStruct((m, MLA_ROPE), F32)],
        compiler_params=_cparams(("parallel",), 48),
        name="even_latent",
    )(h, wlat, gq, gkv, wuq, wuk, wuv, rope64[0], rope64[1])


def _past_kv_kernel(ckv_ref, krp_ref, wuk_ref, wuv_ref, kcat_ref, v_ref):
    _kv_expand(ckv_ref[...].astype(BF16), krp_ref[...].astype(BF16), wuk_ref, wuv_ref, kcat_ref, v_ref)


def past_kv(ckvn, kr_pad, wuk, wuv):
    m = ckvn.shape[0]
    tm = _pick(m, (512, 256, 128, 64))
    full = lambda a: pl.BlockSpec(a.shape, lambda i: (0,) * a.ndim)
    row = lambda w: pl.BlockSpec((tm, w), lambda i: (i, 0))
    qk_w = MLA_HEADS * MLA_QK_PAD
    v_w = MLA_HEADS * MLA_V
    return pl.pallas_call(
        _past_kv_kernel,
        grid=(m // tm,),
        in_specs=[row(MLA_KV_LORA), row(LANES), full(wuk), full(wuv)],
        out_specs=[row(qk_w), row(v_w)],
        out_shape=[jax.ShapeDtypeStruct((m, qk_w), BF16), jax.ShapeDtypeStruct((m, v_w), BF16)],
        compiler_params=_cparams(("parallel",), 32),
        name="past_kv",
    )(ckvn, kr_pad, wuk, wuv)


def _mla_kernel(q_ref, k_ref, v_ref, o_ref, m_scr, l_scr, acc_scr, *, tq, tk, hb, q_off, sk_valid):
    qi = pl.program_id(2)
    q0 = qi * tq
    cq_lo = (q_off + q0) // CHUNK
    cq_hi = (q_off + q0 + tq - 1) // CHUNK
    n_full = jnp.minimum(lax.div((cq_lo + 1) * CHUNK, tk), sk_valid // tk)
    n_vis = lax.div(jnp.minimum((cq_hi + 1) * CHUNK, sk_valid) + tk - 1, tk)
    m_scr[...] = jnp.full(m_scr.shape, -1e30, F32)
    l_scr[...] = jnp.zeros(l_scr.shape, F32)
    acc_scr[...] = jnp.zeros(acc_scr.shape, F32)

    def step(ki, masked):
        k0 = pl.multiple_of(ki * tk, tk)
        if masked:
            qpos = q_off + q0 + lax.broadcasted_iota(jnp.int32, (tq, tk), 0)
            kpos = k0 + lax.broadcasted_iota(jnp.int32, (tq, tk), 1)
            vis = jnp.logical_and((kpos >> CHUNK_SHIFT) <= (qpos >> CHUNK_SHIFT), kpos < sk_valid)
        for j in range(hb):
            q = q_ref[0, :, j * MLA_QK_PAD:(j + 1) * MLA_QK_PAD]
            k = k_ref[0, pl.ds(k0, tk), j * MLA_QK_PAD:(j + 1) * MLA_QK_PAD]
            v = v_ref[0, pl.ds(k0, tk), j * MLA_V:(j + 1) * MLA_V]
            s = _dot_nt(q, k)
            if masked:
                s = jnp.where(vis, s, -jnp.inf)
            m_prev = m_scr[j]
            m_new = jnp.maximum(m_prev, jnp.max(s, axis=-1, keepdims=True))
            alpha = jnp.exp2(m_prev - m_new)
            p = jnp.exp2(s - m_new[:, :1])
            l_scr[j] = alpha * l_scr[j] + jnp.sum(p, axis=-1, keepdims=True)
            acc_scr[j] = alpha * acc_scr[j] + _dot(p.astype(BF16), v)
            m_scr[j] = m_new

    def body_full(ki, carry):
        step(ki, False)
        return carry

    def body_masked(ki, carry):
        step(ki, True)
        return carry

    lax.fori_loop(0, n_full, body_full, 0)
    lax.fori_loop(n_full, n_vis, body_masked, 0)
    for j in range(hb):
        o_ref[0, :, j * MLA_V:(j + 1) * MLA_V] = (acc_scr[j] / l_scr[j]).astype(o_ref.dtype)


def mla_attention(qcat, kcat, v, *, q_off, sk_valid):
    b, sq, _ = qcat.shape
    skp = kcat.shape[1]
    tq = _pick(sq, (512, 256, 128, 64))
    tk = skp if tq * skp <= 512 * 512 else _pick(skp, (512, 256, 128))
    hb = MLA_HEADS_PER_STEP
    kern = functools.partial(_mla_kernel, tq=tq, tk=tk, hb=hb, q_off=q_off, sk_valid=sk_valid)
    return pl.pallas_call(
        kern,
        grid=(b, MLA_HEADS // hb, sq // tq),
        in_specs=[pl.BlockSpec((1, tq, hb * MLA_QK_PAD), lambda bi, h, qi: (bi, qi, h)),
                  pl.BlockSpec((1, skp, hb * MLA_QK_PAD), lambda bi, h, qi: (bi, 0, h)),
                  pl.BlockSpec((1, skp, hb * MLA_V), lambda bi, h, qi: (bi, 0, h))],
        out_specs=pl.BlockSpec((1, tq, hb * MLA_V), lambda bi, h, qi: (bi, qi, h)),
        out_shape=jax.ShapeDtypeStruct((b, sq, MLA_HEADS * MLA_V), BF16),
        scratch_shapes=[pltpu.VMEM((hb, tq, LANES), F32), pltpu.VMEM((hb, tq, LANES), F32),
                        pltpu.VMEM((hb, tq, MLA_V), F32)],
        compiler_params=_cparams(("parallel", "parallel", "arbitrary"), 48),
        name="mla_attention",
    )(qcat, kcat, v)


def _retention_kernel(lg_ref, qk_ref, vg_ref, s0_ref, gn_ref, out_ref, sfin_ref, state_scr, *, lc):
    ci = pl.program_id(1)

    @pl.when(ci == 0)
    def _():
        state_scr[...] = s0_ref[0]

    ti = lax.broadcasted_iota(jnp.int32, (lc, lc), 0)
    si = lax.broadcasted_iota(jnp.int32, (lc, lc), 1)
    causal = ti >= si
    dpos = jnp.maximum(ti - si, 0).astype(F32)
    tcol = lax.broadcasted_iota(jnp.int32, (lc, 1), 0).astype(F32)
    qk_w = RET_HEADS * RET_DK
    v_w = RET_HEADS * RET_DV
    for h in range(RET_HEADS):
        lg = lg_ref[h:h + 1, 0:1]
        q = qk_ref[0, :, h * RET_DK:(h + 1) * RET_DK]
        k = qk_ref[0, :, qk_w + h * RET_DK: qk_w + (h + 1) * RET_DK]
        v = vg_ref[0, :, h * RET_DV:(h + 1) * RET_DV]
        g = vg_ref[0, :, v_w + h * RET_DV: v_w + (h + 1) * RET_DV].astype(F32)
        st = state_scr[h]
        sc = _dot_nt(q, k) * jnp.where(causal, jnp.exp(lg * dpos), 0.0)
        out = _dot(sc.astype(BF16), v) + jnp.exp(lg * (tcol + 1.0)) * _dot(q, st.astype(BF16))
        kd = (k.astype(F32) * jnp.exp(lg * (lc - 1.0 - tcol))).astype(BF16)
        state_scr[h] = jnp.exp(lg * lc) * st + _dot_tn(kd, v)
        xc = out - jnp.mean(out, axis=-1, keepdims=True)
        y = xc * lax.rsqrt(jnp.mean(xc * xc, axis=-1, keepdims=True) + RMS_EPS)
        y = y * gn_ref[:, h * RET_DV:(h + 1) * RET_DV]
        out_ref[0, :, h * RET_DV:(h + 1) * RET_DV] = (g * _sigmoid(g) * y).astype(out_ref.dtype)

    @pl.when(ci == pl.num_programs(1) - 1)
    def _():
        sfin_ref[0] = state_scr[...]


def retention(qk, vg, s0, gn_g):
    b, s, _ = qk.shape
    lc = _pick(s, (256, 128, 64))
    log_g = jnp.log1p(-jnp.exp2(-5.0 - jnp.arange(RET_HEADS, dtype=F32)))
    lg_tab = jnp.broadcast_to(log_g[:, None], (RET_HEADS, LANES))
    v_w = RET_HEADS * RET_DV
    kern = functools.partial(_retention_kernel, lc=lc)
    st_spec = pl.BlockSpec((1, RET_HEADS, RET_DK, RET_DV), lambda bi, ci: (bi, 0, 0, 0))
    return pl.pallas_call(
        kern,
        grid=(b, s // lc),
        in_specs=[pl.BlockSpec((RET_HEADS, LANES), lambda bi, ci: (0, 0)),
                  pl.BlockSpec((1, lc, qk.shape[2]), lambda bi, ci: (bi, ci, 0)),
                  pl.BlockSpec((1, lc, vg.shape[2]), lambda bi, ci: (bi, ci, 0)),
                  st_spec,
                  pl.BlockSpec((1, v_w), lambda bi, ci: (0, 0))],
        out_specs=[pl.BlockSpec((1, lc, v_w), lambda bi, ci: (bi, ci, 0)), st_spec],
        out_shape=[jax.ShapeDtypeStruct((b, s, v_w), BF16),
                   jax.ShapeDtypeStruct((b, RET_HEADS, RET_DK, RET_DV), F32)],
        scratch_shapes=[pltpu.VMEM((RET_HEADS, RET_DK, RET_DV), F32)],
        compiler_params=_cparams(("parallel", "arbitrary"), 40),
        name="retention",
    )(lg_tab, qk, vg, s0, gn_g.reshape(1, v_w).astype(F32))


def _split3(x):
    hi = x.astype(BF16)
    r1 = x - hi.astype(F32)
    mid = r1.astype(BF16)
    lo = (r1 - mid.astype(F32)).astype(BF16)
    return hi, mid, lo


def _log_sigmoid(x):
    return jnp.minimum(x, 0.0) - jnp.log(1.0 + jnp.exp(-jnp.abs(x)))


def _mlstm_kernel(qkvo_ref, gates_ref, gates_t_ref, c0_ref, n0_ref, m0_ref, ng_ref,
                  out_ref, cfin_ref, nfin_ref, mfin_ref, c_scr, n_scr, m_scr, *, lc):
    ci = pl.program_id(1)

    @pl.when(ci == 0)
    def _():
        c_scr[...] = c0_ref[0]
        n_scr[...] = n0_ref[0]
        m_scr[...] = m0_ref[0]

    gates = gates_ref[0]
    gates_t = gates_t_ref[0]
    ti = lax.broadcasted_iota(jnp.int32, (lc, lc), 0)
    si = lax.broadcasted_iota(jnp.int32, (lc, lc), 1)
    tril = si <= ti
    ones_l = jnp.where(tril, 1.0, 0.0).astype(BF16)
    ones_u = jnp.where(ti <= si, 1.0, 0.0).astype(BF16)
    b_col = sum(_dot(ones_l, part) for part in _split3(_log_sigmoid(gates)))
    b_row = sum(_dot(part, ones_u) for part in _split3(_log_sigmoid(gates_t)))[M_HEADS:2 * M_HEADS]
    i_row = gates_t[0:M_HEADS]
    qk_w = M_HEADS * M_DK
    v_w = M_HEADS * M_DV
    for h in range(M_HEADS):
        bt = b_col[:, M_HEADS + h:M_HEADS + h + 1]
        bs = b_row[h:h + 1, :]
        i_s = i_row[h:h + 1, :]
        i_t = gates[:, h:h + 1]
        m_prev = m_scr[h:h + 1, 0:1]
        q = qkvo_ref[0, :, h * M_DK:(h + 1) * M_DK]
        k = qkvo_ref[0, :, qk_w + h * M_DK: qk_w + (h + 1) * M_DK]
        v = qkvo_ref[0, :, 2 * qk_w + h * M_DV: 2 * qk_w + (h + 1) * M_DV]
        og = qkvo_ref[0, :, 2 * qk_w + v_w + h * M_DV: 2 * qk_w + v_w + (h + 1) * M_DV].astype(F32)
        log_intra = jnp.where(tril, bt - bs + i_s, -jnp.inf)
        log_inter = bt + m_prev
        m_t = jnp.maximum(log_inter, jnp.max(log_intra, axis=-1, keepdims=True))
        w_inter = jnp.exp(log_inter - m_t)
        sc = _dot_nt(q, k) * jnp.exp(log_intra - m_t)
        c_st = c_scr[h]
        n_st = n_scr[h:h + 1, :]
        num = _dot(sc.astype(BF16), v) + w_inter * _dot_nt(q, c_st.astype(BF16))
        den = (jnp.sum(sc, axis=-1, keepdims=True)
               + w_inter * jnp.sum(q.astype(F32) * n_st, axis=-1, keepdims=True))
        hh = num / jnp.maximum(jnp.abs(den), jnp.exp(-m_t))
        y = hh * lax.rsqrt(jnp.mean(hh * hh, axis=-1, keepdims=True) + RMS_EPS)
        y = y * ng_ref[:, h * M_DV:(h + 1) * M_DV]
        out_ref[0, :, h * M_DV:(h + 1) * M_DV] = (y * _sigmoid(og)).astype(out_ref.dtype)
        m_new = m_t[lc - 1:lc, :]
        b_last = bt[lc - 1:lc, :]
        w_state = jnp.exp(b_last + m_prev - m_new)
        kw = k.astype(F32) * jnp.exp(b_last - bt + i_t - m_new)
        c_scr[h] = w_state * c_st + _dot_tn(v, kw.astype(BF16))
        n_scr[h:h + 1, :] = w_state * n_st + jnp.sum(kw, axis=0, keepdims=True)
        m_scr[h:h + 1, :] = jnp.broadcast_to(m_new, (1, LANES))

    @pl.when(ci == pl.num_programs(1) - 1)
    def _():
        cfin_ref[0] = c_scr[...]
        nfin_ref[0] = n_scr[...]
        mfin_ref[0] = m_scr[...]


def mlstm(qkvo, gates, c0, n0, m0, norm_g):
    b, s, w = qkvo.shape
    lc = _pick(s, (256, 128, 64))
    v_w = M_HEADS * M_DV
    gates_t = jnp.swapaxes(gates[:, :, :2 * M_HEADS], 1, 2)
    kern = functools.partial(_mlstm_kernel, lc=lc)
    c_spec = pl.BlockSpec((1, M_HEADS, M_DV, M_DK), lambda bi, ci: (bi, 0, 0, 0))
    n_spec = pl.BlockSpec((1, M_HEADS, M_DK), lambda bi, ci: (bi, 0, 0))
    m_spec = pl.BlockSpec((1, M_HEADS, LANES), lambda bi, ci: (bi, 0, 0))
    return pl.pallas_call(
        kern,
        grid=(b, s // lc),
        in_specs=[pl.BlockSpec((1, lc, w), lambda bi, ci: (bi, ci, 0)),
                  pl.BlockSpec((1, lc, LANES), lambda bi, ci: (bi, ci, 0)),
                  pl.BlockSpec((1, 2 * M_HEADS, lc), lambda bi, ci: (bi, 0, ci)),
                  c_spec, n_spec, m_spec,
                  pl.BlockSpec((1, v_w), lambda bi, ci: (0, 0))],
        out_specs=[pl.BlockSpec((1, lc, v_w), lambda bi, ci: (bi, ci, 0)), c_spec, n_spec, m_spec],
        out_shape=[jax.ShapeDtypeStruct((b, s, v_w), BF16),
                   jax.ShapeDtypeStruct((b, M_HEADS, M_DV, M_DK), F32),
                   jax.ShapeDtypeStruct((b, M_HEADS, M_DK), F32),
                   jax.ShapeDtypeStruct((b, M_HEADS, LANES), F32)],
        scratch_shapes=[pltpu.VMEM((M_HEADS, M_DV, M_DK), F32), pltpu.VMEM((M_HEADS, M_DK), F32),
                        pltpu.VMEM((M_HEADS, LANES), F32)],
        compiler_params=_cparams(("parallel", "arbitrary"), 40),
        name="mlstm",
    )(qkvo, gates, gates_t, c0, n0, m0, norm_g.reshape(1, v_w).astype(F32))


def _residual_out(acc, x_ref, gp_ref, gn_ref, xo_ref, ho_ref):
    xn = x_ref[...] + _rms(acc, gp_ref[...])
    xo_ref[...] = xn
    if ho_ref is not None:
        ho_ref[...] = _rms(xn, gn_ref[...]).astype(ho_ref.dtype)


def _outproj_kernel(*refs, n_in):
    a_refs, w_refs = refs[:n_in], refs[n_in:2 * n_in]
    x_ref, gp_ref, gn_ref, xo_ref, ho_ref = refs[2 * n_in:]
    acc = _dot(a_refs[0][...], w_refs[0][...])
    for a_ref, w_ref in zip(a_refs[1:], w_refs[1:]):
        acc = acc + _dot(a_ref[...], w_ref[...])
    _residual_out(acc, x_ref, gp_ref, gn_ref, xo_ref, ho_ref)


def outproj(acts, ws, x, g_post, g_next):
    m, d = x.shape
    tm = _pick(m, (256, 128, 64))
    n_in = len(acts)
    full = lambda a: pl.BlockSpec(a.shape, lambda i: (0,) * a.ndim)
    row = lambda w: pl.BlockSpec((tm, w), lambda i: (i, 0))
    vec = pl.BlockSpec((1, d), lambda i: (0, 0))
    return pl.pallas_call(
        functools.partial(_outproj_kernel, n_in=n_in),
        grid=(m // tm,),
        in_specs=[row(a.shape[1]) for a in acts] + [full(w) for w in ws] + [row(d), vec, vec],
        out_specs=[row(d), row(d)],
        out_shape=[jax.ShapeDtypeStruct((m, d), F32), jax.ShapeDtypeStruct((m, d), BF16)],
        compiler_params=_cparams(("parallel",), 48),
        name="outproj",
    )(*acts, *ws, x, g_post.reshape(1, d), g_next.reshape(1, d))


def _xattn_kernel(h_ref, wq_ref, mk_ref, mv_ref, wo_ref, x_ref, gp_ref, gn_ref, xo_ref, ho_ref):
    q = _dot(h_ref[...], wq_ref[...]).astype(BF16)
    mk, mv = mk_ref[0], mv_ref[0]
    scale = X_DH ** -0.5
    outs = []
    for hd in range(X_HEADS):
        sl = slice(hd * X_DH, (hd + 1) * X_DH)
        s = _dot_nt(q[:, sl], mk[:, sl]) * scale
        p = jnp.exp(s - jnp.max(s, axis=-1, keepdims=True))
        p = p / jnp.sum(p, axis=-1, keepdims=True)
        outs.append(_dot(p.astype(BF16), mv[:, sl]).astype(BF16))
    o = jnp.concatenate(outs, axis=-1)
    _residual_out(_dot(o, wo_ref[...]), x_ref, gp_ref, gn_ref, xo_ref, ho_ref)


def xattn_block(h, x, mem_k, mem_v, wq, wo, g_post, g_next, seq):
    m, d = x.shape
    tm = _pick(seq, (256, 128, 64))
    per_b = seq // tm
    t, xw = mem_k.shape[1], mem_k.shape[2]
    full = lambda a: pl.BlockSpec(a.shape, lambda i: (0,) * a.ndim)
    row = lambda w: pl.BlockSpec((tm, w), lambda i: (i, 0))
    mem = pl.BlockSpec((1, t, xw), lambda i: (i // per_b, 0, 0))
    vec = pl.BlockSpec((1, d), lambda i: (0, 0))
    return pl.pallas_call(
        _xattn_kernel,
        grid=(m // tm,),
        in_specs=[row(d), full(wq), mem, mem, full(wo), row(d), vec, vec],
        out_specs=[row(d), row(d)],
        out_shape=[jax.ShapeDtypeStruct((m, d), F32), jax.ShapeDtypeStruct((m, d), BF16)],
        compiler_params=_cparams(("parallel",), 48),
        name="xattn_block",
    )(h, wq, mem_k, mem_v, wo, x, g_post.reshape(1, d), g_next.reshape(1, d))


def _mlp_kernel(h_ref, w1_ref, w2_ref, x_ref, gp_ref, gn_ref, xo_ref, *rest, emit_h):
    ho_ref, acc_scr = (rest[0], rest[1]) if emit_h else (None, rest[0])
    f = pl.program_id(1)
    @pl.when(f == 0)
    def _():
        acc_scr[...] = jnp.zeros(acc_scr.shape, F32)

    a = jnp.maximum(_dot(h_ref[...], w1_ref[...]), 0.0)
    acc_scr[...] += _dot((a * a).astype(BF16), w2_ref[...])

    @pl.when(f == pl.num_programs(1) - 1)
    def _():
        _residual_out(acc_scr[...], x_ref, gp_ref, gn_ref, xo_ref, ho_ref)


def mlp_block(h, x, w1, w2, g_post, g_next):
    m, d = x.shape
    ff = w1.shape[1]
    tm = _pick(m, (512, 256, 128, 64))
    tf = _pick(ff, (512, 256, 128))
    emit_h = g_next is not None
    row = pl.BlockSpec((tm, d), lambda i, f: (i, 0))
    vec = pl.BlockSpec((1, d), lambda i, f: (0, 0))
    out_specs = [row] + ([row] if emit_h else [])
    out_shape = [jax.ShapeDtypeStruct((m, d), F32)] + ([jax.ShapeDtypeStruct((m, d), BF16)] if emit_h else [])
    g_n = (g_next if emit_h else g_post).reshape(1, d)
    res = pl.pallas_call(
        functools.partial(_mlp_kernel, emit_h=emit_h),
        grid=(m // tm, ff // tf),
        in_specs=[row, pl.BlockSpec((d, tf), lambda i, f: (0, f)), pl.BlockSpec((tf, d), lambda i, f: (f, 0)),
                  row, vec, vec],
        out_specs=out_specs,
        out_shape=out_shape,
        scratch_shapes=[pltpu.VMEM((tm, d), F32)],
        compiler_params=_cparams(("parallel", "arbitrary"), 56),
        name="mlp_block",
    )(h, w1, w2, x, g_post.reshape(1, d), g_n)
    return (res[0], res[1]) if emit_h else (res[0], None)


def _rope_tables(pos):
    def angles(half):
        inv = jnp.power(ROPE_BASE, -jnp.arange(half, dtype=F32) / half)
        return pos.astype(F32)[:, None] * inv[None, :]
    a = angles(RET_DK // 2)
    c, s = jnp.cos(a), jnp.sin(a)
    t128 = (jnp.concatenate([c, c], axis=1), jnp.concatenate([-s, s], axis=1))
    a = angles(MLA_ROPE // 2)
    c, s = jnp.cos(a), jnp.sin(a)
    z = jnp.zeros((pos.shape[0], LANES - MLA_ROPE), F32)
    t64 = (jnp.concatenate([c, c, z], axis=1), jnp.concatenate([s, s, z], axis=1))
    return t128, t64


def _prep_weights(w_in_even, w_uq, w_ukv, w_out_even, w_in_odd, b_gates_odd, w_out_odd, w_xq, w_xo, w_mlp1, w_mlp2):
    d = w_in_even.shape[1]
    ret_w = RET_HEADS * RET_DK
    n_qkvg = 2 * ret_w + 2 * RET_HEADS * RET_DV
    n_lat = MLA_Q_LORA + MLA_KV_LORA + MLA_ROPE
    even = []
    for e in range(w_in_even.shape[0]):
        w = w_in_even[e]
        wlat = jnp.pad(w[:, n_qkvg:n_qkvg + n_lat], ((0, 0), (0, LANES - MLA_ROPE)))
        uq = w_uq[e].reshape(MLA_Q_LORA, MLA_HEADS, MLA_NOPE + MLA_ROPE)
        uq_nope = uq[:, :, :MLA_NOPE].reshape(MLA_Q_LORA, MLA_HEADS * MLA_NOPE)
        uq_rope = jnp.pad(uq[:, :, MLA_NOPE:], ((0, 0), (0, 0), (0, LANES - MLA_ROPE)))
        uq_rope = uq_rope.reshape(MLA_Q_LORA, MLA_HEADS * LANES)
        ukv = w_ukv[e].reshape(MLA_KV_LORA, MLA_HEADS, MLA_NOPE + MLA_V)
        even.append(dict(
            w_qk=w[:, :2 * ret_w].astype(BF16),
            w_vg=w[:, 2 * ret_w:n_qkvg].astype(BF16),
            w_lat=wlat.astype(BF16),
            w_uq=jnp.concatenate([uq_nope, uq_rope], axis=1).astype(BF16),
            w_uk=ukv[:, :, :MLA_NOPE].reshape(MLA_KV_LORA, MLA_HEADS * MLA_NOPE).astype(BF16),
            w_uv=ukv[:, :, MLA_NOPE:].reshape(MLA_KV_LORA, MLA_HEADS * MLA_V).astype(BF16),
            w_out_ret=w_out_even[e][:RET_HEADS * RET_DV].astype(BF16),
            w_out_att=w_out_even[e][RET_HEADS * RET_DV:].astype(BF16),
        ))
    m_qk = M_HEADS * M_DK
    n_qkvo = 2 * m_qk + 2 * M_HEADS * M_DV
    odd = []
    for o in range(w_in_odd.shape[0]):
        w = w_in_odd[o]
        odd.append(dict(
            w_qkvo=w[:, :n_qkvo].astype(BF16),
            w_gates=jnp.pad(w[:, n_qkvo:], ((0, 0), (0, LANES - 2 * M_HEADS))).astype(BF16),
            b_gates=jnp.pad(b_gates_odd[o], (0, LANES - 2 * M_HEADS)),
            w_out=w_out_odd[o].astype(BF16),
        ))
    q_scale = jnp.concatenate([jnp.full((m_qk,), M_DK ** -0.5, F32), jnp.ones((n_qkvo - m_qk,), F32)])
    rk_scale = jnp.concatenate([jnp.ones((ret_w,), F32), jnp.full((ret_w,), RET_DK ** -0.5, F32)])
    shared = dict(w_xq=w_xq.astype(BF16), w_xo=w_xo.astype(BF16), w_mlp1=w_mlp1.astype(BF16),
                  w_mlp2=w_mlp2.astype(BF16), q_scale=q_scale, rk_scale=rk_scale, d=d)
    return even, odd, shared


def _run_trunk(x, pos, mem_k, mem_v, ret_s0, c0, n0, m0, past, norm_g, gains, even_w, odd_w, shared):
    b, s, d = x.shape
    m = b * s
    depth = norm_g.shape[0]
    t128, t64 = _rope_tables(pos)
    if m % s or s % 64:
        raise ValueError("unsupported sequence length")
    if s < 256:
        t128 = tuple(jnp.tile(t, (b, 1)) for t in t128)
        t64 = tuple(jnp.tile(t, (b, 1)) for t in t64)
    xf = x.reshape(m, d)
    h = rms_rows(xf, norm_g[0, 0], BF16)
    ckvs, krs, rets, cs, ns, ms = [], [], [], [], [], []
    for layer in range(depth):
        g = norm_g[layer]
        if layer % 2 == 0:
            e = layer // 2
            w = even_w[e]
            qk = matmul(h, w["w_qk"], BF16, col_scale=shared["rk_scale"], rope=t128)
            vg = matmul(h, w["w_vg"], BF16)
            ret, ret_s = retention(qk.reshape(b, s, -1), vg.reshape(b, s, -1), ret_s0[e], gains["ret_gn_g"][e])
            qcat, kcat, v, ckv, kr = even_latent(h, w["w_lat"], gains["mla_q_norm_g"][e].reshape(1, -1),
                                                 gains["mla_kv_norm_g"][e].reshape(1, -1),
                                                 w["w_uq"], w["w_uk"], w["w_uv"], t64)
            kcat = kcat.reshape(b, s, -1)
            v = v.reshape(b, s, -1)
            if past is None:
                q_off, sk = 0, s
            else:
                p_ckv, p_krp = past[e]
                p_len = p_ckv.shape[1]
                pk, pv = past_kv(p_ckv.reshape(b * p_len, -1), p_krp.reshape(b * p_len, -1), w["w_uk"], w["w_uv"])
                q_off, sk = p_len, p_len + s
                pad = (-sk) % 256
                kcat = jnp.concatenate([pk.reshape(b, p_len, -1), kcat,
                                        jnp.zeros((b, pad, kcat.shape[-1]), BF16)], axis=1)
                v = jnp.concatenate([pv.reshape(b, p_len, -1), v, jnp.zeros((b, pad, v.shape[-1]), BF16)], axis=1)
            att = mla_attention(qcat.reshape(b, s, -1), kcat, v, q_off=q_off, sk_valid=sk)
            xf, h = outproj([ret.reshape(m, -1), att.reshape(m, -1)], [w["w_out_ret"], w["w_out_att"]],
                            xf, g[1], g[2])
            rets.append(ret_s)
            ckvs.append(ckv.reshape(b, s, -1))
            krs.append(kr.reshape(b, s, -1))
        else:
            o = layer // 2
            w = odd_w[o]
            qkvo = matmul(h, w["w_qkvo"], BF16, col_scale=shared["q_scale"])
            gates = matmul(h, w["w_gates"], F32, col_bias=w["b_gates"])
            hh, c, n, mm = mlstm(qkvo.reshape(b, s, -1), gates.reshape(b, s, -1), c0[o], n0[o],
                                 jnp.broadcast_to(m0[o][..., None], m0[o].shape + (LANES,)),
                                 gains["mlstm_norm_g"][o])
            xf, h = outproj([hh.reshape(m, -1)], [w["w_out"]], xf, g[1], g[2])
            cs.append(c)
            ns.append(n)
            ms.append(mm[..., 0])
        xf, h = xattn_block(h, xf, mem_k[layer], mem_v[layer], shared["w_xq"][layer], shared["w_xo"][layer],
                            g[3], g[4], s)
        g_next = norm_g[layer + 1, 0] if layer + 1 < depth else None
        xf, h = mlp_block(h, xf, shared["w_mlp1"][layer], shared["w_mlp2"][layer], g[5], g_next)
    return (xf.reshape(b, s, d), jnp.stack(ckvs), jnp.stack(krs), jnp.stack(rets),
            jnp.stack(cs), jnp.stack(ns), jnp.stack(ms))


def kernel(x_prompt, x_sample, cache_mla_ckv, cache_mla_krope, state_ret, state_mlstm_C, state_mlstm_n, state_mlstm_m, cache_mem_k, cache_mem_v, mem_prompt, norm_g, mem_norm_g, w_in_even, mla_q_norm_g, mla_kv_norm_g, w_uq, w_ukv, ret_gn_g, w_out_even, w_in_odd, b_gates_odd, mlstm_norm_g, w_out_odd, w_xq, w_xk, w_xv, w_xo, w_mlp1, w_mlp2):
    even_w, odd_w, shared = _prep_weights(w_in_even, w_uq, w_ukv, w_out_even, w_in_odd, b_gates_odd, w_out_odd,
                                          w_xq, w_xo, w_mlp1, w_mlp2)
    gains = dict(ret_gn_g=ret_gn_g, mla_q_norm_g=mla_q_norm_g, mla_kv_norm_g=mla_kv_norm_g,
                 mlstm_norm_g=mlstm_norm_g)
    depth = norm_g.shape[0]
    b, s, d = x_prompt.shape
    n_even, n_odd = w_in_even.shape[0], w_in_odd.shape[0]
    xw = X_HEADS * X_DH

    bm, t, _ = mem_prompt.shape
    mem_flat = mem_prompt.reshape(bm * t, d)
    p_mem_k, p_mem_v = [], []
    for layer in range(depth):
        mn = rms_rows(mem_flat, mem_norm_g[layer], BF16)
        kv = matmul(mn, jnp.concatenate([w_xk[layer], w_xv[layer]], axis=1).astype(BF16), F32)
        p_mem_k.append(kv[:, :xw].reshape(bm, t, X_HEADS, X_DH))
        p_mem_v.append(kv[:, xw:].reshape(bm, t, X_HEADS, X_DH))
    p_mem_k, p_mem_v = jnp.stack(p_mem_k), jnp.stack(p_mem_v)

    pos_p = jnp.arange(s, dtype=jnp.int32)
    zeros = lambda *shape: jnp.zeros(shape, F32)
    y_prompt, p_ckv, p_kr, p_ret, p_c, p_n, p_m = _run_trunk(
        x_prompt, pos_p, p_mem_k.reshape(depth, bm, t, xw).astype(BF16), p_mem_v.reshape(depth, bm, t, xw).astype(BF16),
        zeros(n_even, b, RET_HEADS, RET_DK, RET_DV), zeros(n_odd, b, M_HEADS, M_DV, M_DK),
        zeros(n_odd, b, M_HEADS, M_DK), zeros(n_odd, b, M_HEADS), None, norm_g, gains, even_w, odd_w, shared)

    db, ds, _ = x_sample.shape
    past_len = cache_mla_ckv.shape[2]
    pos_s = past_len + jnp.arange(ds, dtype=jnp.int32)
    past = [(cache_mla_ckv[e], jnp.pad(cache_mla_krope[e], ((0, 0), (0, 0), (0, LANES - MLA_ROPE))))
            for e in range(n_even)]
    y_sample, s_ckv, s_kr, s_ret, s_c, s_n, s_m = _run_trunk(
        x_sample, pos_s, cache_mem_k.reshape(depth, db, -1, xw).astype(BF16),
        cache_mem_v.reshape(depth, db, -1, xw).astype(BF16),
        state_ret, state_mlstm_C, state_mlstm_n, state_mlstm_m, past, norm_g, gains, even_w, odd_w, shared)

    return (y_prompt, y_sample, p_ckv, p_kr, p_ret, p_c, p_n, p_m, p_mem_k, p_mem_v,
            s_ckv, s_kr, s_ret, s_c, s_n, s_m)
```

```python
import functools

import jax
import jax.numpy as jnp
from jax import lax
from jax.experimental import pallas as pl
from jax.experimental.pallas import tpu as pltpu

F32 = jnp.float32
BF16 = jnp.bfloat16

RMS_EPS = 1e-6
ROPE_BASE = 10000.0
CHUNK = 64
CHUNK_SHIFT = 6
assert 1 << CHUNK_SHIFT == CHUNK

RET_HEADS, RET_DK, RET_DV = 8, 128, 128
MLA_HEADS, MLA_NOPE, MLA_ROPE, MLA_V = 8, 128, 64, 128
MLA_Q_LORA, MLA_KV_LORA = 768, 512
MLA_QK_PAD = 256
MLA_HEADS_PER_STEP = 2
MLA_Q_PRESCALE = (MLA_NOPE + MLA_ROPE) ** -0.5 * 1.4426950408889634
M_HEADS, M_DK, M_DV = 8, 128, 256
X_HEADS, X_DH = 4, 128
LANES = 128
ROW_CHAINS = 2

V7X_VMEM_BYTES = 64 * 1024 * 1024
MIB = 1024 * 1024


def _cparams(semantics, vmem_mib):
    assert vmem_mib * MIB < V7X_VMEM_BYTES
    return pltpu.CompilerParams(dimension_semantics=semantics, vmem_limit_bytes=vmem_mib * MIB)


def _pick(n, cands):
    for c in cands:
        if c <= n and n % c == 0:
            return c
    raise ValueError(f"no tile for {n} in {cands}")


def _rms(x, g):
    ms = jnp.mean(x * x, axis=-1, keepdims=True)
    return x * lax.rsqrt(ms + RMS_EPS) * g


def _sigmoid(x):
    return 1.0 / (1.0 + jnp.exp(-x))


def _dot(a, b):
    return jnp.dot(a, b, preferred_element_type=F32)


def _dot_nt(a, b):
    return lax.dot_general(a, b, (((1,), (1,)), ((), ())), preferred_element_type=F32)


def _dot_tn(a, b):
    return lax.dot_general(a, b, (((0,), (0,)), ((), ())), preferred_element_type=F32)


def _rope128(x, cos2, sin2):
    return x * cos2 + pltpu.roll(x, 64, 1) * sin2


def _rope64(x, cos_p, sin_p):
    return x * cos_p + (pltpu.roll(x, 32, 1) - pltpu.roll(x, 96, 1)) * sin_p


def _norm_kernel(x_ref, g_ref, o_ref):
    o_ref[...] = _rms(x_ref[...].astype(F32), g_ref[...]).astype(o_ref.dtype)


def rms_rows(x, g, out_dtype):
    m, k = x.shape
    tm = _pick(m, (512, 256, 128, 64, 32, 16, 8))
    return pl.pallas_call(
        _norm_kernel,
        grid=(m // tm,),
        in_specs=[pl.BlockSpec((tm, k), lambda i: (i, 0)), pl.BlockSpec((1, k), lambda i: (0, 0))],
        out_specs=pl.BlockSpec((tm, k), lambda i: (i, 0)),
        out_shape=jax.ShapeDtypeStruct((m, k), out_dtype),
        compiler_params=_cparams(("parallel",), 32),
        name="rms_rows",
    )(x, g.reshape(1, k).astype(F32))


def _mm_kernel(*refs, has_scale, has_bias, n_rope, tn):
    it = iter(refs)
    h_ref, w_ref = next(it), next(it)
    scale_ref = next(it) if has_scale else None
    bias_ref = next(it) if has_bias else None
    cos_ref, sin_ref = (next(it), next(it)) if n_rope else (None, None)
    o_ref = next(it)
    acc = _dot(h_ref[...], w_ref[...])
    if has_scale:
        acc = acc * scale_ref[...]
    if has_bias:
        acc = acc + bias_ref[...]
    if n_rope:
        c, s = cos_ref[...], sin_ref[...]
        for t in range(tn // LANES):
            sl = slice(t * LANES, (t + 1) * LANES)
            o_ref[:, sl] = _rope128(acc[:, sl], c, s).astype(o_ref.dtype)
    else:
        o_ref[...] = acc.astype(o_ref.dtype)


def matmul(h, w, out_dtype, *, col0=0, n_cols=None, col_scale=None, col_bias=None, rope=None):
    m, k = h.shape
    n = w.shape[1] - col0 if n_cols is None else n_cols
    tm = _pick(m, (1024, 512, 256, 128, 64))
    tn = _pick(n, (512, 256, 128))
    assert col0 % tn == 0 and col0 + n <= w.shape[1]
    jb = col0 // tn
    args = [h, w]
    in_specs = [pl.BlockSpec((tm, k), lambda i, j: (i, 0)), pl.BlockSpec((k, tn), lambda i, j: (0, j + jb))]
    for vec in (col_scale, col_bias):
        if vec is not None:
            args.append(vec.reshape(1, n).astype(F32))
            in_specs.append(pl.BlockSpec((1, tn), lambda i, j: (0, j)))
    if rope is not None:
        r = rope[0].shape[0]
        tm = _pick(m, tuple(c for c in (1024, 512, 256, 128, 64) if r % c == 0))
        in_specs[0] = pl.BlockSpec((tm, k), lambda i, j: (i, 0))
        nrb = r // tm
        for tab in rope:
            args.append(tab)
            in_specs.append(pl.BlockSpec((tm, LANES), lambda i, j: (i % nrb, 0)))
    kern = functools.partial(_mm_kernel, has_scale=col_scale is not None, has_bias=col_bias is not None,
                             n_rope=rope is not None, tn=tn)
    return pl.pallas_call(
        kern,
        grid=(m // tm, n // tn),
        in_specs=in_specs,
        out_specs=pl.BlockSpec((tm, tn), lambda i, j: (i, j)),
        out_shape=jax.ShapeDtypeStruct((m, n), out_dtype),
        compiler_params=_cparams(("parallel", "arbitrary"), 40),
        name="matmul",
    )(*args)


def _kv_expand(ckvn_bf16, krr_bf16, wuk_ref, wuv_ref, kcat_ref, v_ref):
    kn = _dot(ckvn_bf16, wuk_ref[...])
    for hd in range(MLA_HEADS):
        kcat_ref[:, hd * MLA_QK_PAD: hd * MLA_QK_PAD + MLA_NOPE] = (
            kn[:, hd * MLA_NOPE:(hd + 1) * MLA_NOPE].astype(BF16))
        kcat_ref[:, hd * MLA_QK_PAD + MLA_NOPE:(hd + 1) * MLA_QK_PAD] = krr_bf16
    v_ref[...] = _dot(ckvn_bf16, wuv_ref[...]).astype(BF16)


def _even_latent_kernel(h_ref, wlat_ref, gq_ref, gkv_ref, wuq_ref, wuk_ref, wuv_ref, cos_ref, sin_ref,
                        qcat_ref, kcat_ref, v_ref, ckv_ref, kr_ref):
    lat = _dot(h_ref[...], wlat_ref[...])
    cq = lat[:, :MLA_Q_LORA]
    ckv = lat[:, MLA_Q_LORA:MLA_Q_LORA + MLA_KV_LORA]
    krp = lat[:, MLA_Q_LORA + MLA_KV_LORA:]
    c, s = cos_ref[...], sin_ref[...]
    q = _dot(_rms(cq, gq_ref[...]).astype(BF16), wuq_ref[...]) * MLA_Q_PRESCALE
    nope_w = MLA_HEADS * MLA_NOPE
    for hd in range(MLA_HEADS):
        qcat_ref[:, hd * MLA_QK_PAD: hd * MLA_QK_PAD + MLA_NOPE] = (
            q[:, hd * MLA_NOPE:(hd + 1) * MLA_NOPE].astype(BF16))
        qr = q[:, nope_w + hd * LANES: nope_w + (hd + 1) * LANES]
        qcat_ref[:, hd * MLA_QK_PAD + MLA_NOPE:(hd + 1) * MLA_QK_PAD] = _rope64(qr, c, s).astype(BF16)
    ckvn = _rms(ckv, gkv_ref[...])
    ckv_ref[...] = ckvn
    krr = _rope64(krp, c, s)
    kr_ref[...] = krr[:, :MLA_ROPE]
    _kv_expand(ckvn.astype(BF16), krr.astype(BF16), wuk_ref, wuv_ref, kcat_ref, v_ref)


def even_latent(h, wlat, gq, gkv, wuq, wuk, wuv, rope64):
    m, d = h.shape
    r = rope64[0].shape[0]
    tm = _pick(m, tuple(c for c in (256, 128, 64) if r % c == 0))
    nrb = r // tm
    full = lambda a: pl.BlockSpec(a.shape, lambda i: (0,) * a.ndim)
    row = lambda w: pl.BlockSpec((tm, w), lambda i: (i, 0))
    tab = pl.BlockSpec((tm, LANES), lambda i: (i % nrb, 0))
    qk_w = MLA_HEADS * MLA_QK_PAD
    v_w = MLA_HEADS * MLA_V
    return pl.pallas_call(
        _even_latent_kernel,
        grid=(m // tm,),
        in_specs=[row(d), full(wlat), full(gq), full(gkv), full(wuq), full(wuk), full(wuv), tab, tab],
        out_specs=[row(qk_w), row(qk_w), row(v_w), row(MLA_KV_LORA), row(MLA_ROPE)],
        out_shape=[jax.ShapeDtypeStruct((m, qk_w), BF16), jax.ShapeDtypeStruct((m, qk_w), BF16),
                   jax.ShapeDtypeStruct((m, v_w), BF16), jax.ShapeDtypeStruct((m, MLA_KV_LORA), F32),
                   jax.ShapeDtypeStruct((m, MLA_ROPE), F32)],
        compiler_params=_cparams(("parallel",), 48),
        name="even_latent",
    )(h, wlat, gq, gkv, wuq, wuk, wuv, rope64[0], rope64[1])


def _past_kv_kernel(ckv_ref, krp_ref, wuk_ref, wuv_ref, kcat_ref, v_ref):
    _kv_expand(ckv_ref[...].astype(BF16), krp_ref[...].astype(BF16), wuk_ref, wuv_ref, kcat_ref, v_ref)


def past_kv(ckvn, kr_pad, wuk, wuv):
    m = ckvn.shape[0]
    tm = _pick(m, (512, 256, 128, 64))
    full = lambda a: pl.BlockSpec(a.shape, lambda i: (0,) * a.ndim)
    row = lambda w: pl.BlockSpec((tm, w), lambda i: (i, 0))
    qk_w = MLA_HEADS * MLA_QK_PAD
    v_w = MLA_HEADS * MLA_V
    return pl.pallas_call(
        _past_kv_kernel,
        grid=(m // tm,),
        in_specs=[row(MLA_KV_LORA), row(LANES), full(wuk), full(wuv)],
        out_specs=[row(qk_w), row(v_w)],
        out_shape=[jax.ShapeDtypeStruct((m, qk_w), BF16), jax.ShapeDtypeStruct((m, v_w), BF16)],
        compiler_params=_cparams(("parallel",), 32),
        name="past_kv",
    )(ckvn, kr_pad, wuk, wuv)


def _mla_kernel(q_ref, k_ref, v_ref, o_ref, m_scr, l_scr, acc_scr, a_scr, p_scr, *, tq, tk, hb, q_off, sk_valid):
    qi = pl.program_id(2)
    q0 = qi * tq
    cq_lo = (q_off + q0) // CHUNK
    cq_hi = (q_off + q0 + tq - 1) // CHUNK
    n_full = jnp.minimum(lax.div((cq_lo + 1) * CHUNK, tk), sk_valid // tk)
    n_vis = lax.div(jnp.minimum((cq_hi + 1) * CHUNK, sk_valid) + tk - 1, tk)
    m_scr[...] = jnp.full(m_scr.shape, -1e30, F32)
    l_scr[...] = jnp.zeros(l_scr.shape, F32)
    acc_scr[...] = jnp.zeros(acc_scr.shape, F32)
    a_scr[...] = jnp.ones(a_scr.shape, F32)
    p_scr[...] = jnp.zeros(p_scr.shape, BF16)

    def apply_prev(j, ki_prev):
        v = v_ref[0, pl.ds(pl.multiple_of(ki_prev * tk, tk), tk), j * MLA_V:(j + 1) * MLA_V]
        acc_scr[j] = a_scr[j] * acc_scr[j] + _dot(p_scr[j], v)

    def step(ki, masked):
        k0 = pl.multiple_of(ki * tk, tk)
        ki_prev = jnp.maximum(ki - 1, 0)
        if masked:
            qpos = q_off + q0 + lax.broadcasted_iota(jnp.int32, (tq, tk), 0)
            kpos = k0 + lax.broadcasted_iota(jnp.int32, (tq, tk), 1)
            vis = jnp.logical_and((kpos >> CHUNK_SHIFT) <= (qpos >> CHUNK_SHIFT), kpos < sk_valid)
        for j in range(hb):
            q = q_ref[0, :, j * MLA_QK_PAD:(j + 1) * MLA_QK_PAD]
            k = k_ref[0, pl.ds(k0, tk), j * MLA_QK_PAD:(j + 1) * MLA_QK_PAD]
            s = _dot_nt(q, k)
            apply_prev(j, ki_prev)
            if masked:
                s = jnp.where(vis, s, -jnp.inf)
            m_prev = m_scr[j]
            m_new = jnp.maximum(m_prev, jnp.max(s, axis=-1, keepdims=True))
            alpha = jnp.exp2(m_prev - m_new)
            p = [jnp.exp2(s[:, c * LANES:(c + 1) * LANES] - m_new) for c in range(tk // LANES)]
            l_scr[j] = alpha * l_scr[j] + sum(p[1:], p[0])
            p_scr[j] = jnp.concatenate([pc.astype(BF16) for pc in p], axis=-1)
            a_scr[j] = alpha
            m_scr[j] = m_new

    def body_full(ki, carry):
        step(ki, False)
        return carry

    def body_masked(ki, carry):
        step(ki, True)
        return carry

    lax.fori_loop(0, n_full, body_full, 0)
    lax.fori_loop(n_full, n_vis, body_masked, 0)
    for j in range(hb):
        apply_prev(j, n_vis - 1)
        l = jnp.sum(l_scr[j], axis=-1, keepdims=True)
        o_ref[0, :, j * MLA_V:(j + 1) * MLA_V] = (acc_scr[j] / l).astype(o_ref.dtype)


def mla_attention(qcat, kcat, v, *, q_off, sk_valid):
    b, sq, _ = qcat.shape
    skp = kcat.shape[1]
    tq = _pick(sq, (512, 256, 128, 64))
    tk = skp if tq * skp <= 512 * 512 else _pick(skp, (512, 256, 128))
    hb = MLA_HEADS_PER_STEP
    kern = functools.partial(_mla_kernel, tq=tq, tk=tk, hb=hb, q_off=q_off, sk_valid=sk_valid)
    return pl.pallas_call(
        kern,
        grid=(b, MLA_HEADS // hb, sq // tq),
        in_specs=[pl.BlockSpec((1, tq, hb * MLA_QK_PAD), lambda bi, h, qi: (bi, qi, h)),
                  pl.BlockSpec((1, skp, hb * MLA_QK_PAD), lambda bi, h, qi: (bi, 0, h)),
                  pl.BlockSpec((1, skp, hb * MLA_V), lambda bi, h, qi: (bi, 0, h))],
        out_specs=pl.BlockSpec((1, tq, hb * MLA_V), lambda bi, h, qi: (bi, qi, h)),
        out_shape=jax.ShapeDtypeStruct((b, sq, MLA_HEADS * MLA_V), BF16),
        scratch_shapes=[pltpu.VMEM((hb, tq, LANES), F32), pltpu.VMEM((hb, tq, LANES), F32),
                        pltpu.VMEM((hb, tq, MLA_V), F32), pltpu.VMEM((hb, tq, LANES), F32),
                        pltpu.VMEM((hb, tq, tk), BF16)],
        compiler_params=_cparams(("parallel", "parallel", "arbitrary"), 48),
        name="mla_attention",
    )(qcat, kcat, v)


def _retention_kernel(lg_ref, qk_ref, vg_ref, s0_ref, gn_ref, out_ref, sfin_ref, state_scr, *, lc):
    ci = pl.program_id(1)

    @pl.when(ci == 0)
    def _():
        state_scr[...] = s0_ref[0]

    ti = lax.broadcasted_iota(jnp.int32, (lc, lc), 0)
    si = lax.broadcasted_iota(jnp.int32, (lc, lc), 1)
    causal = ti >= si
    dpos = jnp.maximum(ti - si, 0).astype(F32)
    tcol = lax.broadcasted_iota(jnp.int32, (lc, 1), 0).astype(F32)
    qk_w = RET_HEADS * RET_DK
    v_w = RET_HEADS * RET_DV
    for h in range(RET_HEADS):
        lg = lg_ref[h:h + 1, 0:1]
        q = qk_ref[0, :, h * RET_DK:(h + 1) * RET_DK]
        k = qk_ref[0, :, qk_w + h * RET_DK: qk_w + (h + 1) * RET_DK]
        v = vg_ref[0, :, h * RET_DV:(h + 1) * RET_DV]
        g = vg_ref[0, :, v_w + h * RET_DV: v_w + (h + 1) * RET_DV].astype(F32)
        st = state_scr[h]
        sc = _dot_nt(q, k) * jnp.where(causal, jnp.exp(lg * dpos), 0.0)
        out = _dot(sc.astype(BF16), v) + jnp.exp(lg * (tcol + 1.0)) * _dot(q, st.astype(BF16))
        kd = (k.astype(F32) * jnp.exp(lg * (lc - 1.0 - tcol))).astype(BF16)
        state_scr[h] = jnp.exp(lg * lc) * st + _dot_tn(kd, v)
        xc = out - jnp.mean(out, axis=-1, keepdims=True)
        y = xc * lax.rsqrt(jnp.mean(xc * xc, axis=-1, keepdims=True) + RMS_EPS)
        y = y * gn_ref[:, h * RET_DV:(h + 1) * RET_DV]
        out_ref[0, :, h * RET_DV:(h + 1) * RET_DV] = (g * _sigmoid(g) * y).astype(out_ref.dtype)

    @pl.when(ci == pl.num_programs(1) - 1)
    def _():
        sfin_ref[0] = state_scr[...]


def retention(qk, vg, s0, gn_g):
    b, s, _ = qk.shape
    lc = _pick(s, (256, 128, 64))
    log_g = jnp.log1p(-jnp.exp2(-5.0 - jnp.arange(RET_HEADS, dtype=F32)))
    lg_tab = jnp.broadcast_to(log_g[:, None], (RET_HEADS, LANES))
    v_w = RET_HEADS * RET_DV
    kern = functools.partial(_retention_kernel, lc=lc)
    st_spec = pl.BlockSpec((1, RET_HEADS, RET_DK, RET_DV), lambda bi, ci: (bi, 0, 0, 0))
    return pl.pallas_call(
        kern,
        grid=(b, s // lc),
        in_specs=[pl.BlockSpec((RET_HEADS, LANES), lambda bi, ci: (0, 0)),
                  pl.BlockSpec((1, lc, qk.shape[2]), lambda bi, ci: (bi, ci, 0)),
                  pl.BlockSpec((1, lc, vg.shape[2]), lambda bi, ci: (bi, ci, 0)),
                  st_spec,
                  pl.BlockSpec((1, v_w), lambda bi, ci: (0, 0))],
        out_specs=[pl.BlockSpec((1, lc, v_w), lambda bi, ci: (bi, ci, 0)), st_spec],
        out_shape=[jax.ShapeDtypeStruct((b, s, v_w), BF16),
                   jax.ShapeDtypeStruct((b, RET_HEADS, RET_DK, RET_DV), F32)],
        scratch_shapes=[pltpu.VMEM((RET_HEADS, RET_DK, RET_DV), F32)],
        compiler_params=_cparams(("parallel", "arbitrary"), 40),
        name="retention",
    )(lg_tab, qk, vg, s0, gn_g.reshape(1, v_w).astype(F32))


def _split3(x):
    hi = x.astype(BF16)
    r1 = x - hi.astype(F32)
    mid = r1.astype(BF16)
    lo = (r1 - mid.astype(F32)).astype(BF16)
    return hi, mid, lo


def _log_sigmoid(x):
    return jnp.minimum(x, 0.0) - jnp.log(1.0 + jnp.exp(-jnp.abs(x)))


def _mlstm_kernel(qkvo_ref, gates_ref, gates_t_ref, c0_ref, n0_ref, m0_ref, ng_ref,
                  out_ref, cfin_ref, nfin_ref, mfin_ref, c_scr, n_scr, m_scr, *, lc):
    ci = pl.program_id(1)

    @pl.when(ci == 0)
    def _():
        c_scr[...] = c0_ref[0]
        n_scr[...] = n0_ref[0]
        m_scr[...] = m0_ref[0]

    gates = gates_ref[0]
    gates_t = gates_t_ref[0]
    ti = lax.broadcasted_iota(jnp.int32, (lc, lc), 0)
    si = lax.broadcasted_iota(jnp.int32, (lc, lc), 1)
    tril = si <= ti
    ones_l = jnp.where(tril, 1.0, 0.0).astype(BF16)
    ones_u = jnp.where(ti <= si, 1.0, 0.0).astype(BF16)
    b_col = sum(_dot(ones_l, part) for part in _split3(_log_sigmoid(gates)))
    b_row = sum(_dot(part, ones_u) for part in _split3(_log_sigmoid(gates_t)))[M_HEADS:2 * M_HEADS]
    i_row = gates_t[0:M_HEADS]
    qk_w = M_HEADS * M_DK
    v_w = M_HEADS * M_DV
    for h in range(M_HEADS):
        bt = b_col[:, M_HEADS + h:M_HEADS + h + 1]
        bs = b_row[h:h + 1, :]
        i_s = i_row[h:h + 1, :]
        i_t = gates[:, h:h + 1]
        m_prev = m_scr[h:h + 1, 0:1]
        q = qkvo_ref[0, :, h * M_DK:(h + 1) * M_DK]
        k = qkvo_ref[0, :, qk_w + h * M_DK: qk_w + (h + 1) * M_DK]
        v = qkvo_ref[0, :, 2 * qk_w + h * M_DV: 2 * qk_w + (h + 1) * M_DV]
        og = qkvo_ref[0, :, 2 * qk_w + v_w + h * M_DV: 2 * qk_w + v_w + (h + 1) * M_DV].astype(F32)
        log_intra = jnp.where(tril, bt - bs + i_s, -jnp.inf)
        log_inter = bt + m_prev
        m_t = jnp.maximum(log_inter, jnp.max(log_intra, axis=-1, keepdims=True))
        w_inter = jnp.exp(log_inter - m_t)
        sc = _dot_nt(q, k) * jnp.exp(log_intra - m_t)
        c_st = c_scr[h]
        n_st = n_scr[h:h + 1, :]
        num = _dot(sc.astype(BF16), v) + w_inter * _dot_nt(q, c_st.astype(BF16))
        den = (jnp.sum(sc, axis=-1, keepdims=True)
               + w_inter * jnp.sum(q.astype(F32) * n_st, axis=-1, keepdims=True))
        hh = num / jnp.maximum(jnp.abs(den), jnp.exp(-m_t))
        y = hh * lax.rsqrt(jnp.mean(hh * hh, axis=-1, keepdims=True) + RMS_EPS)
        y = y * ng_ref[:, h * M_DV:(h + 1) * M_DV]
        out_ref[0, :, h * M_DV:(h + 1) * M_DV] = (y * _sigmoid(og)).astype(out_ref.dtype)
        m_new = m_t[lc - 1:lc, :]
        b_last = bt[lc - 1:lc, :]
        w_state = jnp.exp(b_last + m_prev - m_new)
        kw = k.astype(F32) * jnp.exp(b_last - bt + i_t - m_new)
        c_scr[h] = w_state * c_st + _dot_tn(v, kw.astype(BF16))
        n_scr[h:h + 1, :] = w_state * n_st + jnp.sum(kw, axis=0, keepdims=True)
        m_scr[h:h + 1, :] = jnp.broadcast_to(m_new, (1, LANES))

    @pl.when(ci == pl.num_programs(1) - 1)
    def _():
        cfin_ref[0] = c_scr[...]
        nfin_ref[0] = n_scr[...]
        mfin_ref[0] = m_scr[...]


def mlstm(qkvo, gates, c0, n0, m0, norm_g):
    b, s, w = qkvo.shape
    lc = _pick(s, (256, 128, 64))
    v_w = M_HEADS * M_DV
    gates_t = jnp.swapaxes(gates[:, :, :2 * M_HEADS], 1, 2)
    kern = functools.partial(_mlstm_kernel, lc=lc)
    c_spec = pl.BlockSpec((1, M_HEADS, M_DV, M_DK), lambda bi, ci: (bi, 0, 0, 0))
    n_spec = pl.BlockSpec((1, M_HEADS, M_DK), lambda bi, ci: (bi, 0, 0))
    m_spec = pl.BlockSpec((1, M_HEADS, LANES), lambda bi, ci: (bi, 0, 0))
    return pl.pallas_call(
        kern,
        grid=(b, s // lc),
        in_specs=[pl.BlockSpec((1, lc, w), lambda bi, ci: (bi, ci, 0)),
                  pl.BlockSpec((1, lc, LANES), lambda bi, ci: (bi, ci, 0)),
                  pl.BlockSpec((1, 2 * M_HEADS, lc), lambda bi, ci: (bi, 0, ci)),
                  c_spec, n_spec, m_spec,
                  pl.BlockSpec((1, v_w), lambda bi, ci: (0, 0))],
        out_specs=[pl.BlockSpec((1, lc, v_w), lambda bi, ci: (bi, ci, 0)), c_spec, n_spec, m_spec],
        out_shape=[jax.ShapeDtypeStruct((b, s, v_w), BF16),
                   jax.ShapeDtypeStruct((b, M_HEADS, M_DV, M_DK), F32),
                   jax.ShapeDtypeStruct((b, M_HEADS, M_DK), F32),
                   jax.ShapeDtypeStruct((b, M_HEADS, LANES), F32)],
        scratch_shapes=[pltpu.VMEM((M_HEADS, M_DV, M_DK), F32), pltpu.VMEM((M_HEADS, M_DK), F32),
                        pltpu.VMEM((M_HEADS, LANES), F32)],
        compiler_params=_cparams(("parallel", "arbitrary"), 40),
        name="mlstm",
    )(qkvo, gates, gates_t, c0, n0, m0, norm_g.reshape(1, v_w).astype(F32))


def _residual_out(acc, x_ref, gp_ref, gn_ref, xo_ref, ho_ref, rows=slice(None)):
    xn = x_ref[rows, :] + _rms(acc, gp_ref[...])
    xo_ref[rows, :] = xn
    if ho_ref is not None:
        ho_ref[rows, :] = _rms(xn, gn_ref[...]).astype(ho_ref.dtype)


def _row_chains(tm):
    n = ROW_CHAINS if tm % (ROW_CHAINS * 128) == 0 else 1
    return [slice(c * (tm // n), (c + 1) * (tm // n)) for c in range(n)]


def _outproj_kernel(*refs, n_in):
    a_refs, w_refs = refs[:n_in], refs[n_in:2 * n_in]
    x_ref, gp_ref, gn_ref, xo_ref, ho_ref = refs[2 * n_in:]
    for rows in _row_chains(x_ref.shape[0]):
        acc = _dot(a_refs[0][rows, :], w_refs[0][...])
        for a_ref, w_ref in zip(a_refs[1:], w_refs[1:]):
            acc = acc + _dot(a_ref[rows, :], w_ref[...])
        _residual_out(acc, x_ref, gp_ref, gn_ref, xo_ref, ho_ref, rows)


def outproj(acts, w, x, g_post, g_next):
    m, d = x.shape
    tm = _pick(m, (512, 256, 128, 64))
    n_in = len(acts)
    row = lambda wd: pl.BlockSpec((tm, wd), lambda i: (i, 0))
    vec = pl.BlockSpec((1, d), lambda i: (0, 0))
    w_specs, r0 = [], 0
    for a in acts:
        ka = a.shape[1]
        assert r0 % ka == 0
        w_specs.append(pl.BlockSpec((ka, d), functools.partial(lambda blk, i: (blk, 0), r0 // ka)))
        r0 += ka
    assert r0 == w.shape[0]
    return pl.pallas_call(
        functools.partial(_outproj_kernel, n_in=n_in),
        grid=(m // tm,),
        in_specs=[row(a.shape[1]) for a in acts] + w_specs + [row(d), vec, vec],
        out_specs=[row(d), row(d)],
        out_shape=[jax.ShapeDtypeStruct((m, d), F32), jax.ShapeDtypeStruct((m, d), BF16)],
        compiler_params=_cparams(("parallel",), 56),
        name="outproj",
    )(*acts, *([w] * n_in), x, g_post.reshape(1, d), g_next.reshape(1, d))


def _xattn_kernel(h_ref, wq_ref, mk_ref, mv_ref, wo_ref, x_ref, gp_ref, gn_ref, xo_ref, ho_ref):
    mk, mv = mk_ref[0], mv_ref[0]
    scale = X_DH ** -0.5
    for rows in _row_chains(x_ref.shape[0]):
        q = _dot(h_ref[rows, :], wq_ref[...]).astype(BF16)
        outs = []
        for hd in range(X_HEADS):
            sl = slice(hd * X_DH, (hd + 1) * X_DH)
            s = _dot_nt(q[:, sl], mk[:, sl]) * scale
            p = jnp.exp(s - jnp.max(s, axis=-1, keepdims=True))
            p = p / jnp.sum(p, axis=-1, keepdims=True)
            outs.append(_dot(p.astype(BF16), mv[:, sl]).astype(BF16))
        o = jnp.concatenate(outs, axis=-1)
        _residual_out(_dot(o, wo_ref[...]), x_ref, gp_ref, gn_ref, xo_ref, ho_ref, rows)


def xattn_block(h, x, mem_k, mem_v, wq, wo, g_post, g_next, seq):
    m, d = x.shape
    tm = _pick(seq, (512, 256, 128, 64))
    per_b = seq // tm
    t, xw = mem_k.shape[1], mem_k.shape[2]
    full = lambda a: pl.BlockSpec(a.shape, lambda i: (0,) * a.ndim)
    row = lambda w: pl.BlockSpec((tm, w), lambda i: (i, 0))
    mem = pl.BlockSpec((1, t, xw), lambda i: (i // per_b, 0, 0))
    vec = pl.BlockSpec((1, d), lambda i: (0, 0))
    return pl.pallas_call(
        _xattn_kernel,
        grid=(m // tm,),
        in_specs=[row(d), full(wq), mem, mem, full(wo), row(d), vec, vec],
        out_specs=[row(d), row(d)],
        out_shape=[jax.ShapeDtypeStruct((m, d), F32), jax.ShapeDtypeStruct((m, d), BF16)],
        compiler_params=_cparams(("parallel",), 48),
        name="xattn_block",
    )(h, wq, mem_k, mem_v, wo, x, g_post.reshape(1, d), g_next.reshape(1, d))


def _mlp_kernel(h_ref, w1_ref, w2_ref, x_ref, gp_ref, gn_ref, xo_ref, *rest, emit_h):
    ho_ref, acc_scr = (rest[0], rest[1]) if emit_h else (None, rest[0])
    f = pl.program_id(1)
    @pl.when(f == 0)
    def _():
        acc_scr[...] = jnp.zeros(acc_scr.shape, F32)

    a = jnp.maximum(_dot(h_ref[...], w1_ref[...]), 0.0)
    acc_scr[...] += _dot((a * a).astype(BF16), w2_ref[...])

    @pl.when(f == pl.num_programs(1) - 1)
    def _():
        _residual_out(acc_scr[...], x_ref, gp_ref, gn_ref, xo_ref, ho_ref)


def mlp_block(h, x, w1, w2, g_post, g_next):
    m, d = x.shape
    ff = w1.shape[1]
    tm = _pick(m, (512, 256, 128, 64))
    tf = _pick(ff, (512, 256, 128))
    emit_h = g_next is not None
    row = pl.BlockSpec((tm, d), lambda i, f: (i, 0))
    vec = pl.BlockSpec((1, d), lambda i, f: (0, 0))
    out_specs = [row] + ([row] if emit_h else [])
    out_shape = [jax.ShapeDtypeStruct((m, d), F32)] + ([jax.ShapeDtypeStruct((m, d), BF16)] if emit_h else [])
    g_n = (g_next if emit_h else g_post).reshape(1, d)
    res = pl.pallas_call(
        functools.partial(_mlp_kernel, emit_h=emit_h),
        grid=(m // tm, ff // tf),
        in_specs=[row, pl.BlockSpec((d, tf), lambda i, f: (0, f)), pl.BlockSpec((tf, d), lambda i, f: (f, 0)),
                  row, vec, vec],
        out_specs=out_specs,
        out_shape=out_shape,
        scratch_shapes=[pltpu.VMEM((tm, d), F32)],
        compiler_params=_cparams(("parallel", "arbitrary"), 56),
        name="mlp_block",
    )(h, w1, w2, x, g_post.reshape(1, d), g_n)
    return (res[0], res[1]) if emit_h else (res[0], None)


def _rope_tables(pos):
    def angles(half):
        inv = jnp.power(ROPE_BASE, -jnp.arange(half, dtype=F32) / half)
        return pos.astype(F32)[:, None] * inv[None, :]
    a = angles(RET_DK // 2)
    c, s = jnp.cos(a), jnp.sin(a)
    t128 = (jnp.concatenate([c, c], axis=1), jnp.concatenate([-s, s], axis=1))
    a = angles(MLA_ROPE // 2)
    c, s = jnp.cos(a), jnp.sin(a)
    z = jnp.zeros((pos.shape[0], LANES - MLA_ROPE), F32)
    t64 = (jnp.concatenate([c, c, z], axis=1), jnp.concatenate([s, s, z], axis=1))
    return t128, t64


def _prep_weights(w_in_even, w_uq, w_ukv, w_out_even, w_in_odd, b_gates_odd, w_out_odd, w_xq, w_xo, w_mlp1, w_mlp2):
    d = w_in_even.shape[1]
    ret_w = RET_HEADS * RET_DK
    n_qkvg = 2 * ret_w + 2 * RET_HEADS * RET_DV
    n_lat = MLA_Q_LORA + MLA_KV_LORA + MLA_ROPE
    even = []
    for e in range(w_in_even.shape[0]):
        w = w_in_even[e].astype(BF16)
        wlat = jnp.pad(w[:, n_qkvg:n_qkvg + n_lat], ((0, 0), (0, LANES - MLA_ROPE)))
        uq = w_uq[e].reshape(MLA_Q_LORA, MLA_HEADS, MLA_NOPE + MLA_ROPE)
        uq_nope = uq[:, :, :MLA_NOPE].reshape(MLA_Q_LORA, MLA_HEADS * MLA_NOPE)
        uq_rope = jnp.pad(uq[:, :, MLA_NOPE:], ((0, 0), (0, 0), (0, LANES - MLA_ROPE)))
        uq_rope = uq_rope.reshape(MLA_Q_LORA, MLA_HEADS * LANES)
        ukv = w_ukv[e].reshape(MLA_KV_LORA, MLA_HEADS, MLA_NOPE + MLA_V)
        even.append(dict(
            w_in=w,
            w_lat=wlat,
            w_uq=jnp.concatenate([uq_nope, uq_rope], axis=1).astype(BF16),
            w_uk=ukv[:, :, :MLA_NOPE].reshape(MLA_KV_LORA, MLA_HEADS * MLA_NOPE).astype(BF16),
            w_uv=ukv[:, :, MLA_NOPE:].reshape(MLA_KV_LORA, MLA_HEADS * MLA_V).astype(BF16),
            w_out=w_out_even[e].astype(BF16),
        ))
    m_qk = M_HEADS * M_DK
    n_qkvo = 2 * m_qk + 2 * M_HEADS * M_DV
    odd = []
    for o in range(w_in_odd.shape[0]):
        w = w_in_odd[o].astype(BF16)
        odd.append(dict(
            w_in=w,
            w_gates=jnp.pad(w[:, n_qkvo:], ((0, 0), (0, LANES - 2 * M_HEADS))),
            b_gates=jnp.pad(b_gates_odd[o], (0, LANES - 2 * M_HEADS)),
            w_out=w_out_odd[o].astype(BF16),
        ))
    q_scale = jnp.concatenate([jnp.full((m_qk,), M_DK ** -0.5, F32), jnp.ones((n_qkvo - m_qk,), F32)])
    rk_scale = jnp.concatenate([jnp.ones((ret_w,), F32), jnp.full((ret_w,), RET_DK ** -0.5, F32)])
    shared = dict(w_xq=w_xq.astype(BF16), w_xo=w_xo.astype(BF16), w_mlp1=w_mlp1.astype(BF16),
                  w_mlp2=w_mlp2.astype(BF16), q_scale=q_scale, rk_scale=rk_scale, d=d,
                  n_qk=2 * ret_w, n_vg=n_qkvg - 2 * ret_w, n_qkvo=n_qkvo)
    return even, odd, shared


def _run_trunk(x, pos, mem_k, mem_v, ret_s0, c0, n0, m0, past, norm_g, gains, even_w, odd_w, shared):
    b, s, d = x.shape
    m = b * s
    depth = norm_g.shape[0]
    t128, t64 = _rope_tables(pos)
    if m % s or s % 64:
        raise ValueError("unsupported sequence length")
    if s < 256:
        t128 = tuple(jnp.tile(t, (b, 1)) for t in t128)
        t64 = tuple(jnp.tile(t, (b, 1)) for t in t64)
    xf = x.reshape(m, d)
    h = rms_rows(xf, norm_g[0, 0], BF16)
    ckvs, krs, rets, cs, ns, ms = [], [], [], [], [], []
    for layer in range(depth):
        g = norm_g[layer]
        if layer % 2 == 0:
            e = layer // 2
            w = even_w[e]
            qk = matmul(h, w["w_in"], BF16, n_cols=shared["n_qk"], col_scale=shared["rk_scale"], rope=t128)
            vg = matmul(h, w["w_in"], BF16, col0=shared["n_qk"], n_cols=shared["n_vg"])
            ret, ret_s = retention(qk.reshape(b, s, -1), vg.reshape(b, s, -1), ret_s0[e], gains["ret_gn_g"][e])
            qcat, kcat, v, ckv, kr = even_latent(h, w["w_lat"], gains["mla_q_norm_g"][e].reshape(1, -1),
                                                 gains["mla_kv_norm_g"][e].reshape(1, -1),
                                                 w["w_uq"], w["w_uk"], w["w_uv"], t64)
            kcat = kcat.reshape(b, s, -1)
            v = v.reshape(b, s, -1)
            if past is None:
                q_off, sk = 0, s
            else:
                p_ckv, p_krp = past[e]
                p_len = p_ckv.shape[1]
                pk, pv = past_kv(p_ckv.reshape(b * p_len, -1), p_krp.reshape(b * p_len, -1), w["w_uk"], w["w_uv"])
                q_off, sk = p_len, p_len + s
                pad = (-sk) % 256
                kcat = jnp.concatenate([pk.reshape(b, p_len, -1), kcat,
                                        jnp.zeros((b, pad, kcat.shape[-1]), BF16)], axis=1)
                v = jnp.concatenate([pv.reshape(b, p_len, -1), v, jnp.zeros((b, pad, v.shape[-1]), BF16)], axis=1)
            att = mla_attention(qcat.reshape(b, s, -1), kcat, v, q_off=q_off, sk_valid=sk)
            xf, h = outproj([ret.reshape(m, -1), att.reshape(m, -1)], w["w_out"], xf, g[1], g[2])
            rets.append(ret_s)
            ckvs.append(ckv.reshape(b, s, -1))
            krs.append(kr.reshape(b, s, -1))
        else:
            o = layer // 2
            w = odd_w[o]
            qkvo = matmul(h, w["w_in"], BF16, n_cols=shared["n_qkvo"], col_scale=shared["q_scale"])
            gates = matmul(h, w["w_gates"], F32, col_bias=w["b_gates"])
            hh, c, n, mm = mlstm(qkvo.reshape(b, s, -1), gates.reshape(b, s, -1), c0[o], n0[o],
                                 jnp.broadcast_to(m0[o][..., None], m0[o].shape + (LANES,)),
                                 gains["mlstm_norm_g"][o])
            xf, h = outproj([hh.reshape(m, -1)], w["w_out"], xf, g[1], g[2])
            cs.append(c)
            ns.append(n)
            ms.append(mm[..., 0])
        xf, h = xattn_block(h, xf, mem_k[layer], mem_v[layer], shared["w_xq"][layer], shared["w_xo"][layer],
                            g[3], g[4], s)
        g_next = norm_g[layer + 1, 0] if layer + 1 < depth else None
        xf, h = mlp_block(h, xf, shared["w_mlp1"][layer], shared["w_mlp2"][layer], g[5], g_next)
    return (xf.reshape(b, s, d), jnp.stack(ckvs), jnp.stack(krs), jnp.stack(rets),
            jnp.stack(cs), jnp.stack(ns), jnp.stack(ms))


def kernel(x_prompt, x_sample, cache_mla_ckv, cache_mla_krope, state_ret, state_mlstm_C, state_mlstm_n, state_mlstm_m, cache_mem_k, cache_mem_v, mem_prompt, norm_g, mem_norm_g, w_in_even, mla_q_norm_g, mla_kv_norm_g, w_uq, w_ukv, ret_gn_g, w_out_even, w_in_odd, b_gates_odd, mlstm_norm_g, w_out_odd, w_xq, w_xk, w_xv, w_xo, w_mlp1, w_mlp2):
    even_w, odd_w, shared = _prep_weights(w_in_even, w_uq, w_ukv, w_out_even, w_in_odd, b_gates_odd, w_out_odd,
                                          w_xq, w_xo, w_mlp1, w_mlp2)
    gains = dict(ret_gn_g=ret_gn_g, mla_q_norm_g=mla_q_norm_g, mla_kv_norm_g=mla_kv_norm_g,
                 mlstm_norm_g=mlstm_norm_g)
    depth = norm_g.shape[0]
    b, s, d = x_prompt.shape
    n_even, n_odd = w_in_even.shape[0], w_in_odd.shape[0]
    xw = X_HEADS * X_DH

    bm, t, _ = mem_prompt.shape
    mem_flat = mem_prompt.reshape(bm * t, d)
    p_mem_k, p_mem_v = [], []
    for layer in range(depth):
        mn = rms_rows(mem_flat, mem_norm_g[layer], BF16)
        kv = matmul(mn, jnp.concatenate([w_xk[layer], w_xv[layer]], axis=1).astype(BF16), F32)
        p_mem_k.append(kv[:, :xw].reshape(bm, t, X_HEADS, X_DH))
        p_mem_v.append(kv[:, xw:].reshape(bm, t, X_HEADS, X_DH))
    p_mem_k, p_mem_v = jnp.stack(p_mem_k), jnp.stack(p_mem_v)

    pos_p = jnp.arange(s, dtype=jnp.int32)
    zeros = lambda *shape: jnp.zeros(shape, F32)
    y_prompt, p_ckv, p_kr, p_ret, p_c, p_n, p_m = _run_trunk(
        x_prompt, pos_p, p_mem_k.reshape(depth, bm, t, xw).astype(BF16), p_mem_v.reshape(depth, bm, t, xw).astype(BF16),
        zeros(n_even, b, RET_HEADS, RET_DK, RET_DV), zeros(n_odd, b, M_HEADS, M_DV, M_DK),
        zeros(n_odd, b, M_HEADS, M_DK), zeros(n_odd, b, M_HEADS), None, norm_g, gains, even_w, odd_w, shared)

    db, ds, _ = x_sample.shape
    past_len = cache_mla_ckv.shape[2]
    pos_s = past_len + jnp.arange(ds, dtype=jnp.int32)
    past = [(cache_mla_ckv[e], jnp.pad(cache_mla_krope[e], ((0, 0), (0, 0), (0, LANES - MLA_ROPE))))
            for e in range(n_even)]
    y_sample, s_ckv, s_kr, s_ret, s_c, s_n, s_m = _run_trunk(
        x_sample, pos_s, cache_mem_k.reshape(depth, db, -1, xw).astype(BF16),
        cache_mem_v.reshape(depth, db, -1, xw).astype(BF16),
        state_ret, state_mlstm_C, state_mlstm_n, state_mlstm_m, past, norm_g, gains, even_w, odd_w, shared)

    return (y_prompt, y_sample, p_ckv, p_kr, p_ret, p_c, p_n, p_m, p_mem_k, p_mem_v,
            s_ckv, s_kr, s_ret, s_c, s_n, s_m)
```

```python
import functools

import jax
import jax.numpy as jnp
from jax import lax
from jax.experimental import pallas as pl
from jax.experimental.pallas import tpu as pltpu

F32 = jnp.float32
BF16 = jnp.bfloat16

RMS_EPS = 1e-6
ROPE_BASE = 10000.0
CHUNK = 64
CHUNK_SHIFT = 6
assert 1 << CHUNK_SHIFT == CHUNK

RET_HEADS, RET_DK, RET_DV = 8, 128, 128
MLA_HEADS, MLA_NOPE, MLA_ROPE, MLA_V = 8, 128, 64, 128
MLA_Q_LORA, MLA_KV_LORA = 768, 512
MLA_QK_PAD = 256
MLA_HEADS_PER_STEP = 2
MLA_Q_PRESCALE = (MLA_NOPE + MLA_ROPE) ** -0.5 * 1.4426950408889634
M_HEADS, M_DK, M_DV = 8, 128, 256
X_HEADS, X_DH = 4, 128
LANES = 128
ROW_CHAINS = 2

V7X_VMEM_BYTES = 64 * 1024 * 1024
MIB = 1024 * 1024


def _cparams(semantics, vmem_mib):
    assert vmem_mib * MIB < V7X_VMEM_BYTES
    return pltpu.CompilerParams(dimension_semantics=semantics, vmem_limit_bytes=vmem_mib * MIB)


def _pick(n, cands):
    for c in cands:
        if c <= n and n % c == 0:
            return c
    raise ValueError(f"no tile for {n} in {cands}")


def _rms(x, g):
    ms = jnp.mean(x * x, axis=-1, keepdims=True)
    return x * lax.rsqrt(ms + RMS_EPS) * g


def _sigmoid(x):
    return 1.0 / (1.0 + jnp.exp(-x))


def _dot(a, b):
    return jnp.dot(a, b, preferred_element_type=F32)


def _dot_nt(a, b):
    return lax.dot_general(a, b, (((1,), (1,)), ((), ())), preferred_element_type=F32)


def _dot_tn(a, b):
    return lax.dot_general(a, b, (((0,), (0,)), ((), ())), preferred_element_type=F32)


def _rope128(x, cos2, sin2):
    return x * cos2 + pltpu.roll(x, 64, 1) * sin2


def _rope64(x, cos_p, sin_p):
    return x * cos_p + (pltpu.roll(x, 32, 1) - pltpu.roll(x, 96, 1)) * sin_p


def _norm_kernel(x_ref, g_ref, o_ref):
    o_ref[...] = _rms(x_ref[...].astype(F32), g_ref[...]).astype(o_ref.dtype)


def rms_rows(x, g, out_dtype):
    m, k = x.shape
    tm = _pick(m, (512, 256, 128, 64, 32, 16, 8))
    return pl.pallas_call(
        _norm_kernel,
        grid=(m // tm,),
        in_specs=[pl.BlockSpec((tm, k), lambda i: (i, 0)), pl.BlockSpec((1, k), lambda i: (0, 0))],
        out_specs=pl.BlockSpec((tm, k), lambda i: (i, 0)),
        out_shape=jax.ShapeDtypeStruct((m, k), out_dtype),
        compiler_params=_cparams(("parallel",), 32),
        name="rms_rows",
    )(x, g.reshape(1, k).astype(F32))


def _mm_kernel(*refs, has_scale, has_bias, n_rope, tn):
    it = iter(refs)
    h_ref, w_ref = next(it), next(it)
    scale_ref = next(it) if has_scale else None
    bias_ref = next(it) if has_bias else None
    cos_ref, sin_ref = (next(it), next(it)) if n_rope else (None, None)
    o_ref = next(it)
    acc = _dot(h_ref[...], w_ref[...])
    if has_scale:
        acc = acc * scale_ref[...]
    if has_bias:
        acc = acc + bias_ref[...]
    if n_rope:
        c, s = cos_ref[...], sin_ref[...]
        for t in range(tn // LANES):
            sl = slice(t * LANES, (t + 1) * LANES)
            o_ref[:, sl] = _rope128(acc[:, sl], c, s).astype(o_ref.dtype)
    else:
        o_ref[...] = acc.astype(o_ref.dtype)


def matmul(h, w, out_dtype, *, col0=0, n_cols=None, col_scale=None, col_bias=None, rope=None):
    m, k = h.shape
    n = w.shape[1] - col0 if n_cols is None else n_cols
    tm = _pick(m, (1024, 512, 256, 128, 64))
    tn = _pick(n, (512, 256, 128))
    assert col0 % tn == 0 and col0 + n <= w.shape[1]
    jb = col0 // tn
    args = [h, w]
    in_specs = [pl.BlockSpec((tm, k), lambda i, j: (i, 0)), pl.BlockSpec((k, tn), lambda i, j: (0, j + jb))]
    for vec in (col_scale, col_bias):
        if vec is not None:
            args.append(vec.reshape(1, n).astype(F32))
            in_specs.append(pl.BlockSpec((1, tn), lambda i, j: (0, j)))
    if rope is not None:
        r = rope[0].shape[0]
        tm = _pick(m, tuple(c for c in (1024, 512, 256, 128, 64) if r % c == 0))
        in_specs[0] = pl.BlockSpec((tm, k), lambda i, j: (i, 0))
        nrb = r // tm
        for tab in rope:
            args.append(tab)
            in_specs.append(pl.BlockSpec((tm, LANES), lambda i, j: (i % nrb, 0)))
    kern = functools.partial(_mm_kernel, has_scale=col_scale is not None, has_bias=col_bias is not None,
                             n_rope=rope is not None, tn=tn)
    return pl.pallas_call(
        kern,
        grid=(m // tm, n // tn),
        in_specs=in_specs,
        out_specs=pl.BlockSpec((tm, tn), lambda i, j: (i, j)),
        out_shape=jax.ShapeDtypeStruct((m, n), out_dtype),
        compiler_params=_cparams(("parallel", "arbitrary"), 40),
        name="matmul",
    )(*args)


def _kv_expand(ckvn_bf16, krr_bf16, wuk_ref, wuv_ref, kcat_ref, v_ref):
    kn = _dot(ckvn_bf16, wuk_ref[...])
    for hd in range(MLA_HEADS):
        kcat_ref[:, hd * MLA_QK_PAD: hd * MLA_QK_PAD + MLA_NOPE] = (
            kn[:, hd * MLA_NOPE:(hd + 1) * MLA_NOPE].astype(BF16))
        kcat_ref[:, hd * MLA_QK_PAD + MLA_NOPE:(hd + 1) * MLA_QK_PAD] = krr_bf16
    v_ref[...] = _dot(ckvn_bf16, wuv_ref[...]).astype(BF16)


def _even_latent_kernel(h_ref, wlat_ref, gq_ref, gkv_ref, wuq_ref, wuk_ref, wuv_ref, cos_ref, sin_ref,
                        qcat_ref, kcat_ref, v_ref, ckv_ref, kr_ref):
    lat = _dot(h_ref[...], wlat_ref[...])
    cq = lat[:, :MLA_Q_LORA]
    ckv = lat[:, MLA_Q_LORA:MLA_Q_LORA + MLA_KV_LORA]
    krp = lat[:, MLA_Q_LORA + MLA_KV_LORA:]
    c, s = cos_ref[...], sin_ref[...]
    q = _dot(_rms(cq, gq_ref[...]).astype(BF16), wuq_ref[...]) * MLA_Q_PRESCALE
    nope_w = MLA_HEADS * MLA_NOPE
    for hd in range(MLA_HEADS):
        qcat_ref[:, hd * MLA_QK_PAD: hd * MLA_QK_PAD + MLA_NOPE] = (
            q[:, hd * MLA_NOPE:(hd + 1) * MLA_NOPE].astype(BF16))
        qr = q[:, nope_w + hd * LANES: nope_w + (hd + 1) * LANES]
        qcat_ref[:, hd * MLA_QK_PAD + MLA_NOPE:(hd + 1) * MLA_QK_PAD] = _rope64(qr, c, s).astype(BF16)
    ckvn = _rms(ckv, gkv_ref[...])
    ckv_ref[...] = ckvn
    krr = _rope64(krp, c, s)
    kr_ref[...] = krr[:, :MLA_ROPE]
    _kv_expand(ckvn.astype(BF16), krr.astype(BF16), wuk_ref, wuv_ref, kcat_ref, v_ref)


def even_latent(h, wlat, gq, gkv, wuq, wuk, wuv, rope64):
    m, d = h.shape
    r = rope64[0].shape[0]
    tm = _pick(m, tuple(c for c in (256, 128, 64) if r % c == 0))
    nrb = r // tm
    full = lambda a: pl.BlockSpec(a.shape, lambda i: (0,) * a.ndim)
    row = lambda w: pl.BlockSpec((tm, w), lambda i: (i, 0))
    tab = pl.BlockSpec((tm, LANES), lambda i: (i % nrb, 0))
    qk_w = MLA_HEADS * MLA_QK_PAD
    v_w = MLA_HEADS * MLA_V
    return pl.pallas_call(
        _even_latent_kernel,
        grid=(m // tm,),
        in_specs=[row(d), full(wlat), full(gq), full(gkv), full(wuq), full(wuk), full(wuv), tab, tab],
        out_specs=[row(qk_w), row(qk_w), row(v_w), row(MLA_KV_LORA), row(MLA_ROPE)],
        out_shape=[jax.ShapeDtypeStruct((m, qk_w), BF16), jax.ShapeDtypeStruct((m, qk_w), BF16),
                   jax.ShapeDtypeStruct((m, v_w), BF16), jax.ShapeDtypeStruct((m, MLA_KV_LORA), F32),
                   jax.ShapeDtypeStruct((m, MLA_ROPE), F32)],
        compiler_params=_cparams(("parallel",), 48),
        name="even_latent",
    )(h, wlat, gq, gkv, wuq, wuk, wuv, rope64[0], rope64[1])


def _past_kv_kernel(ckv_ref, krp_ref, wuk_ref, wuv_ref, kcat_ref, v_ref):
    _kv_expand(ckv_ref[...].astype(BF16), krp_ref[...].astype(BF16), wuk_ref, wuv_ref, kcat_ref, v_ref)


def past_kv(ckvn, kr_pad, wuk, wuv):
    m = ckvn.shape[0]
    tm = _pick(m, (512, 256, 128, 64))
    full = lambda a: pl.BlockSpec(a.shape, lambda i: (0,) * a.ndim)
    row = lambda w: pl.BlockSpec((tm, w), lambda i: (i, 0))
    qk_w = MLA_HEADS * MLA_QK_PAD
    v_w = MLA_HEADS * MLA_V
    return pl.pallas_call(
        _past_kv_kernel,
        grid=(m // tm,),
        in_specs=[row(MLA_KV_LORA), row(LANES), full(wuk), full(wuv)],
        out_specs=[row(qk_w), row(v_w)],
        out_shape=[jax.ShapeDtypeStruct((m, qk_w), BF16), jax.ShapeDtypeStruct((m, v_w), BF16)],
        compiler_params=_cparams(("parallel",), 32),
        name="past_kv",
    )(ckvn, kr_pad, wuk, wuv)


def _mla_kernel(q_ref, k_ref, v_ref, o_ref, m_scr, l_scr, acc_scr, a_scr, p_scr, *, tq, tk, hb, q_off, sk_valid):
    qi = pl.program_id(2)
    q0 = qi * tq
    cq_lo = (q_off + q0) // CHUNK
    cq_hi = (q_off + q0 + tq - 1) // CHUNK
    n_full = jnp.minimum(lax.div((cq_lo + 1) * CHUNK, tk), sk_valid // tk)
    n_vis = lax.div(jnp.minimum((cq_hi + 1) * CHUNK, sk_valid) + tk - 1, tk)
    m_scr[...] = jnp.full(m_scr.shape, -1e30, F32)
    l_scr[...] = jnp.zeros(l_scr.shape, F32)
    acc_scr[...] = jnp.zeros(acc_scr.shape, F32)
    a_scr[...] = jnp.ones(a_scr.shape, F32)
    p_scr[...] = jnp.zeros(p_scr.shape, BF16)

    def apply_prev(j, ki_prev):
        v = v_ref[0, pl.ds(pl.multiple_of(ki_prev * tk, tk), tk), j * MLA_V:(j + 1) * MLA_V]
        acc_scr[j] = a_scr[j] * acc_scr[j] + _dot(p_scr[j], v)

    def step(ki, masked):
        k0 = pl.multiple_of(ki * tk, tk)
        ki_prev = jnp.maximum(ki - 1, 0)
        if masked:
            qpos = q_off + q0 + lax.broadcasted_iota(jnp.int32, (tq, tk), 0)
            kpos = k0 + lax.broadcasted_iota(jnp.int32, (tq, tk), 1)
            vis = jnp.logical_and((kpos >> CHUNK_SHIFT) <= (qpos >> CHUNK_SHIFT), kpos < sk_valid)
        for j in range(hb):
            q = q_ref[0, :, j * MLA_QK_PAD:(j + 1) * MLA_QK_PAD]
            k = k_ref[0, pl.ds(k0, tk), j * MLA_QK_PAD:(j + 1) * MLA_QK_PAD]
            s = _dot_nt(q, k)
            apply_prev(j, ki_prev)
            if masked:
                s = jnp.where(vis, s, -jnp.inf)
            m_prev = m_scr[j]
            m_new = jnp.maximum(m_prev, jnp.max(s, axis=-1, keepdims=True))
            alpha = jnp.exp2(m_prev - m_new)
            p = [jnp.exp2(s[:, c * LANES:(c + 1) * LANES] - m_new) for c in range(tk // LANES)]
            l_scr[j] = alpha * l_scr[j] + sum(p[1:], p[0])
            p_scr[j] = jnp.concatenate([pc.astype(BF16) for pc in p], axis=-1)
            a_scr[j] = alpha
            m_scr[j] = m_new

    def body_full(ki, carry):
        step(ki, False)
        return carry

    def body_masked(ki, carry):
        step(ki, True)
        return carry

    lax.fori_loop(0, n_full, body_full, 0)
    lax.fori_loop(n_full, n_vis, body_masked, 0)
    for j in range(hb):
        apply_prev(j, n_vis - 1)
        l = jnp.sum(l_scr[j], axis=-1, keepdims=True)
        o_ref[0, :, j * MLA_V:(j + 1) * MLA_V] = (acc_scr[j] / l).astype(o_ref.dtype)


def mla_attention(qcat, kcat, v, *, q_off, sk_valid):
    b, sq, _ = qcat.shape
    skp = kcat.shape[1]
    tq = _pick(sq, (512, 256, 128, 64))
    tk = skp if tq * skp <= 512 * 512 else _pick(skp, (512, 256, 128))
    hb = MLA_HEADS_PER_STEP
    kern = functools.partial(_mla_kernel, tq=tq, tk=tk, hb=hb, q_off=q_off, sk_valid=sk_valid)
    return pl.pallas_call(
        kern,
        grid=(b, MLA_HEADS // hb, sq // tq),
        in_specs=[pl.BlockSpec((1, tq, hb * MLA_QK_PAD), lambda bi, h, qi: (bi, qi, h)),
                  pl.BlockSpec((1, skp, hb * MLA_QK_PAD), lambda bi, h, qi: (bi, 0, h)),
                  pl.BlockSpec((1, skp, hb * MLA_V), lambda bi, h, qi: (bi, 0, h))],
        out_specs=pl.BlockSpec((1, tq, hb * MLA_V), lambda bi, h, qi: (bi, qi, h)),
        out_shape=jax.ShapeDtypeStruct((b, sq, MLA_HEADS * MLA_V), BF16),
        scratch_shapes=[pltpu.VMEM((hb, tq, LANES), F32), pltpu.VMEM((hb, tq, LANES), F32),
                        pltpu.VMEM((hb, tq, MLA_V), F32), pltpu.VMEM((hb, tq, LANES), F32),
                        pltpu.VMEM((hb, tq, tk), BF16)],
        compiler_params=_cparams(("parallel", "parallel", "arbitrary"), 48),
        name="mla_attention",
    )(qcat, kcat, v)


def _retention_kernel(lg_ref, qk_ref, vg_ref, s0_ref, gn_ref, out_ref, sfin_ref,
                      state_scr, decay_scr, qdec_scr, kdec_scr, *, lc):
    ci = pl.program_id(1)
    heads = range(RET_HEADS)

    @pl.when(ci == 0)
    def _():
        state_scr[...] = s0_ref[0]
        ti = lax.broadcasted_iota(jnp.int32, (lc, lc), 0)
        si = lax.broadcasted_iota(jnp.int32, (lc, lc), 1)
        dpos = jnp.maximum(ti - si, 0).astype(F32)
        trow = lax.broadcasted_iota(jnp.int32, (lc, LANES), 0).astype(F32)
        for h in heads:
            lg = lg_ref[h:h + 1, :]
            decay_scr[h] = jnp.where(ti >= si, jnp.exp(lg[:, 0:1] * dpos), 0.0)
            qdec_scr[h] = jnp.exp(lg * (trow + 1.0))
            kdec_scr[h] = jnp.exp(lg * (lc - 1.0 - trow))

    qk_w = RET_HEADS * RET_DK
    v_w = RET_HEADS * RET_DV
    q = [qk_ref[0, :, h * RET_DK:(h + 1) * RET_DK] for h in heads]
    k = [qk_ref[0, :, qk_w + h * RET_DK: qk_w + (h + 1) * RET_DK] for h in heads]
    v = [vg_ref[0, :, h * RET_DV:(h + 1) * RET_DV] for h in heads]
    st = [state_scr[h] for h in heads]
    sc = [_dot_nt(q[h], k[h]) for h in heads]
    inter = [_dot(q[h], st[h].astype(BF16)) for h in heads]
    scd = [(sc[h] * decay_scr[h]).astype(BF16) for h in heads]
    kd = [(k[h].astype(F32) * kdec_scr[h]).astype(BF16) for h in heads]
    intra = [_dot(scd[h], v[h]) for h in heads]
    upd = [_dot_tn(kd[h], v[h]) for h in heads]
    for h in heads:
        state_scr[h] = jnp.exp(lg_ref[h:h + 1, :] * lc) * st[h] + upd[h]
        out = intra[h] + qdec_scr[h] * inter[h]
        g = vg_ref[0, :, v_w + h * RET_DV: v_w + (h + 1) * RET_DV].astype(F32)
        xc = out - jnp.mean(out, axis=-1, keepdims=True)
        y = xc * lax.rsqrt(jnp.mean(xc * xc, axis=-1, keepdims=True) + RMS_EPS)
        y = y * gn_ref[:, h * RET_DV:(h + 1) * RET_DV]
        out_ref[0, :, h * RET_DV:(h + 1) * RET_DV] = (g * _sigmoid(g) * y).astype(out_ref.dtype)

    @pl.when(ci == pl.num_programs(1) - 1)
    def _():
        sfin_ref[0] = state_scr[...]


def retention(qk, vg, s0, gn_g):
    b, s, _ = qk.shape
    lc = _pick(s, (256, 128, 64))
    log_g = jnp.log1p(-jnp.exp2(-5.0 - jnp.arange(RET_HEADS, dtype=F32)))
    lg_tab = jnp.broadcast_to(log_g[:, None], (RET_HEADS, LANES))
    v_w = RET_HEADS * RET_DV
    kern = functools.partial(_retention_kernel, lc=lc)
    st_spec = pl.BlockSpec((1, RET_HEADS, RET_DK, RET_DV), lambda bi, ci: (bi, 0, 0, 0))
    return pl.pallas_call(
        kern,
        grid=(b, s // lc),
        in_specs=[pl.BlockSpec((RET_HEADS, LANES), lambda bi, ci: (0, 0)),
                  pl.BlockSpec((1, lc, qk.shape[2]), lambda bi, ci: (bi, ci, 0)),
                  pl.BlockSpec((1, lc, vg.shape[2]), lambda bi, ci: (bi, ci, 0)),
                  st_spec,
                  pl.BlockSpec((1, v_w), lambda bi, ci: (0, 0))],
        out_specs=[pl.BlockSpec((1, lc, v_w), lambda bi, ci: (bi, ci, 0)), st_spec],
        out_shape=[jax.ShapeDtypeStruct((b, s, v_w), BF16),
                   jax.ShapeDtypeStruct((b, RET_HEADS, RET_DK, RET_DV), F32)],
        scratch_shapes=[pltpu.VMEM((RET_HEADS, RET_DK, RET_DV), F32), pltpu.VMEM((RET_HEADS, lc, lc), F32),
                        pltpu.VMEM((RET_HEADS, lc, LANES), F32), pltpu.VMEM((RET_HEADS, lc, LANES), F32)],
        compiler_params=_cparams(("parallel", "arbitrary"), 40),
        name="retention",
    )(lg_tab, qk, vg, s0, gn_g.reshape(1, v_w).astype(F32))


def _split3(x):
    hi = x.astype(BF16)
    r1 = x - hi.astype(F32)
    mid = r1.astype(BF16)
    lo = (r1 - mid.astype(F32)).astype(BF16)
    return hi, mid, lo


def _log_sigmoid(x):
    return jnp.minimum(x, 0.0) - jnp.log(1.0 + jnp.exp(-jnp.abs(x)))


def _widen(x, width):
    if width < LANES:
        return x[:, :width]
    return x if width == LANES else jnp.concatenate([x] * (width // LANES), axis=1)


def _mlstm_kernel(qkvo_ref, gates_ref, gates_t_ref, c0_ref, n0_ref, m0_ref, ng_ref,
                  out_ref, cfin_ref, nfin_ref, mfin_ref, c_scr, n_scr, m_scr, *, lc):
    ci = pl.program_id(1)
    heads = range(M_HEADS)

    @pl.when(ci == 0)
    def _():
        c_scr[...] = c0_ref[0]
        n_scr[...] = n0_ref[0]
        m_scr[...] = m0_ref[0]

    gates = gates_ref[0]
    gates_t = gates_t_ref[0]
    ti = lax.broadcasted_iota(jnp.int32, (lc, lc), 0)
    si = lax.broadcasted_iota(jnp.int32, (lc, lc), 1)
    tril = si <= ti
    ones_l = jnp.where(tril, 1.0, 0.0).astype(BF16)
    ones_u = jnp.where(ti <= si, 1.0, 0.0).astype(BF16)
    b_col = sum(_dot(ones_l, part) for part in _split3(_log_sigmoid(gates)))
    b_row = sum(_dot(part, ones_u) for part in _split3(_log_sigmoid(gates_t)))[M_HEADS:2 * M_HEADS]
    i_row = gates_t[0:M_HEADS]
    qk_w = M_HEADS * M_DK
    v_w = M_HEADS * M_DV
    q = [qkvo_ref[0, :, h * M_DK:(h + 1) * M_DK] for h in heads]
    k = [qkvo_ref[0, :, qk_w + h * M_DK: qk_w + (h + 1) * M_DK] for h in heads]
    v = [qkvo_ref[0, :, 2 * qk_w + h * M_DV: 2 * qk_w + (h + 1) * M_DV] for h in heads]
    c_st = [c_scr[h] for h in heads]
    n_st = [n_scr[h:h + 1, :] for h in heads]
    m_prev = [m_scr[h:h + 1, :] for h in heads]
    qk = [_dot_nt(q[h], k[h]) for h in heads]
    qc = [_dot(q[h], c_st[h].astype(BF16)) for h in heads]
    bt = [jnp.broadcast_to(b_col[:, M_HEADS + h:M_HEADS + h + 1], (lc, LANES)) for h in heads]
    i_t = [jnp.broadcast_to(gates[:, h:h + 1], (lc, LANES)) for h in heads]
    m_t, w_inter, sc = [], [], []
    for h in heads:
        log_intra = jnp.where(tril, _widen(bt[h], lc) - b_row[h:h + 1, :] + i_row[h:h + 1, :], -jnp.inf)
        log_inter = bt[h] + m_prev[h]
        m_t.append(jnp.maximum(log_inter, jnp.max(log_intra, axis=-1, keepdims=True)))
        w_inter.append(jnp.exp(log_inter - m_t[h]))
        sc.append(qk[h] * jnp.exp(log_intra - _widen(m_t[h], lc)))
    num_intra = [_dot(sc[h].astype(BF16), v[h]) for h in heads]
    kw = []
    for h in heads:
        m_new = m_t[h][lc - 1:lc, :]
        b_last = bt[h][lc - 1:lc, :]
        w_state = jnp.exp(b_last + m_prev[h] - m_new)
        kw.append(k[h].astype(F32) * jnp.exp(b_last - bt[h] + i_t[h] - m_new))
        n_scr[h:h + 1, :] = w_state * n_st[h] + jnp.sum(kw[h], axis=0, keepdims=True)
        m_scr[h:h + 1, :] = m_new
        c_scr[h] = _widen(w_state, M_DV) * c_st[h] + _dot_tn(kw[h].astype(BF16), v[h])
    for h in heads:
        num = num_intra[h] + _widen(w_inter[h], M_DV) * qc[h]
        den = (jnp.sum(sc[h], axis=-1, keepdims=True)
               + w_inter[h] * jnp.sum(q[h].astype(F32) * n_st[h], axis=-1, keepdims=True))
        hh = num * _widen(1.0 / jnp.maximum(jnp.abs(den), jnp.exp(-m_t[h])), M_DV)
        y = hh * lax.rsqrt(jnp.mean(hh * hh, axis=-1, keepdims=True) + RMS_EPS)
        y = y * ng_ref[:, h * M_DV:(h + 1) * M_DV]
        og = qkvo_ref[0, :, 2 * qk_w + v_w + h * M_DV: 2 * qk_w + v_w + (h + 1) * M_DV].astype(F32)
        out_ref[0, :, h * M_DV:(h + 1) * M_DV] = (y * _sigmoid(og)).astype(out_ref.dtype)

    @pl.when(ci == pl.num_programs(1) - 1)
    def _():
        cfin_ref[0] = c_scr[...]
        nfin_ref[0] = n_scr[...]
        mfin_ref[0] = m_scr[...]


def mlstm(qkvo, gates, c0, n0, m0, norm_g):
    b, s, w = qkvo.shape
    lc = _pick(s, (256, 128, 64))
    v_w = M_HEADS * M_DV
    gates_t = jnp.swapaxes(gates[:, :, :2 * M_HEADS], 1, 2)
    kern = functools.partial(_mlstm_kernel, lc=lc)
    c_spec = pl.BlockSpec((1, M_HEADS, M_DK, M_DV), lambda bi, ci: (bi, 0, 0, 0))
    n_spec = pl.BlockSpec((1, M_HEADS, M_DK), lambda bi, ci: (bi, 0, 0))
    m_spec = pl.BlockSpec((1, M_HEADS, LANES), lambda bi, ci: (bi, 0, 0))
    hh, c_t, n, m = pl.pallas_call(
        kern,
        grid=(b, s // lc),
        in_specs=[pl.BlockSpec((1, lc, w), lambda bi, ci: (bi, ci, 0)),
                  pl.BlockSpec((1, lc, LANES), lambda bi, ci: (bi, ci, 0)),
                  pl.BlockSpec((1, 2 * M_HEADS, lc), lambda bi, ci: (bi, 0, ci)),
                  c_spec, n_spec, m_spec,
                  pl.BlockSpec((1, v_w), lambda bi, ci: (0, 0))],
        out_specs=[pl.BlockSpec((1, lc, v_w), lambda bi, ci: (bi, ci, 0)), c_spec, n_spec, m_spec],
        out_shape=[jax.ShapeDtypeStruct((b, s, v_w), BF16),
                   jax.ShapeDtypeStruct((b, M_HEADS, M_DK, M_DV), F32),
                   jax.ShapeDtypeStruct((b, M_HEADS, M_DK), F32),
                   jax.ShapeDtypeStruct((b, M_HEADS, LANES), F32)],
        scratch_shapes=[pltpu.VMEM((M_HEADS, M_DK, M_DV), F32), pltpu.VMEM((M_HEADS, M_DK), F32),
                        pltpu.VMEM((M_HEADS, LANES), F32)],
        compiler_params=_cparams(("parallel", "arbitrary"), 48),
        name="mlstm",
    )(qkvo, gates, gates_t, jnp.swapaxes(c0, 2, 3), n0, jnp.broadcast_to(m0[..., None], m0.shape + (LANES,)),
      norm_g.reshape(1, v_w).astype(F32))
    return hh, jnp.swapaxes(c_t, 2, 3), n, m[..., 0]


def _residual_out(acc, x_ref, gp_ref, gn_ref, xo_ref, ho_ref, rows=slice(None)):
    xn = x_ref[rows, :] + _rms(acc, gp_ref[...])
    xo_ref[rows, :] = xn
    if ho_ref is not None:
        ho_ref[rows, :] = _rms(xn, gn_ref[...]).astype(ho_ref.dtype)


def _row_chains(tm):
    n = ROW_CHAINS if tm % (ROW_CHAINS * 128) == 0 else 1
    return [slice(c * (tm // n), (c + 1) * (tm // n)) for c in range(n)]


def _outproj_kernel(*refs, n_in):
    a_refs, w_refs = refs[:n_in], refs[n_in:2 * n_in]
    x_ref, gp_ref, gn_ref, xo_ref, ho_ref = refs[2 * n_in:]
    for rows in _row_chains(x_ref.shape[0]):
        acc = _dot(a_refs[0][rows, :], w_refs[0][...])
        for a_ref, w_ref in zip(a_refs[1:], w_refs[1:]):
            acc = acc + _dot(a_ref[rows, :], w_ref[...])
        _residual_out(acc, x_ref, gp_ref, gn_ref, xo_ref, ho_ref, rows)


def outproj(acts, w, x, g_post, g_next):
    m, d = x.shape
    tm = _pick(m, (512, 256, 128, 64))
    n_in = len(acts)
    row = lambda wd: pl.BlockSpec((tm, wd), lambda i: (i, 0))
    vec = pl.BlockSpec((1, d), lambda i: (0, 0))
    w_specs, r0 = [], 0
    for a in acts:
        ka = a.shape[1]
        assert r0 % ka == 0
        w_specs.append(pl.BlockSpec((ka, d), functools.partial(lambda blk, i: (blk, 0), r0 // ka)))
        r0 += ka
    assert r0 == w.shape[0]
    return pl.pallas_call(
        functools.partial(_outproj_kernel, n_in=n_in),
        grid=(m // tm,),
        in_specs=[row(a.shape[1]) for a in acts] + w_specs + [row(d), vec, vec],
        out_specs=[row(d), row(d)],
        out_shape=[jax.ShapeDtypeStruct((m, d), F32), jax.ShapeDtypeStruct((m, d), BF16)],
        compiler_params=_cparams(("parallel",), 56),
        name="outproj",
    )(*acts, *([w] * n_in), x, g_post.reshape(1, d), g_next.reshape(1, d))


def _xattn_kernel(h_ref, wq_ref, mk_ref, mv_ref, wo_ref, x_ref, gp_ref, gn_ref, xo_ref, ho_ref):
    mk, mv = mk_ref[0], mv_ref[0]
    scale = X_DH ** -0.5
    for rows in _row_chains(x_ref.shape[0]):
        q = _dot(h_ref[rows, :], wq_ref[...]).astype(BF16)
        outs = []
        for hd in range(X_HEADS):
            sl = slice(hd * X_DH, (hd + 1) * X_DH)
            s = _dot_nt(q[:, sl], mk[:, sl]) * scale
            p = jnp.exp(s - jnp.max(s, axis=-1, keepdims=True))
            p = p / jnp.sum(p, axis=-1, keepdims=True)
            outs.append(_dot(p.astype(BF16), mv[:, sl]).astype(BF16))
        o = jnp.concatenate(outs, axis=-1)
        _residual_out(_dot(o, wo_ref[...]), x_ref, gp_ref, gn_ref, xo_ref, ho_ref, rows)


def xattn_block(h, x, mem_k, mem_v, wq, wo, g_post, g_next, seq):
    m, d = x.shape
    tm = _pick(seq, (512, 256, 128, 64))
    per_b = seq // tm
    t, xw = mem_k.shape[1], mem_k.shape[2]
    full = lambda a: pl.BlockSpec(a.shape, lambda i: (0,) * a.ndim)
    row = lambda w: pl.BlockSpec((tm, w), lambda i: (i, 0))
    mem = pl.BlockSpec((1, t, xw), lambda i: (i // per_b, 0, 0))
    vec = pl.BlockSpec((1, d), lambda i: (0, 0))
    return pl.pallas_call(
        _xattn_kernel,
        grid=(m // tm,),
        in_specs=[row(d), full(wq), mem, mem, full(wo), row(d), vec, vec],
        out_specs=[row(d), row(d)],
        out_shape=[jax.ShapeDtypeStruct((m, d), F32), jax.ShapeDtypeStruct((m, d), BF16)],
        compiler_params=_cparams(("parallel",), 48),
        name="xattn_block",
    )(h, wq, mem_k, mem_v, wo, x, g_post.reshape(1, d), g_next.reshape(1, d))


def _mlp_kernel(h_ref, w1_ref, w2_ref, x_ref, gp_ref, gn_ref, xo_ref, *rest, emit_h):
    ho_ref, acc_scr = (rest[0], rest[1]) if emit_h else (None, rest[0])
    f = pl.program_id(1)
    @pl.when(f == 0)
    def _():
        acc_scr[...] = jnp.zeros(acc_scr.shape, F32)

    a = jnp.maximum(_dot(h_ref[...], w1_ref[...]), 0.0)
    acc_scr[...] += _dot((a * a).astype(BF16), w2_ref[...])

    @pl.when(f == pl.num_programs(1) - 1)
    def _():
        _residual_out(acc_scr[...], x_ref, gp_ref, gn_ref, xo_ref, ho_ref)


def mlp_block(h, x, w1, w2, g_post, g_next):
    m, d = x.shape
    ff = w1.shape[1]
    tm = _pick(m, (512, 256, 128, 64))
    tf = _pick(ff, (1024, 512, 256, 128))
    emit_h = g_next is not None
    row = pl.BlockSpec((tm, d), lambda i, f: (i, 0))
    vec = pl.BlockSpec((1, d), lambda i, f: (0, 0))
    out_specs = [row] + ([row] if emit_h else [])
    out_shape = [jax.ShapeDtypeStruct((m, d), F32)] + ([jax.ShapeDtypeStruct((m, d), BF16)] if emit_h else [])
    g_n = (g_next if emit_h else g_post).reshape(1, d)
    res = pl.pallas_call(
        functools.partial(_mlp_kernel, emit_h=emit_h),
        grid=(m // tm, ff // tf),
        in_specs=[row, pl.BlockSpec((d, tf), lambda i, f: (0, f)), pl.BlockSpec((tf, d), lambda i, f: (f, 0)),
                  row, vec, vec],
        out_specs=out_specs,
        out_shape=out_shape,
        scratch_shapes=[pltpu.VMEM((tm, d), F32)],
        compiler_params=_cparams(("parallel", "arbitrary"), 56),
        name="mlp_block",
    )(h, w1, w2, x, g_post.reshape(1, d), g_n)
    return (res[0], res[1]) if emit_h else (res[0], None)


def _rope_tables(pos):
    def angles(half):
        inv = jnp.power(ROPE_BASE, -jnp.arange(half, dtype=F32) / half)
        return pos.astype(F32)[:, None] * inv[None, :]
    a = angles(RET_DK // 2)
    c, s = jnp.cos(a), jnp.sin(a)
    t128 = (jnp.concatenate([c, c], axis=1), jnp.concatenate([-s, s], axis=1))
    a = angles(MLA_ROPE // 2)
    c, s = jnp.cos(a), jnp.sin(a)
    z = jnp.zeros((pos.shape[0], LANES - MLA_ROPE), F32)
    t64 = (jnp.concatenate([c, c, z], axis=1), jnp.concatenate([s, s, z], axis=1))
    return t128, t64


def _prep_weights(w_in_even, w_uq, w_ukv, w_out_even, w_in_odd, b_gates_odd, w_out_odd, w_xq, w_xo, w_mlp1, w_mlp2):
    d = w_in_even.shape[1]
    ret_w = RET_HEADS * RET_DK
    n_qkvg = 2 * ret_w + 2 * RET_HEADS * RET_DV
    n_lat = MLA_Q_LORA + MLA_KV_LORA + MLA_ROPE
    even = []
    for e in range(w_in_even.shape[0]):
        w = w_in_even[e].astype(BF16)
        wlat = jnp.pad(w[:, n_qkvg:n_qkvg + n_lat], ((0, 0), (0, LANES - MLA_ROPE)))
        uq = w_uq[e].reshape(MLA_Q_LORA, MLA_HEADS, MLA_NOPE + MLA_ROPE)
        uq_nope = uq[:, :, :MLA_NOPE].reshape(MLA_Q_LORA, MLA_HEADS * MLA_NOPE)
        uq_rope = jnp.pad(uq[:, :, MLA_NOPE:], ((0, 0), (0, 0), (0, LANES - MLA_ROPE)))
        uq_rope = uq_rope.reshape(MLA_Q_LORA, MLA_HEADS * LANES)
        ukv = w_ukv[e].reshape(MLA_KV_LORA, MLA_HEADS, MLA_NOPE + MLA_V)
        even.append(dict(
            w_in=w,
            w_lat=wlat,
            w_uq=jnp.concatenate([uq_nope, uq_rope], axis=1).astype(BF16),
            w_uk=ukv[:, :, :MLA_NOPE].reshape(MLA_KV_LORA, MLA_HEADS * MLA_NOPE).astype(BF16),
            w_uv=ukv[:, :, MLA_NOPE:].reshape(MLA_KV_LORA, MLA_HEADS * MLA_V).astype(BF16),
            w_out=w_out_even[e].astype(BF16),
        ))
    m_qk = M_HEADS * M_DK
    n_qkvo = 2 * m_qk + 2 * M_HEADS * M_DV
    odd = []
    for o in range(w_in_odd.shape[0]):
        w = w_in_odd[o].astype(BF16)
        odd.append(dict(
            w_in=w,
            w_gates=jnp.pad(w[:, n_qkvo:], ((0, 0), (0, LANES - 2 * M_HEADS))),
            b_gates=jnp.pad(b_gates_odd[o], (0, LANES - 2 * M_HEADS)),
            w_out=w_out_odd[o].astype(BF16),
        ))
    q_scale = jnp.concatenate([jnp.full((m_qk,), M_DK ** -0.5, F32), jnp.ones((n_qkvo - m_qk,), F32)])
    rk_scale = jnp.concatenate([jnp.ones((ret_w,), F32), jnp.full((ret_w,), RET_DK ** -0.5, F32)])
    shared = dict(w_xq=w_xq.astype(BF16), w_xo=w_xo.astype(BF16), w_mlp1=w_mlp1.astype(BF16),
                  w_mlp2=w_mlp2.astype(BF16), q_scale=q_scale, rk_scale=rk_scale, d=d,
                  n_qk=2 * ret_w, n_vg=n_qkvg - 2 * ret_w, n_qkvo=n_qkvo)
    return even, odd, shared


def _run_trunk(x, pos, mem_k, mem_v, ret_s0, c0, n0, m0, past, norm_g, gains, even_w, odd_w, shared):
    b, s, d = x.shape
    m = b * s
    depth = norm_g.shape[0]
    t128, t64 = _rope_tables(pos)
    if m % s or s % 64:
        raise ValueError("unsupported sequence length")
    if s < 256:
        t128 = tuple(jnp.tile(t, (b, 1)) for t in t128)
        t64 = tuple(jnp.tile(t, (b, 1)) for t in t64)
    xf = x.reshape(m, d)
    h = rms_rows(xf, norm_g[0, 0], BF16)
    ckvs, krs, rets, cs, ns, ms = [], [], [], [], [], []
    for layer in range(depth):
        g = norm_g[layer]
        if layer % 2 == 0:
            e = layer // 2
            w = even_w[e]
            qk = matmul(h, w["w_in"], BF16, n_cols=shared["n_qk"], col_scale=shared["rk_scale"], rope=t128)
            vg = matmul(h, w["w_in"], BF16, col0=shared["n_qk"], n_cols=shared["n_vg"])
            ret, ret_s = retention(qk.reshape(b, s, -1), vg.reshape(b, s, -1), ret_s0[e], gains["ret_gn_g"][e])
            qcat, kcat, v, ckv, kr = even_latent(h, w["w_lat"], gains["mla_q_norm_g"][e].reshape(1, -1),
                                                 gains["mla_kv_norm_g"][e].reshape(1, -1),
                                                 w["w_uq"], w["w_uk"], w["w_uv"], t64)
            kcat = kcat.reshape(b, s, -1)
            v = v.reshape(b, s, -1)
            if past is None:
                q_off, sk = 0, s
            else:
                p_ckv, p_krp = past[e]
                p_len = p_ckv.shape[1]
                pk, pv = past_kv(p_ckv.reshape(b * p_len, -1), p_krp.reshape(b * p_len, -1), w["w_uk"], w["w_uv"])
                q_off, sk = p_len, p_len + s
                pad = (-sk) % 256
                kcat = jnp.concatenate([pk.reshape(b, p_len, -1), kcat,
                                        jnp.zeros((b, pad, kcat.shape[-1]), BF16)], axis=1)
                v = jnp.concatenate([pv.reshape(b, p_len, -1), v, jnp.zeros((b, pad, v.shape[-1]), BF16)], axis=1)
            att = mla_attention(qcat.reshape(b, s, -1), kcat, v, q_off=q_off, sk_valid=sk)
            xf, h = outproj([ret.reshape(m, -1), att.reshape(m, -1)], w["w_out"], xf, g[1], g[2])
            rets.append(ret_s)
            ckvs.append(ckv.reshape(b, s, -1))
            krs.append(kr.reshape(b, s, -1))
        else:
            o = layer // 2
            w = odd_w[o]
            qkvo = matmul(h, w["w_in"], BF16, n_cols=shared["n_qkvo"], col_scale=shared["q_scale"])
            gates = matmul(h, w["w_gates"], F32, col_bias=w["b_gates"])
            hh, c, n, mm = mlstm(qkvo.reshape(b, s, -1), gates.reshape(b, s, -1), c0[o], n0[o], m0[o],
                                 gains["mlstm_norm_g"][o])
            xf, h = outproj([hh.reshape(m, -1)], w["w_out"], xf, g[1], g[2])
            cs.append(c)
            ns.append(n)
            ms.append(mm)
        xf, h = xattn_block(h, xf, mem_k[layer], mem_v[layer], shared["w_xq"][layer], shared["w_xo"][layer],
                            g[3], g[4], s)
        g_next = norm_g[layer + 1, 0] if layer + 1 < depth else None
        xf, h = mlp_block(h, xf, shared["w_mlp1"][layer], shared["w_mlp2"][layer], g[5], g_next)
    return (xf.reshape(b, s, d), jnp.stack(ckvs), jnp.stack(krs), jnp.stack(rets),
            jnp.stack(cs), jnp.stack(ns), jnp.stack(ms))


def kernel(x_prompt, x_sample, cache_mla_ckv, cache_mla_krope, state_ret, state_mlstm_C, state_mlstm_n, state_mlstm_m, cache_mem_k, cache_mem_v, mem_prompt, norm_g, mem_norm_g, w_in_even, mla_q_norm_g, mla_kv_norm_g, w_uq, w_ukv, ret_gn_g, w_out_even, w_in_odd, b_gates_odd, mlstm_norm_g, w_out_odd, w_xq, w_xk, w_xv, w_xo, w_mlp1, w_mlp2):
    even_w, odd_w, shared = _prep_weights(w_in_even, w_uq, w_ukv, w_out_even, w_in_odd, b_gates_odd, w_out_odd,
                                          w_xq, w_xo, w_mlp1, w_mlp2)
    gains = dict(ret_gn_g=ret_gn_g, mla_q_norm_g=mla_q_norm_g, mla_kv_norm_g=mla_kv_norm_g,
                 mlstm_norm_g=mlstm_norm_g)
    depth = norm_g.shape[0]
    b, s, d = x_prompt.shape
    n_even, n_odd = w_in_even.shape[0], w_in_odd.shape[0]
    xw = X_HEADS * X_DH

    bm, t, _ = mem_prompt.shape
    mem_flat = mem_prompt.reshape(bm * t, d)
    p_mem_k, p_mem_v = [], []
    for layer in range(depth):
        mn = rms_rows(mem_flat, mem_norm_g[layer], BF16)
        kv = matmul(mn, jnp.concatenate([w_xk[layer], w_xv[layer]], axis=1).astype(BF16), F32)
        p_mem_k.append(kv[:, :xw].reshape(bm, t, X_HEADS, X_DH))
        p_mem_v.append(kv[:, xw:].reshape(bm, t, X_HEADS, X_DH))
    p_mem_k, p_mem_v = jnp.stack(p_mem_k), jnp.stack(p_mem_v)

    pos_p = jnp.arange(s, dtype=jnp.int32)
    zeros = lambda *shape: jnp.zeros(shape, F32)
    y_prompt, p_ckv, p_kr, p_ret, p_c, p_n, p_m = _run_trunk(
        x_prompt, pos_p, p_mem_k.reshape(depth, bm, t, xw).astype(BF16), p_mem_v.reshape(depth, bm, t, xw).astype(BF16),
        zeros(n_even, b, RET_HEADS, RET_DK, RET_DV), zeros(n_odd, b, M_HEADS, M_DV, M_DK),
        zeros(n_odd, b, M_HEADS, M_DK), zeros(n_odd, b, M_HEADS), None, norm_g, gains, even_w, odd_w, shared)

    db, ds, _ = x_sample.shape
    past_len = cache_mla_ckv.shape[2]
    pos_s = past_len + jnp.arange(ds, dtype=jnp.int32)
    past = [(cache_mla_ckv[e], jnp.pad(cache_mla_krope[e], ((0, 0), (0, 0), (0, LANES - MLA_ROPE))))
            for e in range(n_even)]
    y_sample, s_ckv, s_kr, s_ret, s_c, s_n, s_m = _run_trunk(
        x_sample, pos_s, cache_mem_k.reshape(depth, db, -1, xw).astype(BF16),
        cache_mem_v.reshape(depth, db, -1, xw).astype(BF16),
        state_ret, state_mlstm_C, state_mlstm_n, state_mlstm_m, past, norm_g, gains, even_w, odd_w, shared)

    return (y_prompt, y_sample, p_ckv, p_kr, p_ret, p_c, p_n, p_m, p_mem_k, p_mem_v,
            s_ckv, s_kr, s_ret, s_c, s_n, s_m)
```

```python
import functools

import jax
import jax.numpy as jnp
from jax import lax
from jax.experimental import pallas as pl
from jax.experimental.pallas import tpu as pltpu

F32 = jnp.float32
BF16 = jnp.bfloat16

RMS_EPS = 1e-6
ROPE_BASE = 10000.0
CHUNK = 64
CHUNK_SHIFT = 6
assert 1 << CHUNK_SHIFT == CHUNK

RET_HEADS, RET_DK, RET_DV = 8, 128, 128
MLA_HEADS, MLA_NOPE, MLA_ROPE, MLA_V = 8, 128, 64, 128
MLA_Q_LORA, MLA_KV_LORA = 768, 512
MLA_QK_PAD = 256
MLA_HEADS_PER_STEP = 2
MLA_Q_PRESCALE = (MLA_NOPE + MLA_ROPE) ** -0.5 * 1.4426950408889634
M_HEADS, M_DK, M_DV = 8, 128, 256
X_HEADS, X_DH = 4, 128
X_Q_PRESCALE = X_DH ** -0.5 * 1.4426950408889634
LANES = 128
ROW_CHAINS = 2

V7X_VMEM_BYTES = 64 * 1024 * 1024
MIB = 1024 * 1024


def _cparams(semantics, vmem_mib):
    assert vmem_mib * MIB < V7X_VMEM_BYTES
    return pltpu.CompilerParams(dimension_semantics=semantics, vmem_limit_bytes=vmem_mib * MIB)


def _pick(n, cands):
    for c in cands:
        if c <= n and n % c == 0:
            return c
    raise ValueError(f"no tile for {n} in {cands}")


def _rms(x, g):
    ms = jnp.mean(x * x, axis=-1, keepdims=True)
    return x * lax.rsqrt(ms + RMS_EPS) * g


def _sigmoid(x):
    return 1.0 / (1.0 + jnp.exp(-x))


def _dot(a, b):
    return jnp.dot(a, b, preferred_element_type=F32)


def _dot_nt(a, b):
    return lax.dot_general(a, b, (((1,), (1,)), ((), ())), preferred_element_type=F32)


def _dot_tn(a, b):
    return lax.dot_general(a, b, (((0,), (0,)), ((), ())), preferred_element_type=F32)


def _rope128(x, cos2, sin2):
    return x * cos2 + pltpu.roll(x, 64, 1) * sin2


def _rope64(x, cos_p, sin_p):
    return x * cos_p + (pltpu.roll(x, 32, 1) - pltpu.roll(x, 96, 1)) * sin_p


def _norm_kernel(x_ref, g_ref, o_ref):
    o_ref[...] = _rms(x_ref[...].astype(F32), g_ref[...]).astype(o_ref.dtype)


def rms_rows(x, g, out_dtype):
    m, k = x.shape
    tm = _pick(m, (512, 256, 128, 64, 32, 16, 8))
    return pl.pallas_call(
        _norm_kernel,
        grid=(m // tm,),
        in_specs=[pl.BlockSpec((tm, k), lambda i: (i, 0)), pl.BlockSpec((1, k), lambda i: (0, 0))],
        out_specs=pl.BlockSpec((tm, k), lambda i: (i, 0)),
        out_shape=jax.ShapeDtypeStruct((m, k), out_dtype),
        compiler_params=_cparams(("parallel",), 32),
        name="rms_rows",
    )(x, g.reshape(1, k).astype(F32))


def _mm_kernel(*refs, has_scale, has_bias, n_rope, tn):
    it = iter(refs)
    h_ref, w_ref = next(it), next(it)
    scale_ref = next(it) if has_scale else None
    bias_ref = next(it) if has_bias else None
    cos_ref, sin_ref = (next(it), next(it)) if n_rope else (None, None)
    o_ref = next(it)
    acc = _dot(h_ref[...], w_ref[...])
    if has_scale:
        acc = acc * scale_ref[...]
    if has_bias:
        acc = acc + bias_ref[...]
    if n_rope:
        c, s = cos_ref[...], sin_ref[...]
        for t in range(tn // LANES):
            sl = slice(t * LANES, (t + 1) * LANES)
            o_ref[:, sl] = _rope128(acc[:, sl], c, s).astype(o_ref.dtype)
    else:
        o_ref[...] = acc.astype(o_ref.dtype)


def matmul(h, w, out_dtype, *, col0=0, n_cols=None, col_scale=None, col_bias=None, rope=None):
    m, k = h.shape
    n = w.shape[1] - col0 if n_cols is None else n_cols
    tm = _pick(m, (1024, 512, 256, 128, 64))
    tn = _pick(n, (512, 256, 128))
    assert col0 % tn == 0 and col0 + n <= w.shape[1]
    jb = col0 // tn
    args = [h, w]
    in_specs = [pl.BlockSpec((tm, k), lambda i, j: (i, 0)), pl.BlockSpec((k, tn), lambda i, j: (0, j + jb))]
    for vec in (col_scale, col_bias):
        if vec is not None:
            args.append(vec.reshape(1, n).astype(F32))
            in_specs.append(pl.BlockSpec((1, tn), lambda i, j: (0, j)))
    if rope is not None:
        r = rope[0].shape[0]
        tm = _pick(m, tuple(c for c in (1024, 512, 256, 128, 64) if r % c == 0))
        in_specs[0] = pl.BlockSpec((tm, k), lambda i, j: (i, 0))
        nrb = r // tm
        for tab in rope:
            args.append(tab)
            in_specs.append(pl.BlockSpec((tm, LANES), lambda i, j: (i % nrb, 0)))
    kern = functools.partial(_mm_kernel, has_scale=col_scale is not None, has_bias=col_bias is not None,
                             n_rope=rope is not None, tn=tn)
    return pl.pallas_call(
        kern,
        grid=(m // tm, n // tn),
        in_specs=in_specs,
        out_specs=pl.BlockSpec((tm, tn), lambda i, j: (i, j)),
        out_shape=jax.ShapeDtypeStruct((m, n), out_dtype),
        compiler_params=_cparams(("parallel", "arbitrary"), 40),
        name="matmul",
    )(*args)


def _kv_expand(ckvn_bf16, krr_bf16, wuk_ref, wuv_ref, kcat_ref, v_ref):
    kn = _dot(ckvn_bf16, wuk_ref[...])
    for hd in range(MLA_HEADS):
        kcat_ref[:, hd * MLA_QK_PAD: hd * MLA_QK_PAD + MLA_NOPE] = (
            kn[:, hd * MLA_NOPE:(hd + 1) * MLA_NOPE].astype(BF16))
        kcat_ref[:, hd * MLA_QK_PAD + MLA_NOPE:(hd + 1) * MLA_QK_PAD] = krr_bf16
    v_ref[...] = _dot(ckvn_bf16, wuv_ref[...]).astype(BF16)


def _even_latent_kernel(h_ref, wlat_ref, gq_ref, gkv_ref, wuq_ref, wuk_ref, wuv_ref, cos_ref, sin_ref, *refs):
    qcat_ref, kcat_ref, v_ref, ckv_ref, kr_ref = refs[-5:]
    lat = _dot(h_ref[...], wlat_ref[...])
    cq = lat[:, :MLA_Q_LORA]
    ckv = lat[:, MLA_Q_LORA:MLA_Q_LORA + MLA_KV_LORA]
    krp = lat[:, MLA_Q_LORA + MLA_KV_LORA:]
    c, s = cos_ref[...], sin_ref[...]
    q = _dot(_rms(cq, gq_ref[...]).astype(BF16), wuq_ref[...]) * MLA_Q_PRESCALE
    nope_w = MLA_HEADS * MLA_NOPE
    for hd in range(MLA_HEADS):
        qcat_ref[:, hd * MLA_QK_PAD: hd * MLA_QK_PAD + MLA_NOPE] = (
            q[:, hd * MLA_NOPE:(hd + 1) * MLA_NOPE].astype(BF16))
        qr = q[:, nope_w + hd * LANES: nope_w + (hd + 1) * LANES]
        qcat_ref[:, hd * MLA_QK_PAD + MLA_NOPE:(hd + 1) * MLA_QK_PAD] = _rope64(qr, c, s).astype(BF16)
    ckvn = _rms(ckv, gkv_ref[...])
    ckv_ref[...] = ckvn
    krr = _rope64(krp, c, s)
    kr_ref[...] = krr[:, :MLA_ROPE]
    _kv_expand(ckvn.astype(BF16), krr.astype(BF16), wuk_ref, wuv_ref, kcat_ref, v_ref)


def even_latent(h, wlat, gq, gkv, wuq, wuk, wuv, rope64, e, n_even, ckv_slab):
    m, d = h.shape
    r = rope64[0].shape[0]
    tm = _pick(m, tuple(c for c in (256, 128, 64) if r % c == 0))
    nrb = r // tm
    full = lambda a: pl.BlockSpec(a.shape, lambda i: (0,) * a.ndim)
    row = lambda w: pl.BlockSpec((tm, w), lambda i: (i, 0))
    tab = pl.BlockSpec((tm, LANES), lambda i: (i % nrb, 0))
    qk_w = MLA_HEADS * MLA_QK_PAD
    v_w = MLA_HEADS * MLA_V
    args = [h, wlat, gq, gkv, wuq, wuk, wuv, rope64[0], rope64[1]]
    in_specs = [row(d), full(wlat), full(gq), full(gkv), full(wuq), full(wuk), full(wuv), tab, tab]
    aliases = {}
    if ckv_slab is not None:
        aliases = {len(args): 3}
        args.append(ckv_slab)
        in_specs.append(pl.BlockSpec(memory_space=pl.ANY))
    return pl.pallas_call(
        _even_latent_kernel,
        grid=(m // tm,),
        in_specs=in_specs,
        out_specs=[row(qk_w), row(qk_w), row(v_w),
                   pl.BlockSpec((None, tm, MLA_KV_LORA), lambda i: (e, i, 0)), row(MLA_ROPE)],
        out_shape=[jax.ShapeDtypeStruct((m, qk_w), BF16), jax.ShapeDtypeStruct((m, qk_w), BF16),
                   jax.ShapeDtypeStruct((m, v_w), BF16), jax.ShapeDtypeStruct((n_even, m, MLA_KV_LORA), F32),
                   jax.ShapeDtypeStruct((m, MLA_ROPE), F32)],
        input_output_aliases=aliases,
        compiler_params=_cparams(("parallel",), 48),
        name="even_latent",
    )(*args)


def _past_kv_kernel(ckv_ref, krp_ref, wuk_ref, wuv_ref, kcat_ref, v_ref):
    _kv_expand(ckv_ref[...].astype(BF16), krp_ref[...].astype(BF16), wuk_ref, wuv_ref, kcat_ref, v_ref)


def past_kv(ckvn, kr_pad, e, wuk, wuv, skp):
    _, b, p_len, _ = ckvn.shape
    tm = _pick(p_len, (512, 256, 128, 64))
    full = lambda a: pl.BlockSpec(a.shape, lambda bi, i: (0,) * a.ndim)
    src = lambda w: pl.BlockSpec((None, None, tm, w), lambda bi, i: (e, bi, i, 0))
    dst = lambda w: pl.BlockSpec((None, tm, w), lambda bi, i: (bi, i, 0))
    qk_w = MLA_HEADS * MLA_QK_PAD
    v_w = MLA_HEADS * MLA_V
    return pl.pallas_call(
        _past_kv_kernel,
        grid=(b, p_len // tm),
        in_specs=[src(MLA_KV_LORA), src(LANES), full(wuk), full(wuv)],
        out_specs=[dst(qk_w), dst(v_w)],
        out_shape=[jax.ShapeDtypeStruct((b, skp, qk_w), BF16), jax.ShapeDtypeStruct((b, skp, v_w), BF16)],
        compiler_params=_cparams(("parallel", "parallel"), 32),
        name="past_kv",
    )(ckvn, kr_pad, wuk, wuv)


def _mla_kernel(q_ref, k_ref, v_ref, o_ref, m_scr, l_scr, acc_scr, a_scr, p_scr, *, tq, tk, hb, q_off, sk_valid):
    qi = pl.program_id(2)
    q0 = qi * tq
    cq_lo = (q_off + q0) // CHUNK
    cq_hi = (q_off + q0 + tq - 1) // CHUNK
    n_full = jnp.minimum(lax.div((cq_lo + 1) * CHUNK, tk), sk_valid // tk)
    n_vis = lax.div(jnp.minimum((cq_hi + 1) * CHUNK, sk_valid) + tk - 1, tk)
    m_scr[...] = jnp.full(m_scr.shape, -1e30, F32)
    l_scr[...] = jnp.zeros(l_scr.shape, F32)
    acc_scr[...] = jnp.zeros(acc_scr.shape, F32)
    a_scr[...] = jnp.ones(a_scr.shape, F32)
    p_scr[...] = jnp.zeros(p_scr.shape, BF16)

    def apply_prev(j, ki_prev):
        v = v_ref[0, pl.ds(pl.multiple_of(ki_prev * tk, tk), tk), j * MLA_V:(j + 1) * MLA_V]
        acc_scr[j] = a_scr[j] * acc_scr[j] + _dot(p_scr[j], v)

    def step(ki, masked):
        k0 = pl.multiple_of(ki * tk, tk)
        ki_prev = jnp.maximum(ki - 1, 0)
        if masked:
            qpos = q_off + q0 + lax.broadcasted_iota(jnp.int32, (tq, tk), 0)
            kpos = k0 + lax.broadcasted_iota(jnp.int32, (tq, tk), 1)
            vis = jnp.logical_and((kpos >> CHUNK_SHIFT) <= (qpos >> CHUNK_SHIFT), kpos < sk_valid)
        for j in range(hb):
            q = q_ref[0, :, j * MLA_QK_PAD:(j + 1) * MLA_QK_PAD]
            k = k_ref[0, pl.ds(k0, tk), j * MLA_QK_PAD:(j + 1) * MLA_QK_PAD]
            s = _dot_nt(q, k)
            apply_prev(j, ki_prev)
            if masked:
                s = jnp.where(vis, s, -jnp.inf)
            m_prev = m_scr[j]
            m_new = jnp.maximum(m_prev, jnp.max(s, axis=-1, keepdims=True))
            alpha = jnp.exp2(m_prev - m_new)
            p = [jnp.exp2(s[:, c * LANES:(c + 1) * LANES] - m_new) for c in range(tk // LANES)]
            l_scr[j] = alpha * l_scr[j] + sum(p[1:], p[0])
            p_scr[j] = jnp.concatenate([pc.astype(BF16) for pc in p], axis=-1)
            a_scr[j] = alpha
            m_scr[j] = m_new

    def body_full(ki, carry):
        step(ki, False)
        return carry

    def body_masked(ki, carry):
        step(ki, True)
        return carry

    lax.fori_loop(0, n_full, body_full, 0)
    lax.fori_loop(n_full, n_vis, body_masked, 0)
    for j in range(hb):
        apply_prev(j, n_vis - 1)
        l = jnp.sum(l_scr[j], axis=-1, keepdims=True)
        o_ref[0, :, j * MLA_V:(j + 1) * MLA_V] = (acc_scr[j] / l).astype(o_ref.dtype)


def mla_attention(qcat, kcat, v, *, q_off, sk_valid):
    b, sq, _ = qcat.shape
    skp = kcat.shape[1]
    tq = _pick(sq, (512, 256, 128, 64))
    tk = skp if tq * skp <= 512 * 512 else _pick(skp, (512, 256, 128))
    hb = MLA_HEADS if tk == skp else MLA_HEADS_PER_STEP
    kern = functools.partial(_mla_kernel, tq=tq, tk=tk, hb=hb, q_off=q_off, sk_valid=sk_valid)
    return pl.pallas_call(
        kern,
        grid=(b, MLA_HEADS // hb, sq // tq),
        in_specs=[pl.BlockSpec((1, tq, hb * MLA_QK_PAD), lambda bi, h, qi: (bi, qi, h)),
                  pl.BlockSpec((1, skp, hb * MLA_QK_PAD), lambda bi, h, qi: (bi, 0, h)),
                  pl.BlockSpec((1, skp, hb * MLA_V), lambda bi, h, qi: (bi, 0, h))],
        out_specs=pl.BlockSpec((1, tq, hb * MLA_V), lambda bi, h, qi: (bi, qi, h)),
        out_shape=jax.ShapeDtypeStruct((b, sq, MLA_HEADS * MLA_V), BF16),
        scratch_shapes=[pltpu.VMEM((hb, tq, LANES), F32), pltpu.VMEM((hb, tq, LANES), F32),
                        pltpu.VMEM((hb, tq, MLA_V), F32), pltpu.VMEM((hb, tq, LANES), F32),
                        pltpu.VMEM((hb, tq, tk), BF16)],
        compiler_params=_cparams(("parallel", "parallel", "arbitrary"), 48),
        name="mla_attention",
    )(qcat, kcat, v)


def _retention_kernel(lg_ref, qk_ref, vg_ref, s0_ref, gn_ref, out_ref, sfin_ref,
                      state_scr, decay_scr, qdec_scr, kdec_scr, *, lc):
    ci = pl.program_id(1)
    heads = range(RET_HEADS)

    @pl.when(ci == 0)
    def _():
        state_scr[...] = s0_ref[0]
        ti = lax.broadcasted_iota(jnp.int32, (lc, lc), 0)
        si = lax.broadcasted_iota(jnp.int32, (lc, lc), 1)
        dpos = jnp.maximum(ti - si, 0).astype(F32)
        trow = lax.broadcasted_iota(jnp.int32, (lc, LANES), 0).astype(F32)
        for h in heads:
            lg = lg_ref[h:h + 1, :]
            decay_scr[h] = jnp.where(ti >= si, jnp.exp(lg[:, 0:1] * dpos), 0.0)
            qdec_scr[h] = jnp.exp(lg * (trow + 1.0))
            kdec_scr[h] = jnp.exp(lg * (lc - 1.0 - trow))

    qk_w = RET_HEADS * RET_DK
    v_w = RET_HEADS * RET_DV
    q = [qk_ref[0, :, h * RET_DK:(h + 1) * RET_DK] for h in heads]
    k = [qk_ref[0, :, qk_w + h * RET_DK: qk_w + (h + 1) * RET_DK] for h in heads]
    v = [vg_ref[0, :, h * RET_DV:(h + 1) * RET_DV] for h in heads]
    st = [state_scr[h] for h in heads]
    sc = [_dot_nt(q[h], k[h]) for h in heads]
    inter = [_dot(q[h], st[h].astype(BF16)) for h in heads]
    scd = [(sc[h] * decay_scr[h]).astype(BF16) for h in heads]
    kd = [(k[h].astype(F32) * kdec_scr[h]).astype(BF16) for h in heads]
    intra = [_dot(scd[h], v[h]) for h in heads]
    upd = [_dot_tn(kd[h], v[h]) for h in heads]
    for h in heads:
        state_scr[h] = jnp.exp(lg_ref[h:h + 1, :] * lc) * st[h] + upd[h]
        out = intra[h] + qdec_scr[h] * inter[h]
        g = vg_ref[0, :, v_w + h * RET_DV: v_w + (h + 1) * RET_DV].astype(F32)
        xc = out - jnp.mean(out, axis=-1, keepdims=True)
        y = xc * lax.rsqrt(jnp.mean(xc * xc, axis=-1, keepdims=True) + RMS_EPS)
        y = y * gn_ref[:, h * RET_DV:(h + 1) * RET_DV]
        out_ref[0, :, h * RET_DV:(h + 1) * RET_DV] = (g * _sigmoid(g) * y).astype(out_ref.dtype)

    @pl.when(ci == pl.num_programs(1) - 1)
    def _():
        sfin_ref[0] = state_scr[...]


def retention(qk, vg, s0, gn_g):
    b, s, _ = qk.shape
    lc = _pick(s, (256, 128, 64))
    log_g = jnp.log1p(-jnp.exp2(-5.0 - jnp.arange(RET_HEADS, dtype=F32)))
    lg_tab = jnp.broadcast_to(log_g[:, None], (RET_HEADS, LANES))
    v_w = RET_HEADS * RET_DV
    kern = functools.partial(_retention_kernel, lc=lc)
    st_spec = pl.BlockSpec((1, RET_HEADS, RET_DK, RET_DV), lambda bi, ci: (bi, 0, 0, 0))
    return pl.pallas_call(
        kern,
        grid=(b, s // lc),
        in_specs=[pl.BlockSpec((RET_HEADS, LANES), lambda bi, ci: (0, 0)),
                  pl.BlockSpec((1, lc, qk.shape[2]), lambda bi, ci: (bi, ci, 0)),
                  pl.BlockSpec((1, lc, vg.shape[2]), lambda bi, ci: (bi, ci, 0)),
                  st_spec,
                  pl.BlockSpec((1, v_w), lambda bi, ci: (0, 0))],
        out_specs=[pl.BlockSpec((1, lc, v_w), lambda bi, ci: (bi, ci, 0)), st_spec],
        out_shape=[jax.ShapeDtypeStruct((b, s, v_w), BF16),
                   jax.ShapeDtypeStruct((b, RET_HEADS, RET_DK, RET_DV), F32)],
        scratch_shapes=[pltpu.VMEM((RET_HEADS, RET_DK, RET_DV), F32), pltpu.VMEM((RET_HEADS, lc, lc), F32),
                        pltpu.VMEM((RET_HEADS, lc, LANES), F32), pltpu.VMEM((RET_HEADS, lc, LANES), F32)],
        compiler_params=_cparams(("parallel", "arbitrary"), 40),
        name="retention",
    )(lg_tab, qk, vg, s0, gn_g.reshape(1, v_w).astype(F32))


def _split3(x):
    hi = x.astype(BF16)
    r1 = x - hi.astype(F32)
    mid = r1.astype(BF16)
    lo = (r1 - mid.astype(F32)).astype(BF16)
    return hi, mid, lo


def _log_sigmoid(x):
    return jnp.minimum(x, 0.0) - jnp.log(1.0 + jnp.exp(-jnp.abs(x)))


def _widen(x, width):
    if width < LANES:
        return x[:, :width]
    return x if width == LANES else jnp.concatenate([x] * (width // LANES), axis=1)


def _mlstm_kernel(qkvo_ref, gates_ref, gates_t_ref, c0_ref, n0_ref, m0_ref, ng_ref,
                  out_ref, cfin_ref, nfin_ref, mfin_ref, c_scr, n_scr, m_scr, *, lc):
    ci = pl.program_id(1)
    heads = range(M_HEADS)

    @pl.when(ci == 0)
    def _():
        c_scr[...] = c0_ref[0]
        n_scr[...] = n0_ref[0]
        m_scr[...] = m0_ref[0]

    gates = gates_ref[0]
    gates_t = gates_t_ref[0]
    ti = lax.broadcasted_iota(jnp.int32, (lc, lc), 0)
    si = lax.broadcasted_iota(jnp.int32, (lc, lc), 1)
    tril = si <= ti
    ones_l = jnp.where(tril, 1.0, 0.0).astype(BF16)
    ones_u = jnp.where(ti <= si, 1.0, 0.0).astype(BF16)
    b_col = sum(_dot(ones_l, part) for part in _split3(_log_sigmoid(gates)))
    b_row = sum(_dot(part, ones_u) for part in _split3(_log_sigmoid(gates_t)))[M_HEADS:2 * M_HEADS]
    i_row = gates_t[0:M_HEADS]
    qk_w = M_HEADS * M_DK
    v_w = M_HEADS * M_DV
    q = [qkvo_ref[0, :, h * M_DK:(h + 1) * M_DK] for h in heads]
    k = [qkvo_ref[0, :, qk_w + h * M_DK: qk_w + (h + 1) * M_DK] for h in heads]
    v = [qkvo_ref[0, :, 2 * qk_w + h * M_DV: 2 * qk_w + (h + 1) * M_DV] for h in heads]
    c_st = [c_scr[h] for h in heads]
    n_st = [n_scr[h:h + 1, :] for h in heads]
    m_prev = [m_scr[h:h + 1, :] for h in heads]
    qk = [_dot_nt(q[h], k[h]) for h in heads]
    qc = [_dot(q[h], c_st[h].astype(BF16)) for h in heads]
    bt = [jnp.broadcast_to(b_col[:, M_HEADS + h:M_HEADS + h + 1], (lc, LANES)) for h in heads]
    i_t = [jnp.broadcast_to(gates[:, h:h + 1], (lc, LANES)) for h in heads]
    m_t, w_inter, sc = [], [], []
    for h in heads:
        log_intra = jnp.where(tril, _widen(bt[h], lc) - b_row[h:h + 1, :] + i_row[h:h + 1, :], -jnp.inf)
        log_inter = bt[h] + m_prev[h]
        m_t.append(jnp.maximum(log_inter, jnp.max(log_intra, axis=-1, keepdims=True)))
        w_inter.append(jnp.exp(log_inter - m_t[h]))
        sc.append(qk[h] * jnp.exp(log_intra - _widen(m_t[h], lc)))
    num_intra = [_dot(sc[h].astype(BF16), v[h]) for h in heads]
    kw = []
    for h in heads:
        m_new = m_t[h][lc - 1:lc, :]
        b_last = bt[h][lc - 1:lc, :]
        w_state = jnp.exp(b_last + m_prev[h] - m_new)
        kw.append(k[h].astype(F32) * jnp.exp(b_last - bt[h] + i_t[h] - m_new))
        n_scr[h:h + 1, :] = w_state * n_st[h] + jnp.sum(kw[h], axis=0, keepdims=True)
        m_scr[h:h + 1, :] = m_new
        c_scr[h] = _widen(w_state, M_DV) * c_st[h] + _dot_tn(kw[h].astype(BF16), v[h])
    for h in heads:
        num = num_intra[h] + _widen(w_inter[h], M_DV) * qc[h]
        den = (jnp.sum(sc[h], axis=-1, keepdims=True)
               + w_inter[h] * jnp.sum(q[h].astype(F32) * n_st[h], axis=-1, keepdims=True))
        hh = num * _widen(1.0 / jnp.maximum(jnp.abs(den), jnp.exp(-m_t[h])), M_DV)
        y = hh * lax.rsqrt(jnp.mean(hh * hh, axis=-1, keepdims=True) + RMS_EPS)
        y = y * ng_ref[:, h * M_DV:(h + 1) * M_DV]
        og = qkvo_ref[0, :, 2 * qk_w + v_w + h * M_DV: 2 * qk_w + v_w + (h + 1) * M_DV].astype(F32)
        out_ref[0, :, h * M_DV:(h + 1) * M_DV] = (y * _sigmoid(og)).astype(out_ref.dtype)

    @pl.when(ci == pl.num_programs(1) - 1)
    def _():
        cfin_ref[0] = c_scr[...]
        nfin_ref[0] = n_scr[...]
        mfin_ref[0] = m_scr[...]


def mlstm(qkvo, gates, c0, n0, m0, norm_g):
    b, s, w = qkvo.shape
    lc = _pick(s, (256, 128, 64))
    v_w = M_HEADS * M_DV
    gates_t = jnp.swapaxes(gates[:, :, :2 * M_HEADS], 1, 2)
    kern = functools.partial(_mlstm_kernel, lc=lc)
    c_spec = pl.BlockSpec((1, M_HEADS, M_DK, M_DV), lambda bi, ci: (bi, 0, 0, 0))
    n_spec = pl.BlockSpec((1, M_HEADS, M_DK), lambda bi, ci: (bi, 0, 0))
    m_spec = pl.BlockSpec((1, M_HEADS, LANES), lambda bi, ci: (bi, 0, 0))
    hh, c_t, n, m = pl.pallas_call(
        kern,
        grid=(b, s // lc),
        in_specs=[pl.BlockSpec((1, lc, w), lambda bi, ci: (bi, ci, 0)),
                  pl.BlockSpec((1, lc, LANES), lambda bi, ci: (bi, ci, 0)),
                  pl.BlockSpec((1, 2 * M_HEADS, lc), lambda bi, ci: (bi, 0, ci)),
                  c_spec, n_spec, m_spec,
                  pl.BlockSpec((1, v_w), lambda bi, ci: (0, 0))],
        out_specs=[pl.BlockSpec((1, lc, v_w), lambda bi, ci: (bi, ci, 0)), c_spec, n_spec, m_spec],
        out_shape=[jax.ShapeDtypeStruct((b, s, v_w), BF16),
                   jax.ShapeDtypeStruct((b, M_HEADS, M_DK, M_DV), F32),
                   jax.ShapeDtypeStruct((b, M_HEADS, M_DK), F32),
                   jax.ShapeDtypeStruct((b, M_HEADS, LANES), F32)],
        scratch_shapes=[pltpu.VMEM((M_HEADS, M_DK, M_DV), F32), pltpu.VMEM((M_HEADS, M_DK), F32),
                        pltpu.VMEM((M_HEADS, LANES), F32)],
        compiler_params=_cparams(("parallel", "arbitrary"), 48),
        name="mlstm",
    )(qkvo, gates, gates_t, jnp.swapaxes(c0, 2, 3), n0, jnp.broadcast_to(m0[..., None], m0.shape + (LANES,)),
      norm_g.reshape(1, v_w).astype(F32))
    return hh, jnp.swapaxes(c_t, 2, 3), n, m[..., 0]


def _residual_out(acc, x_ref, gp_ref, gn_ref, xo_ref, ho_ref, rows=slice(None)):
    xn = x_ref[rows, :] + _rms(acc, gp_ref[...])
    xo_ref[rows, :] = xn
    if ho_ref is not None:
        ho_ref[rows, :] = _rms(xn, gn_ref[...]).astype(ho_ref.dtype)


def _row_chains(tm):
    n = ROW_CHAINS if tm % (ROW_CHAINS * 128) == 0 else 1
    return [slice(c * (tm // n), (c + 1) * (tm // n)) for c in range(n)]


def _outproj_kernel(*refs, n_in):
    a_refs, w_refs = refs[:n_in], refs[n_in:2 * n_in]
    x_ref, gp_ref, gn_ref, xo_ref, ho_ref = refs[2 * n_in:]
    for rows in _row_chains(x_ref.shape[0]):
        acc = _dot(a_refs[0][rows, :], w_refs[0][...])
        for a_ref, w_ref in zip(a_refs[1:], w_refs[1:]):
            acc = acc + _dot(a_ref[rows, :], w_ref[...])
        _residual_out(acc, x_ref, gp_ref, gn_ref, xo_ref, ho_ref, rows)


def outproj(acts, w, x, g_post, g_next):
    m, d = x.shape
    tm = _pick(m, (512, 256, 128, 64))
    n_in = len(acts)
    row = lambda wd: pl.BlockSpec((tm, wd), lambda i: (i, 0))
    vec = pl.BlockSpec((1, d), lambda i: (0, 0))
    w_specs, r0 = [], 0
    for a in acts:
        ka = a.shape[1]
        assert r0 % ka == 0
        w_specs.append(pl.BlockSpec((ka, d), functools.partial(lambda blk, i: (blk, 0), r0 // ka)))
        r0 += ka
    assert r0 == w.shape[0]
    return pl.pallas_call(
        functools.partial(_outproj_kernel, n_in=n_in),
        grid=(m // tm,),
        in_specs=[row(a.shape[1]) for a in acts] + w_specs + [row(d), vec, vec],
        out_specs=[row(d), row(d)],
        out_shape=[jax.ShapeDtypeStruct((m, d), F32), jax.ShapeDtypeStruct((m, d), BF16)],
        compiler_params=_cparams(("parallel",), 56),
        name="outproj",
    )(*acts, *([w] * n_in), x, g_post.reshape(1, d), g_next.reshape(1, d))


def _xattn_kernel(h_ref, wq_ref, mk_ref, mv_ref, wo_ref, x_ref, gp_ref, gn_ref, xo_ref, ho_ref):
    mk, mv = mk_ref[0], mv_ref[0]
    t = mk.shape[0]
    chains = _row_chains(x_ref.shape[0])
    heads = [slice(hd * X_DH, (hd + 1) * X_DH) for hd in range(X_HEADS)]
    q = [(_dot(h_ref[rows, :], wq_ref[...]) * X_Q_PRESCALE).astype(BF16) for rows in chains]
    s = [[_dot_nt(qc[:, sl], mk[:, sl]) for sl in heads] for qc in q]
    p, inv_l = [], []
    for sc in s:
        pc, lc = [], []
        for sh in sc:
            e = jnp.exp2(sh - jnp.max(sh, axis=-1, keepdims=True))
            part = e[:, :LANES]
            for c in range(1, t // LANES):
                part = part + e[:, c * LANES:(c + 1) * LANES]
            pc.append(e.astype(BF16))
            lc.append(1.0 / jnp.sum(part, axis=-1, keepdims=True))
        p.append(pc)
        inv_l.append(lc)
    o = [jnp.concatenate([(_dot(pc[hd], mv[:, heads[hd]]) * lc[hd]).astype(BF16) for hd in range(X_HEADS)], axis=-1)
         for pc, lc in zip(p, inv_l)]
    acc = [_dot(oc, wo_ref[...]) for oc in o]
    for rows, ac in zip(chains, acc):
        _residual_out(ac, x_ref, gp_ref, gn_ref, xo_ref, ho_ref, rows)


def xattn_block(h, x, mem_k, mem_v, wq, wo, g_post, g_next, seq):
    m, d = x.shape
    tm = _pick(seq, (512, 256, 128, 64))
    per_b = seq // tm
    t, xw = mem_k.shape[1], mem_k.shape[2]
    full = lambda a: pl.BlockSpec(a.shape, lambda i: (0,) * a.ndim)
    row = lambda w: pl.BlockSpec((tm, w), lambda i: (i, 0))
    mem = pl.BlockSpec((1, t, xw), lambda i: (i // per_b, 0, 0))
    vec = pl.BlockSpec((1, d), lambda i: (0, 0))
    return pl.pallas_call(
        _xattn_kernel,
        grid=(m // tm,),
        in_specs=[row(d), full(wq), mem, mem, full(wo), row(d), vec, vec],
        out_specs=[row(d), row(d)],
        out_shape=[jax.ShapeDtypeStruct((m, d), F32), jax.ShapeDtypeStruct((m, d), BF16)],
        compiler_params=_cparams(("parallel",), 48),
        name="xattn_block",
    )(h, wq, mem_k, mem_v, wo, x, g_post.reshape(1, d), g_next.reshape(1, d))


def _mlp_kernel(h_ref, w1_ref, w2_ref, x_ref, gp_ref, gn_ref, xo_ref, *rest, emit_h):
    ho_ref, acc_scr = (rest[0], rest[1]) if emit_h else (None, rest[0])
    f = pl.program_id(1)
    @pl.when(f == 0)
    def _():
        acc_scr[...] = jnp.zeros(acc_scr.shape, F32)

    a = jnp.maximum(_dot(h_ref[...], w1_ref[...]), 0.0)
    acc_scr[...] += _dot((a * a).astype(BF16), w2_ref[...])

    @pl.when(f == pl.num_programs(1) - 1)
    def _():
        _residual_out(acc_scr[...], x_ref, gp_ref, gn_ref, xo_ref, ho_ref)


def mlp_block(h, x, w1, w2, layer, g_post, g_next):
    m, d = x.shape
    ff = w1.shape[2]
    tm = _pick(m, (512, 256, 128, 64))
    tf = _pick(ff, (1024, 512, 256, 128))
    emit_h = g_next is not None
    row = pl.BlockSpec((tm, d), lambda i, f: (i, 0))
    vec = pl.BlockSpec((1, d), lambda i, f: (0, 0))
    out_specs = [row] + ([row] if emit_h else [])
    out_shape = [jax.ShapeDtypeStruct((m, d), F32)] + ([jax.ShapeDtypeStruct((m, d), BF16)] if emit_h else [])
    g_n = (g_next if emit_h else g_post).reshape(1, d)
    res = pl.pallas_call(
        functools.partial(_mlp_kernel, emit_h=emit_h),
        grid=(m // tm, ff // tf),
        in_specs=[row, pl.BlockSpec((None, d, tf), lambda i, f: (layer, 0, f)),
                  pl.BlockSpec((None, tf, d), lambda i, f: (layer, f, 0)), row, vec, vec],
        out_specs=out_specs,
        out_shape=out_shape,
        scratch_shapes=[pltpu.VMEM((tm, d), F32)],
        compiler_params=_cparams(("parallel", "arbitrary"), 56),
        name="mlp_block",
    )(h, w1, w2, x, g_post.reshape(1, d), g_n)
    return (res[0], res[1]) if emit_h else (res[0], None)


def _rope_tables(pos):
    def angles(half):
        inv = jnp.power(ROPE_BASE, -jnp.arange(half, dtype=F32) / half)
        return pos.astype(F32)[:, None] * inv[None, :]
    a = angles(RET_DK // 2)
    c, s = jnp.cos(a), jnp.sin(a)
    t128 = (jnp.concatenate([c, c], axis=1), jnp.concatenate([-s, s], axis=1))
    a = angles(MLA_ROPE // 2)
    c, s = jnp.cos(a), jnp.sin(a)
    z = jnp.zeros((pos.shape[0], LANES - MLA_ROPE), F32)
    t64 = (jnp.concatenate([c, c, z], axis=1), jnp.concatenate([s, s, z], axis=1))
    return t128, t64


def _prep_weights(w_in_even, w_uq, w_ukv, w_out_even, w_in_odd, b_gates_odd, w_out_odd, w_xq, w_xo, w_mlp1, w_mlp2):
    d = w_in_even.shape[1]
    ret_w = RET_HEADS * RET_DK
    n_qkvg = 2 * ret_w + 2 * RET_HEADS * RET_DV
    n_lat = MLA_Q_LORA + MLA_KV_LORA + MLA_ROPE
    even = []
    for e in range(w_in_even.shape[0]):
        w = w_in_even[e].astype(BF16)
        wlat = jnp.pad(w[:, n_qkvg:n_qkvg + n_lat], ((0, 0), (0, LANES - MLA_ROPE)))
        uq = w_uq[e].reshape(MLA_Q_LORA, MLA_HEADS, MLA_NOPE + MLA_ROPE)
        uq_nope = uq[:, :, :MLA_NOPE].reshape(MLA_Q_LORA, MLA_HEADS * MLA_NOPE)
        uq_rope = jnp.pad(uq[:, :, MLA_NOPE:], ((0, 0), (0, 0), (0, LANES - MLA_ROPE)))
        uq_rope = uq_rope.reshape(MLA_Q_LORA, MLA_HEADS * LANES)
        ukv = w_ukv[e].reshape(MLA_KV_LORA, MLA_HEADS, MLA_NOPE + MLA_V)
        even.append(dict(
            w_in=w,
            w_lat=wlat,
            w_uq=jnp.concatenate([uq_nope, uq_rope], axis=1).astype(BF16),
            w_uk=ukv[:, :, :MLA_NOPE].reshape(MLA_KV_LORA, MLA_HEADS * MLA_NOPE).astype(BF16),
            w_uv=ukv[:, :, MLA_NOPE:].reshape(MLA_KV_LORA, MLA_HEADS * MLA_V).astype(BF16),
            w_out=w_out_even[e].astype(BF16),
        ))
    m_qk = M_HEADS * M_DK
    n_qkvo = 2 * m_qk + 2 * M_HEADS * M_DV
    odd = []
    for o in range(w_in_odd.shape[0]):
        w = w_in_odd[o].astype(BF16)
        odd.append(dict(
            w_in=w,
            w_gates=jnp.pad(w[:, n_qkvo:], ((0, 0), (0, LANES - 2 * M_HEADS))),
            b_gates=jnp.pad(b_gates_odd[o], (0, LANES - 2 * M_HEADS)),
            w_out=w_out_odd[o].astype(BF16),
        ))
    q_scale = jnp.concatenate([jnp.full((m_qk,), M_DK ** -0.5, F32), jnp.ones((n_qkvo - m_qk,), F32)])
    rk_scale = jnp.concatenate([jnp.ones((ret_w,), F32), jnp.full((ret_w,), RET_DK ** -0.5, F32)])
    shared = dict(w_xq=w_xq.astype(BF16), w_xo=w_xo.astype(BF16), w_mlp1=w_mlp1.astype(BF16),
                  w_mlp2=w_mlp2.astype(BF16), q_scale=q_scale, rk_scale=rk_scale, d=d,
                  n_qk=2 * ret_w, n_vg=n_qkvg - 2 * ret_w, n_qkvo=n_qkvo)
    return even, odd, shared


def _run_trunk(x, pos, mem_k, mem_v, ret_s0, c0, n0, m0, past, norm_g, gains, even_w, odd_w, shared):
    b, s, d = x.shape
    m = b * s
    depth = norm_g.shape[0]
    t128, t64 = _rope_tables(pos)
    if m % s or s % 64:
        raise ValueError("unsupported sequence length")
    if s < 256:
        t128 = tuple(jnp.tile(t, (b, 1)) for t in t128)
        t64 = tuple(jnp.tile(t, (b, 1)) for t in t64)
    xf = x.reshape(m, d)
    h = rms_rows(xf, norm_g[0, 0], BF16)
    n_even = len(even_w)
    ckv_slab = None
    krs, rets, cs, ns, ms = [], [], [], [], []
    for layer in range(depth):
        g = norm_g[layer]
        if layer % 2 == 0:
            e = layer // 2
            w = even_w[e]
            qk = matmul(h, w["w_in"], BF16, n_cols=shared["n_qk"], col_scale=shared["rk_scale"], rope=t128)
            vg = matmul(h, w["w_in"], BF16, col0=shared["n_qk"], n_cols=shared["n_vg"])
            ret, ret_s = retention(qk.reshape(b, s, -1), vg.reshape(b, s, -1), ret_s0[e], gains["ret_gn_g"][e])
            qcat, kcat, v, ckv_slab, kr = even_latent(h, w["w_lat"], gains["mla_q_norm_g"][e].reshape(1, -1),
                                                      gains["mla_kv_norm_g"][e].reshape(1, -1),
                                                      w["w_uq"], w["w_uk"], w["w_uv"], t64, e, n_even, ckv_slab)
            kcat = kcat.reshape(b, s, -1)
            v = v.reshape(b, s, -1)
            if past is None:
                q_off, sk = 0, s
            else:
                p_ckv, p_krp = past
                p_len = p_ckv.shape[2]
                q_off, sk = p_len, p_len + s
                pad = (-sk) % 256
                pk, pv = past_kv(p_ckv, p_krp, e, w["w_uk"], w["w_uv"], sk + pad)
                tail = lambda new: jnp.concatenate([new, jnp.zeros((b, pad, new.shape[-1]), BF16)], axis=1)
                kcat = lax.dynamic_update_slice(pk, tail(kcat), (0, p_len, 0))
                v = lax.dynamic_update_slice(pv, tail(v), (0, p_len, 0))
            att = mla_attention(qcat.reshape(b, s, -1), kcat, v, q_off=q_off, sk_valid=sk)
            xf, h = outproj([ret.reshape(m, -1), att.reshape(m, -1)], w["w_out"], xf, g[1], g[2])
            rets.append(ret_s)
            krs.append(kr.reshape(b, s, -1))
        else:
            o = layer // 2
            w = odd_w[o]
            qkvo = matmul(h, w["w_in"], BF16, n_cols=shared["n_qkvo"], col_scale=shared["q_scale"])
            gates = matmul(h, w["w_gates"], F32, col_bias=w["b_gates"])
            hh, c, n, mm = mlstm(qkvo.reshape(b, s, -1), gates.reshape(b, s, -1), c0[o], n0[o], m0[o],
                                 gains["mlstm_norm_g"][o])
            xf, h = outproj([hh.reshape(m, -1)], w["w_out"], xf, g[1], g[2])
            cs.append(c)
            ns.append(n)
            ms.append(mm)
        xf, h = xattn_block(h, xf, mem_k[layer], mem_v[layer], shared["w_xq"][layer], shared["w_xo"][layer],
                            g[3], g[4], s)
        g_next = norm_g[layer + 1, 0] if layer + 1 < depth else None
        xf, h = mlp_block(h, xf, shared["w_mlp1"], shared["w_mlp2"], layer, g[5], g_next)
    return (xf.reshape(b, s, d), ckv_slab.reshape(n_even, b, s, -1), jnp.stack(krs), jnp.stack(rets),
            jnp.stack(cs), jnp.stack(ns), jnp.stack(ms))


def kernel(x_prompt, x_sample, cache_mla_ckv, cache_mla_krope, state_ret, state_mlstm_C, state_mlstm_n, state_mlstm_m, cache_mem_k, cache_mem_v, mem_prompt, norm_g, mem_norm_g, w_in_even, mla_q_norm_g, mla_kv_norm_g, w_uq, w_ukv, ret_gn_g, w_out_even, w_in_odd, b_gates_odd, mlstm_norm_g, w_out_odd, w_xq, w_xk, w_xv, w_xo, w_mlp1, w_mlp2):
    even_w, odd_w, shared = _prep_weights(w_in_even, w_uq, w_ukv, w_out_even, w_in_odd, b_gates_odd, w_out_odd,
                                          w_xq, w_xo, w_mlp1, w_mlp2)
    gains = dict(ret_gn_g=ret_gn_g, mla_q_norm_g=mla_q_norm_g, mla_kv_norm_g=mla_kv_norm_g,
                 mlstm_norm_g=mlstm_norm_g)
    depth = norm_g.shape[0]
    b, s, d = x_prompt.shape
    n_even, n_odd = w_in_even.shape[0], w_in_odd.shape[0]
    xw = X_HEADS * X_DH

    bm, t, _ = mem_prompt.shape
    mem_flat = mem_prompt.reshape(bm * t, d)
    p_mem_k, p_mem_v = [], []
    for layer in range(depth):
        mn = rms_rows(mem_flat, mem_norm_g[layer], BF16)
        kv = matmul(mn, jnp.concatenate([w_xk[layer], w_xv[layer]], axis=1).astype(BF16), F32)
        p_mem_k.append(kv[:, :xw].reshape(bm, t, X_HEADS, X_DH))
        p_mem_v.append(kv[:, xw:].reshape(bm, t, X_HEADS, X_DH))
    p_mem_k, p_mem_v = jnp.stack(p_mem_k), jnp.stack(p_mem_v)

    pos_p = jnp.arange(s, dtype=jnp.int32)
    zeros = lambda *shape: jnp.zeros(shape, F32)
    y_prompt, p_ckv, p_kr, p_ret, p_c, p_n, p_m = _run_trunk(
        x_prompt, pos_p, p_mem_k.reshape(depth, bm, t, xw).astype(BF16), p_mem_v.reshape(depth, bm, t, xw).astype(BF16),
        zeros(n_even, b, RET_HEADS, RET_DK, RET_DV), zeros(n_odd, b, M_HEADS, M_DV, M_DK),
        zeros(n_odd, b, M_HEADS, M_DK), zeros(n_odd, b, M_HEADS), None, norm_g, gains, even_w, odd_w, shared)

    db, ds, _ = x_sample.shape
    past_len = cache_mla_ckv.shape[2]
    pos_s = past_len + jnp.arange(ds, dtype=jnp.int32)
    past = (cache_mla_ckv, jnp.pad(cache_mla_krope, ((0, 0), (0, 0), (0, 0), (0, LANES - MLA_ROPE))))
    y_sample, s_ckv, s_kr, s_ret, s_c, s_n, s_m = _run_trunk(
        x_sample, pos_s, cache_mem_k.reshape(depth, db, -1, xw).astype(BF16),
        cache_mem_v.reshape(depth, db, -1, xw).astype(BF16),
        state_ret, state_mlstm_C, state_mlstm_n, state_mlstm_m, past, norm_g, gains, even_w, odd_w, shared)

    return (y_prompt, y_sample, p_ckv, p_kr, p_ret, p_c, p_n, p_m, p_mem_k, p_mem_v,
            s_ckv, s_kr, s_ret, s_c, s_n, s_m)
```

```python
import functools

import jax
import jax.numpy as jnp
from jax import lax
from jax.experimental import pallas as pl
from jax.experimental.pallas import tpu as pltpu

F32 = jnp.float32
BF16 = jnp.bfloat16

RMS_EPS = 1e-6
ROPE_BASE = 10000.0
CHUNK = 64
CHUNK_SHIFT = 6
assert 1 << CHUNK_SHIFT == CHUNK

RET_HEADS, RET_DK, RET_DV = 8, 128, 128
MLA_HEADS, MLA_NOPE, MLA_ROPE, MLA_V = 8, 128, 64, 128
MLA_Q_LORA, MLA_KV_LORA = 768, 512
MLA_QK_PAD = 256
KV_UNROLL = 2
MLA_HEADS_PER_STEP = 2
MLA_Q_PRESCALE = (MLA_NOPE + MLA_ROPE) ** -0.5 * 1.4426950408889634
M_HEADS, M_DK, M_DV = 8, 128, 256
X_HEADS, X_DH = 4, 128
X_Q_PRESCALE = X_DH ** -0.5 * 1.4426950408889634
LANES = 128
ROW_CHAINS = 2

V7X_VMEM_BYTES = 64 * 1024 * 1024
MIB = 1024 * 1024


def _cparams(semantics, vmem_mib):
    assert vmem_mib * MIB < V7X_VMEM_BYTES
    return pltpu.CompilerParams(dimension_semantics=semantics, vmem_limit_bytes=vmem_mib * MIB)


def _pick(n, cands):
    for c in cands:
        if c <= n and n % c == 0:
            return c
    raise ValueError(f"no tile for {n} in {cands}")


def _rms(x, g):
    ms = jnp.mean(x * x, axis=-1, keepdims=True)
    return x * lax.rsqrt(ms + RMS_EPS) * g


def _sigmoid(x):
    return 1.0 / (1.0 + jnp.exp(-x))


def _dot(a, b):
    return jnp.dot(a, b, preferred_element_type=F32)


def _dot_nt(a, b):
    return lax.dot_general(a, b, (((1,), (1,)), ((), ())), preferred_element_type=F32)


def _dot_tn(a, b):
    return lax.dot_general(a, b, (((0,), (0,)), ((), ())), preferred_element_type=F32)


def _rope128(x, cos2, sin2):
    return x * cos2 + pltpu.roll(x, 64, 1) * sin2


def _rope64(x, cos_p, sin_p):
    return x * cos_p + (pltpu.roll(x, 32, 1) - pltpu.roll(x, 96, 1)) * sin_p


def _norm_kernel(x_ref, g_ref, o_ref):
    o_ref[...] = _rms(x_ref[...].astype(F32), g_ref[...]).astype(o_ref.dtype)


def rms_rows(x, g, out_dtype):
    m, k = x.shape
    tm = _pick(m, (512, 256, 128, 64, 32, 16, 8))
    return pl.pallas_call(
        _norm_kernel,
        grid=(m // tm,),
        in_specs=[pl.BlockSpec((tm, k), lambda i: (i, 0)), pl.BlockSpec((1, k), lambda i: (0, 0))],
        out_specs=pl.BlockSpec((tm, k), lambda i: (i, 0)),
        out_shape=jax.ShapeDtypeStruct((m, k), out_dtype),
        compiler_params=_cparams(("parallel",), 32),
        name="rms_rows",
    )(x, g.reshape(1, k).astype(F32))


def _mm_kernel(*refs, has_scale, has_bias, n_rope, tn):
    it = iter(refs)
    h_ref, w_ref = next(it), next(it)
    scale_ref = next(it) if has_scale else None
    bias_ref = next(it) if has_bias else None
    cos_ref, sin_ref = (next(it), next(it)) if n_rope else (None, None)
    o_ref = next(it)
    acc = _dot(h_ref[...], w_ref[...])
    if has_scale:
        acc = acc * scale_ref[...]
    if has_bias:
        acc = acc + bias_ref[...]
    if n_rope:
        c, s = cos_ref[...], sin_ref[...]
        for t in range(tn // LANES):
            sl = slice(t * LANES, (t + 1) * LANES)
            o_ref[:, sl] = _rope128(acc[:, sl], c, s).astype(o_ref.dtype)
    else:
        o_ref[...] = acc.astype(o_ref.dtype)


def matmul(h, w, out_dtype, *, col0=0, n_cols=None, col_scale=None, col_bias=None, rope=None):
    m, k = h.shape
    n = w.shape[1] - col0 if n_cols is None else n_cols
    tm = _pick(m, (1024, 512, 256, 128, 64))
    tn = _pick(n, (512, 256, 128))
    assert col0 % tn == 0 and col0 + n <= w.shape[1]
    jb = col0 // tn
    args = [h, w]
    in_specs = [pl.BlockSpec((tm, k), lambda i, j: (i, 0)), pl.BlockSpec((k, tn), lambda i, j: (0, j + jb))]
    for vec in (col_scale, col_bias):
        if vec is not None:
            args.append(vec.reshape(1, n).astype(F32))
            in_specs.append(pl.BlockSpec((1, tn), lambda i, j: (0, j)))
    if rope is not None:
        r = rope[0].shape[0]
        tm = _pick(m, tuple(c for c in (1024, 512, 256, 128, 64) if r % c == 0))
        in_specs[0] = pl.BlockSpec((tm, k), lambda i, j: (i, 0))
        nrb = r // tm
        for tab in rope:
            args.append(tab)
            in_specs.append(pl.BlockSpec((tm, LANES), lambda i, j: (i % nrb, 0)))
    kern = functools.partial(_mm_kernel, has_scale=col_scale is not None, has_bias=col_bias is not None,
                             n_rope=rope is not None, tn=tn)
    return pl.pallas_call(
        kern,
        grid=(m // tm, n // tn),
        in_specs=in_specs,
        out_specs=pl.BlockSpec((tm, tn), lambda i, j: (i, j)),
        out_shape=jax.ShapeDtypeStruct((m, n), out_dtype),
        compiler_params=_cparams(("parallel", "arbitrary"), 40),
        name="matmul",
    )(*args)


def _kv_expand(ckvn_bf16, krr_bf16, wuk_ref, wuv_ref, kcat_ref, v_ref):
    kn = _dot(ckvn_bf16, wuk_ref[...])
    for hd in range(MLA_HEADS):
        kcat_ref[:, hd * MLA_QK_PAD: hd * MLA_QK_PAD + MLA_NOPE] = (
            kn[:, hd * MLA_NOPE:(hd + 1) * MLA_NOPE].astype(BF16))
        kcat_ref[:, hd * MLA_QK_PAD + MLA_NOPE:(hd + 1) * MLA_QK_PAD] = krr_bf16
    v_ref[...] = _dot(ckvn_bf16, wuv_ref[...]).astype(BF16)


def _even_latent_kernel(h_ref, wlat_ref, gq_ref, gkv_ref, wuq_ref, wuk_ref, wuv_ref, cos_ref, sin_ref, *refs):
    qcat_ref, kcat_ref, v_ref, ckv_ref, kr_ref = refs[-5:]
    lat = _dot(h_ref[...], wlat_ref[...])
    cq = lat[:, :MLA_Q_LORA]
    ckv = lat[:, MLA_Q_LORA:MLA_Q_LORA + MLA_KV_LORA]
    krp = lat[:, MLA_Q_LORA + MLA_KV_LORA:]
    c, s = cos_ref[...], sin_ref[...]
    q = _dot(_rms(cq, gq_ref[...]).astype(BF16), wuq_ref[...]) * MLA_Q_PRESCALE
    nope_w = MLA_HEADS * MLA_NOPE
    for hd in range(MLA_HEADS):
        qcat_ref[:, hd * MLA_QK_PAD: hd * MLA_QK_PAD + MLA_NOPE] = (
            q[:, hd * MLA_NOPE:(hd + 1) * MLA_NOPE].astype(BF16))
        qr = q[:, nope_w + hd * LANES: nope_w + (hd + 1) * LANES]
        qcat_ref[:, hd * MLA_QK_PAD + MLA_NOPE:(hd + 1) * MLA_QK_PAD] = _rope64(qr, c, s).astype(BF16)
    ckvn = _rms(ckv, gkv_ref[...])
    ckv_ref[...] = ckvn
    krr = _rope64(krp, c, s)
    kr_ref[...] = krr[:, :MLA_ROPE]
    _kv_expand(ckvn.astype(BF16), krr.astype(BF16), wuk_ref, wuv_ref, kcat_ref, v_ref)


def even_latent(h, wlat, gq, gkv, wuq, wuk, wuv, rope64, e, n_even, ckv_slab):
    m, d = h.shape
    r = rope64[0].shape[0]
    tm = _pick(m, tuple(c for c in (256, 128, 64) if r % c == 0))
    nrb = r // tm
    full = lambda a: pl.BlockSpec(a.shape, lambda i: (0,) * a.ndim)
    row = lambda w: pl.BlockSpec((tm, w), lambda i: (i, 0))
    tab = pl.BlockSpec((tm, LANES), lambda i: (i % nrb, 0))
    qk_w = MLA_HEADS * MLA_QK_PAD
    v_w = MLA_HEADS * MLA_V
    args = [h, wlat, gq, gkv, wuq, wuk, wuv, rope64[0], rope64[1]]
    in_specs = [row(d), full(wlat), full(gq), full(gkv), full(wuq), full(wuk), full(wuv), tab, tab]
    aliases = {}
    if ckv_slab is not None:
        aliases = {len(args): 3}
        args.append(ckv_slab)
        in_specs.append(pl.BlockSpec(memory_space=pl.ANY))
    return pl.pallas_call(
        _even_latent_kernel,
        grid=(m // tm,),
        in_specs=in_specs,
        out_specs=[row(qk_w), row(qk_w), row(v_w),
                   pl.BlockSpec((None, tm, MLA_KV_LORA), lambda i: (e, i, 0)), row(MLA_ROPE)],
        out_shape=[jax.ShapeDtypeStruct((m, qk_w), BF16), jax.ShapeDtypeStruct((m, qk_w), BF16),
                   jax.ShapeDtypeStruct((m, v_w), BF16), jax.ShapeDtypeStruct((n_even, m, MLA_KV_LORA), F32),
                   jax.ShapeDtypeStruct((m, MLA_ROPE), F32)],
        input_output_aliases=aliases,
        compiler_params=_cparams(("parallel",), 48),
        name="even_latent",
    )(*args)


def _past_kv_kernel(ckv_ref, krp_ref, wuk_ref, wuv_ref, kcat_ref, v_ref):
    _kv_expand(ckv_ref[...].astype(BF16), krp_ref[...].astype(BF16), wuk_ref, wuv_ref, kcat_ref, v_ref)


def past_kv(ckvn, kr_pad, e, wuk, wuv, skp):
    _, b, p_len, _ = ckvn.shape
    tm = _pick(p_len, (512, 256, 128, 64))
    full = lambda a: pl.BlockSpec(a.shape, lambda bi, i: (0,) * a.ndim)
    src = lambda w: pl.BlockSpec((None, None, tm, w), lambda bi, i: (e, bi, i, 0))
    dst = lambda w: pl.BlockSpec((None, tm, w), lambda bi, i: (bi, i, 0))
    qk_w = MLA_HEADS * MLA_QK_PAD
    v_w = MLA_HEADS * MLA_V
    return pl.pallas_call(
        _past_kv_kernel,
        grid=(b, p_len // tm),
        in_specs=[src(MLA_KV_LORA), src(LANES), full(wuk), full(wuv)],
        out_specs=[dst(qk_w), dst(v_w)],
        out_shape=[jax.ShapeDtypeStruct((b, skp, qk_w), BF16), jax.ShapeDtypeStruct((b, skp, v_w), BF16)],
        compiler_params=_cparams(("parallel", "parallel"), 32),
        name="past_kv",
    )(ckvn, kr_pad, wuk, wuv)


def _mla_kernel(q_ref, k_ref, v_ref, o_ref, m_scr, l_scr, acc_scr, a_scr, p_scr, *, tq, tk, hb, q_off, sk_valid):
    qi = pl.program_id(2)
    q0 = qi * tq
    cq_lo = (q_off + q0) // CHUNK
    cq_hi = (q_off + q0 + tq - 1) // CHUNK
    n_full = jnp.minimum(lax.div((cq_lo + 1) * CHUNK, tk), sk_valid // tk)
    n_vis = lax.div(jnp.minimum((cq_hi + 1) * CHUNK, sk_valid) + tk - 1, tk)
    m_scr[...] = jnp.full(m_scr.shape, -1e30, F32)
    l_scr[...] = jnp.zeros(l_scr.shape, F32)
    acc_scr[...] = jnp.zeros(acc_scr.shape, F32)
    a_scr[...] = jnp.ones(a_scr.shape, F32)
    p_scr[...] = jnp.zeros(p_scr.shape, BF16)

    def apply_prev(j, ki_prev):
        v = v_ref[0, pl.ds(pl.multiple_of(ki_prev * tk, tk), tk), j * MLA_V:(j + 1) * MLA_V]
        acc_scr[j] = a_scr[j] * acc_scr[j] + _dot(p_scr[j], v)

    def step(ki, masked):
        k0 = pl.multiple_of(ki * tk, tk)
        ki_prev = jnp.maximum(ki - 1, 0)
        if masked:
            qpos = q_off + q0 + lax.broadcasted_iota(jnp.int32, (tq, tk), 0)
            kpos = k0 + lax.broadcasted_iota(jnp.int32, (tq, tk), 1)
            vis = jnp.logical_and((kpos >> CHUNK_SHIFT) <= (qpos >> CHUNK_SHIFT), kpos < sk_valid)
        for j in range(hb):
            q = q_ref[0, :, j * MLA_QK_PAD:(j + 1) * MLA_QK_PAD]
            k = k_ref[0, pl.ds(k0, tk), j * MLA_QK_PAD:(j + 1) * MLA_QK_PAD]
            s = _dot_nt(q, k)
            apply_prev(j, ki_prev)
            if masked:
                s = jnp.where(vis, s, -jnp.inf)
            m_prev = m_scr[j]
            m_new = jnp.maximum(m_prev, jnp.max(s, axis=-1, keepdims=True))
            alpha = jnp.exp2(m_prev - m_new)
            p = [jnp.exp2(s[:, c * LANES:(c + 1) * LANES] - m_new) for c in range(tk // LANES)]
            l_scr[j] = alpha * l_scr[j] + sum(p[1:], p[0])
            p_scr[j] = jnp.concatenate([pc.astype(BF16) for pc in p], axis=-1)
            a_scr[j] = alpha
            m_scr[j] = m_new

    def body_full(ki, carry):
        step(ki, False)
        return carry

    def body_masked(ki, carry):
        step(ki, True)
        return carry

    def body_full_group(kg, carry):
        for u in range(KV_UNROLL):
            step(KV_UNROLL * kg + u, False)
        return carry

    n_groups = lax.div(n_full, KV_UNROLL)
    lax.fori_loop(0, n_groups, body_full_group, 0)
    lax.fori_loop(KV_UNROLL * n_groups, n_full, body_full, 0)
    lax.fori_loop(n_full, n_vis, body_masked, 0)
    for j in range(hb):
        apply_prev(j, n_vis - 1)
        l = jnp.sum(l_scr[j], axis=-1, keepdims=True)
        o_ref[0, :, j * MLA_V:(j + 1) * MLA_V] = (acc_scr[j] / l).astype(o_ref.dtype)


def mla_attention(qcat, kcat, v, *, q_off, sk_valid):
    b, sq, _ = qcat.shape
    skp = kcat.shape[1]
    tq = _pick(sq, (512, 256, 128, 64))
    tk = skp if tq * skp <= 512 * 512 else _pick(skp, (512, 256, 128))
    hb = MLA_HEADS if tk == skp else MLA_HEADS_PER_STEP
    kern = functools.partial(_mla_kernel, tq=tq, tk=tk, hb=hb, q_off=q_off, sk_valid=sk_valid)
    return pl.pallas_call(
        kern,
        grid=(b, MLA_HEADS // hb, sq // tq),
        in_specs=[pl.BlockSpec((1, tq, hb * MLA_QK_PAD), lambda bi, h, qi: (bi, qi, h)),
                  pl.BlockSpec((1, skp, hb * MLA_QK_PAD), lambda bi, h, qi: (bi, 0, h)),
                  pl.BlockSpec((1, skp, hb * MLA_V), lambda bi, h, qi: (bi, 0, h))],
        out_specs=pl.BlockSpec((1, tq, hb * MLA_V), lambda bi, h, qi: (bi, qi, h)),
        out_shape=jax.ShapeDtypeStruct((b, sq, MLA_HEADS * MLA_V), BF16),
        scratch_shapes=[pltpu.VMEM((hb, tq, LANES), F32), pltpu.VMEM((hb, tq, LANES), F32),
                        pltpu.VMEM((hb, tq, MLA_V), F32), pltpu.VMEM((hb, tq, LANES), F32),
                        pltpu.VMEM((hb, tq, tk), BF16)],
        compiler_params=_cparams(("parallel", "parallel", "arbitrary"), 48),
        name="mla_attention",
    )(qcat, kcat, v)


def _retention_kernel(lg_ref, qk_ref, vg_ref, s0_ref, gn_ref, out_ref, sfin_ref,
                      state_scr, decay_scr, qdec_scr, kdec_scr, *, lc):
    ci = pl.program_id(1)
    heads = range(RET_HEADS)

    @pl.when(ci == 0)
    def _():
        state_scr[...] = s0_ref[0]
        ti = lax.broadcasted_iota(jnp.int32, (lc, lc), 0)
        si = lax.broadcasted_iota(jnp.int32, (lc, lc), 1)
        dpos = jnp.maximum(ti - si, 0).astype(F32)
        trow = lax.broadcasted_iota(jnp.int32, (lc, LANES), 0).astype(F32)
        for h in heads:
            lg = lg_ref[h:h + 1, :]
            decay_scr[h] = jnp.where(ti >= si, jnp.exp(lg[:, 0:1] * dpos), 0.0)
            qdec_scr[h] = jnp.exp(lg * (trow + 1.0))
            kdec_scr[h] = jnp.exp(lg * (lc - 1.0 - trow))

    qk_w = RET_HEADS * RET_DK
    v_w = RET_HEADS * RET_DV
    q = [qk_ref[0, :, h * RET_DK:(h + 1) * RET_DK] for h in heads]
    k = [qk_ref[0, :, qk_w + h * RET_DK: qk_w + (h + 1) * RET_DK] for h in heads]
    v = [vg_ref[0, :, h * RET_DV:(h + 1) * RET_DV] for h in heads]
    st = [state_scr[h] for h in heads]
    sc = [_dot_nt(q[h], k[h]) for h in heads]
    inter = [_dot(q[h], st[h].astype(BF16)) for h in heads]
    scd = [(sc[h] * decay_scr[h]).astype(BF16) for h in heads]
    kd = [(k[h].astype(F32) * kdec_scr[h]).astype(BF16) for h in heads]
    intra = [_dot(scd[h], v[h]) for h in heads]
    upd = [_dot_tn(kd[h], v[h]) for h in heads]
    for h in heads:
        state_scr[h] = jnp.exp(lg_ref[h:h + 1, :] * lc) * st[h] + upd[h]
        out = intra[h] + qdec_scr[h] * inter[h]
        g = vg_ref[0, :, v_w + h * RET_DV: v_w + (h + 1) * RET_DV].astype(F32)
        xc = out - jnp.mean(out, axis=-1, keepdims=True)
        y = xc * lax.rsqrt(jnp.mean(xc * xc, axis=-1, keepdims=True) + RMS_EPS)
        y = y * gn_ref[:, h * RET_DV:(h + 1) * RET_DV]
        out_ref[0, :, h * RET_DV:(h + 1) * RET_DV] = (g * _sigmoid(g) * y).astype(out_ref.dtype)

    @pl.when(ci == pl.num_programs(1) - 1)
    def _():
        sfin_ref[0] = state_scr[...]


def retention(qk, vg, s0, gn_g):
    b, s, _ = qk.shape
    lc = _pick(s, (256, 128, 64))
    log_g = jnp.log1p(-jnp.exp2(-5.0 - jnp.arange(RET_HEADS, dtype=F32)))
    lg_tab = jnp.broadcast_to(log_g[:, None], (RET_HEADS, LANES))
    v_w = RET_HEADS * RET_DV
    kern = functools.partial(_retention_kernel, lc=lc)
    st_spec = pl.BlockSpec((1, RET_HEADS, RET_DK, RET_DV), lambda bi, ci: (bi, 0, 0, 0))
    return pl.pallas_call(
        kern,
        grid=(b, s // lc),
        in_specs=[pl.BlockSpec((RET_HEADS, LANES), lambda bi, ci: (0, 0)),
                  pl.BlockSpec((1, lc, qk.shape[2]), lambda bi, ci: (bi, ci, 0)),
                  pl.BlockSpec((1, lc, vg.shape[2]), lambda bi, ci: (bi, ci, 0)),
                  st_spec,
                  pl.BlockSpec((1, v_w), lambda bi, ci: (0, 0))],
        out_specs=[pl.BlockSpec((1, lc, v_w), lambda bi, ci: (bi, ci, 0)), st_spec],
        out_shape=[jax.ShapeDtypeStruct((b, s, v_w), BF16),
                   jax.ShapeDtypeStruct((b, RET_HEADS, RET_DK, RET_DV), F32)],
        scratch_shapes=[pltpu.VMEM((RET_HEADS, RET_DK, RET_DV), F32), pltpu.VMEM((RET_HEADS, lc, lc), F32),
                        pltpu.VMEM((RET_HEADS, lc, LANES), F32), pltpu.VMEM((RET_HEADS, lc, LANES), F32)],
        compiler_params=_cparams(("parallel", "arbitrary"), 40),
        name="retention",
    )(lg_tab, qk, vg, s0, gn_g.reshape(1, v_w).astype(F32))


def _split3(x):
    hi = x.astype(BF16)
    r1 = x - hi.astype(F32)
    mid = r1.astype(BF16)
    lo = (r1 - mid.astype(F32)).astype(BF16)
    return hi, mid, lo


def _log_sigmoid(x):
    return jnp.minimum(x, 0.0) - jnp.log(1.0 + jnp.exp(-jnp.abs(x)))


def _widen(x, width):
    if width < LANES:
        return x[:, :width]
    return x if width == LANES else jnp.concatenate([x] * (width // LANES), axis=1)


def _mlstm_kernel(qkvo_ref, gates_ref, gates_t_ref, c0_ref, n0_ref, m0_ref, ng_ref,
                  out_ref, cfin_ref, nfin_ref, mfin_ref, c_scr, n_scr, m_scr, *, lc):
    ci = pl.program_id(1)
    heads = range(M_HEADS)

    @pl.when(ci == 0)
    def _():
        c_scr[...] = c0_ref[0]
        n_scr[...] = n0_ref[0]
        m_scr[...] = m0_ref[0]

    gates = gates_ref[0]
    gates_t = gates_t_ref[0]
    ti = lax.broadcasted_iota(jnp.int32, (lc, lc), 0)
    si = lax.broadcasted_iota(jnp.int32, (lc, lc), 1)
    tril = si <= ti
    ones_l = jnp.where(tril, 1.0, 0.0).astype(BF16)
    ones_u = jnp.where(ti <= si, 1.0, 0.0).astype(BF16)
    b_col = sum(_dot(ones_l, part) for part in _split3(_log_sigmoid(gates)))
    b_row = sum(_dot(part, ones_u) for part in _split3(_log_sigmoid(gates_t)))[M_HEADS:2 * M_HEADS]
    i_row = gates_t[0:M_HEADS]
    qk_w = M_HEADS * M_DK
    v_w = M_HEADS * M_DV
    q = [qkvo_ref[0, :, h * M_DK:(h + 1) * M_DK] for h in heads]
    k = [qkvo_ref[0, :, qk_w + h * M_DK: qk_w + (h + 1) * M_DK] for h in heads]
    v = [qkvo_ref[0, :, 2 * qk_w + h * M_DV: 2 * qk_w + (h + 1) * M_DV] for h in heads]
    c_st = [c_scr[h] for h in heads]
    n_st = [n_scr[h:h + 1, :] for h in heads]
    m_prev = [m_scr[h:h + 1, :] for h in heads]
    qk = [_dot_nt(q[h], k[h]) for h in heads]
    qc = [_dot(q[h], c_st[h].astype(BF16)) for h in heads]
    bt = [jnp.broadcast_to(b_col[:, M_HEADS + h:M_HEADS + h + 1], (lc, LANES)) for h in heads]
    i_t = [jnp.broadcast_to(gates[:, h:h + 1], (lc, LANES)) for h in heads]
    m_t, w_inter, sc = [], [], []
    for h in heads:
        log_intra = jnp.where(tril, _widen(bt[h], lc) - b_row[h:h + 1, :] + i_row[h:h + 1, :], -jnp.inf)
        log_inter = bt[h] + m_prev[h]
        m_t.append(jnp.maximum(log_inter, jnp.max(log_intra, axis=-1, keepdims=True)))
        w_inter.append(jnp.exp(log_inter - m_t[h]))
        sc.append(qk[h] * jnp.exp(log_intra - _widen(m_t[h], lc)))
    num_intra = [_dot(sc[h].astype(BF16), v[h]) for h in heads]
    kw = []
    for h in heads:
        m_new = m_t[h][lc - 1:lc, :]
        b_last = bt[h][lc - 1:lc, :]
        w_state = jnp.exp(b_last + m_prev[h] - m_new)
        kw.append(k[h].astype(F32) * jnp.exp(b_last - bt[h] + i_t[h] - m_new))
        n_scr[h:h + 1, :] = w_state * n_st[h] + jnp.sum(kw[h], axis=0, keepdims=True)
        m_scr[h:h + 1, :] = m_new
        c_scr[h] = _widen(w_state, M_DV) * c_st[h] + _dot_tn(kw[h].astype(BF16), v[h])
    for h in heads:
        num = num_intra[h] + _widen(w_inter[h], M_DV) * qc[h]
        den = (jnp.sum(sc[h], axis=-1, keepdims=True)
               + w_inter[h] * jnp.sum(q[h].astype(F32) * n_st[h], axis=-1, keepdims=True))
        hh = num * _widen(1.0 / jnp.maximum(jnp.abs(den), jnp.exp(-m_t[h])), M_DV)
        y = hh * lax.rsqrt(jnp.mean(hh * hh, axis=-1, keepdims=True) + RMS_EPS)
        y = y * ng_ref[:, h * M_DV:(h + 1) * M_DV]
        og = qkvo_ref[0, :, 2 * qk_w + v_w + h * M_DV: 2 * qk_w + v_w + (h + 1) * M_DV].astype(F32)
        out_ref[0, :, h * M_DV:(h + 1) * M_DV] = (y * _sigmoid(og)).astype(out_ref.dtype)

    @pl.when(ci == pl.num_programs(1) - 1)
    def _():
        cfin_ref[0] = c_scr[...]
        nfin_ref[0] = n_scr[...]
        mfin_ref[0] = m_scr[...]


def mlstm(qkvo, gates, c0, n0, m0, norm_g):
    b, s, w = qkvo.shape
    lc = _pick(s, (256, 128, 64))
    v_w = M_HEADS * M_DV
    gates_t = jnp.swapaxes(gates[:, :, :2 * M_HEADS], 1, 2)
    kern = functools.partial(_mlstm_kernel, lc=lc)
    c_spec = pl.BlockSpec((1, M_HEADS, M_DK, M_DV), lambda bi, ci: (bi, 0, 0, 0))
    n_spec = pl.BlockSpec((1, M_HEADS, M_DK), lambda bi, ci: (bi, 0, 0))
    m_spec = pl.BlockSpec((1, M_HEADS, LANES), lambda bi, ci: (bi, 0, 0))
    hh, c_t, n, m = pl.pallas_call(
        kern,
        grid=(b, s // lc),
        in_specs=[pl.BlockSpec((1, lc, w), lambda bi, ci: (bi, ci, 0)),
                  pl.BlockSpec((1, lc, LANES), lambda bi, ci: (bi, ci, 0)),
                  pl.BlockSpec((1, 2 * M_HEADS, lc), lambda bi, ci: (bi, 0, ci)),
                  c_spec, n_spec, m_spec,
                  pl.BlockSpec((1, v_w), lambda bi, ci: (0, 0))],
        out_specs=[pl.BlockSpec((1, lc, v_w), lambda bi, ci: (bi, ci, 0)), c_spec, n_spec, m_spec],
        out_shape=[jax.ShapeDtypeStruct((b, s, v_w), BF16),
                   jax.ShapeDtypeStruct((b, M_HEADS, M_DK, M_DV), F32),
                   jax.ShapeDtypeStruct((b, M_HEADS, M_DK), F32),
                   jax.ShapeDtypeStruct((b, M_HEADS, LANES), F32)],
        scratch_shapes=[pltpu.VMEM((M_HEADS, M_DK, M_DV), F32), pltpu.VMEM((M_HEADS, M_DK), F32),
                        pltpu.VMEM((M_HEADS, LANES), F32)],
        compiler_params=_cparams(("parallel", "arbitrary"), 48),
        name="mlstm",
    )(qkvo, gates, gates_t, jnp.swapaxes(c0, 2, 3), n0, jnp.broadcast_to(m0[..., None], m0.shape + (LANES,)),
      norm_g.reshape(1, v_w).astype(F32))
    return hh, jnp.swapaxes(c_t, 2, 3), n, m[..., 0]


def _residual_out(acc, x_ref, gp_ref, gn_ref, xo_ref, ho_ref, rows=slice(None)):
    xn = x_ref[rows, :] + _rms(acc, gp_ref[...])
    xo_ref[rows, :] = xn
    if ho_ref is not None:
        ho_ref[rows, :] = _rms(xn, gn_ref[...]).astype(ho_ref.dtype)


def _row_chains(tm):
    n = ROW_CHAINS if tm % (ROW_CHAINS * 128) == 0 else 1
    return [slice(c * (tm // n), (c + 1) * (tm // n)) for c in range(n)]


def _outproj_kernel(*refs, n_in):
    a_refs, w_refs = refs[:n_in], refs[n_in:2 * n_in]
    x_ref, gp_ref, gn_ref, xo_ref, ho_ref = refs[2 * n_in:]
    for rows in _row_chains(x_ref.shape[0]):
        acc = _dot(a_refs[0][rows, :], w_refs[0][...])
        for a_ref, w_ref in zip(a_refs[1:], w_refs[1:]):
            acc = acc + _dot(a_ref[rows, :], w_ref[...])
        _residual_out(acc, x_ref, gp_ref, gn_ref, xo_ref, ho_ref, rows)


def outproj(acts, w, x, g_post, g_next):
    m, d = x.shape
    tm = _pick(m, (512, 256, 128, 64))
    n_in = len(acts)
    row = lambda wd: pl.BlockSpec((tm, wd), lambda i: (i, 0))
    vec = pl.BlockSpec((1, d), lambda i: (0, 0))
    w_specs, r0 = [], 0
    for a in acts:
        ka = a.shape[1]
        assert r0 % ka == 0
        w_specs.append(pl.BlockSpec((ka, d), functools.partial(lambda blk, i: (blk, 0), r0 // ka)))
        r0 += ka
    assert r0 == w.shape[0]
    return pl.pallas_call(
        functools.partial(_outproj_kernel, n_in=n_in),
        grid=(m // tm,),
        in_specs=[row(a.shape[1]) for a in acts] + w_specs + [row(d), vec, vec],
        out_specs=[row(d), row(d)],
        out_shape=[jax.ShapeDtypeStruct((m, d), F32), jax.ShapeDtypeStruct((m, d), BF16)],
        compiler_params=_cparams(("parallel",), 56),
        name="outproj",
    )(*acts, *([w] * n_in), x, g_post.reshape(1, d), g_next.reshape(1, d))


def _xattn_kernel(h_ref, wq_ref, mk_ref, mv_ref, wo_ref, x_ref, gp_ref, gn_ref, xo_ref, ho_ref):
    mk, mv = mk_ref[0], mv_ref[0]
    t = mk.shape[0]
    chains = _row_chains(x_ref.shape[0])
    heads = [slice(hd * X_DH, (hd + 1) * X_DH) for hd in range(X_HEADS)]
    q = [(_dot(h_ref[rows, :], wq_ref[...]) * X_Q_PRESCALE).astype(BF16) for rows in chains]
    s = [[_dot_nt(qc[:, sl], mk[:, sl]) for sl in heads] for qc in q]
    p, inv_l = [], []
    for sc in s:
        pc, lc = [], []
        for sh in sc:
            e = jnp.exp2(sh - jnp.max(sh, axis=-1, keepdims=True))
            part = e[:, :LANES]
            for c in range(1, t // LANES):
                part = part + e[:, c * LANES:(c + 1) * LANES]
            pc.append(e.astype(BF16))
            lc.append(1.0 / jnp.sum(part, axis=-1, keepdims=True))
        p.append(pc)
        inv_l.append(lc)
    o = [jnp.concatenate([(_dot(pc[hd], mv[:, heads[hd]]) * lc[hd]).astype(BF16) for hd in range(X_HEADS)], axis=-1)
         for pc, lc in zip(p, inv_l)]
    acc = [_dot(oc, wo_ref[...]) for oc in o]
    for rows, ac in zip(chains, acc):
        _residual_out(ac, x_ref, gp_ref, gn_ref, xo_ref, ho_ref, rows)


def xattn_block(h, x, mem_k, mem_v, wq, wo, g_post, g_next, seq):
    m, d = x.shape
    tm = _pick(seq, (512, 256, 128, 64))
    per_b = seq // tm
    t, xw = mem_k.shape[1], mem_k.shape[2]
    full = lambda a: pl.BlockSpec(a.shape, lambda i: (0,) * a.ndim)
    row = lambda w: pl.BlockSpec((tm, w), lambda i: (i, 0))
    mem = pl.BlockSpec((1, t, xw), lambda i: (i // per_b, 0, 0))
    vec = pl.BlockSpec((1, d), lambda i: (0, 0))
    return pl.pallas_call(
        _xattn_kernel,
        grid=(m // tm,),
        in_specs=[row(d), full(wq), mem, mem, full(wo), row(d), vec, vec],
        out_specs=[row(d), row(d)],
        out_shape=[jax.ShapeDtypeStruct((m, d), F32), jax.ShapeDtypeStruct((m, d), BF16)],
        compiler_params=_cparams(("parallel",), 48),
        name="xattn_block",
    )(h, wq, mem_k, mem_v, wo, x, g_post.reshape(1, d), g_next.reshape(1, d))


def _mlp_kernel(h_ref, w1_ref, w2_ref, x_ref, gp_ref, gn_ref, xo_ref, *rest, emit_h):
    ho_ref, acc_scr = (rest[0], rest[1]) if emit_h else (None, rest[0])
    f = pl.program_id(1)
    @pl.when(f == 0)
    def _():
        acc_scr[...] = jnp.zeros(acc_scr.shape, F32)

    a = jnp.maximum(_dot(h_ref[...], w1_ref[...]), 0.0)
    acc_scr[...] += _dot((a * a).astype(BF16), w2_ref[...])

    @pl.when(f == pl.num_programs(1) - 1)
    def _():
        _residual_out(acc_scr[...], x_ref, gp_ref, gn_ref, xo_ref, ho_ref)


def mlp_block(h, x, w1, w2, layer, g_post, g_next):
    m, d = x.shape
    ff = w1.shape[2]
    tm = _pick(m, (512, 256, 128, 64))
    tf = _pick(ff, (1024, 512, 256, 128))
    emit_h = g_next is not None
    row = pl.BlockSpec((tm, d), lambda i, f: (i, 0))
    vec = pl.BlockSpec((1, d), lambda i, f: (0, 0))
    out_specs = [row] + ([row] if emit_h else [])
    out_shape = [jax.ShapeDtypeStruct((m, d), F32)] + ([jax.ShapeDtypeStruct((m, d), BF16)] if emit_h else [])
    g_n = (g_next if emit_h else g_post).reshape(1, d)
    res = pl.pallas_call(
        functools.partial(_mlp_kernel, emit_h=emit_h),
        grid=(m // tm, ff // tf),
        in_specs=[row, pl.BlockSpec((None, d, tf), lambda i, f: (layer, 0, f)),
                  pl.BlockSpec((None, tf, d), lambda i, f: (layer, f, 0)), row, vec, vec],
        out_specs=out_specs,
        out_shape=out_shape,
        scratch_shapes=[pltpu.VMEM((tm, d), F32)],
        compiler_params=_cparams(("parallel", "arbitrary"), 56),
        name="mlp_block",
    )(h, w1, w2, x, g_post.reshape(1, d), g_n)
    return (res[0], res[1]) if emit_h else (res[0], None)


def _rope_tables(pos):
    def angles(half):
        inv = jnp.power(ROPE_BASE, -jnp.arange(half, dtype=F32) / half)
        return pos.astype(F32)[:, None] * inv[None, :]
    a = angles(RET_DK // 2)
    c, s = jnp.cos(a), jnp.sin(a)
    t128 = (jnp.concatenate([c, c], axis=1), jnp.concatenate([-s, s], axis=1))
    a = angles(MLA_ROPE // 2)
    c, s = jnp.cos(a), jnp.sin(a)
    z = jnp.zeros((pos.shape[0], LANES - MLA_ROPE), F32)
    t64 = (jnp.concatenate([c, c, z], axis=1), jnp.concatenate([s, s, z], axis=1))
    return t128, t64


def _prep_weights(w_in_even, w_uq, w_ukv, w_out_even, w_in_odd, b_gates_odd, w_out_odd, w_xq, w_xo, w_mlp1, w_mlp2):
    d = w_in_even.shape[1]
    ret_w = RET_HEADS * RET_DK
    n_qkvg = 2 * ret_w + 2 * RET_HEADS * RET_DV
    n_lat = MLA_Q_LORA + MLA_KV_LORA + MLA_ROPE
    even = []
    for e in range(w_in_even.shape[0]):
        w = w_in_even[e].astype(BF16)
        wlat = jnp.pad(w[:, n_qkvg:n_qkvg + n_lat], ((0, 0), (0, LANES - MLA_ROPE)))
        uq = w_uq[e].reshape(MLA_Q_LORA, MLA_HEADS, MLA_NOPE + MLA_ROPE)
        uq_nope = uq[:, :, :MLA_NOPE].reshape(MLA_Q_LORA, MLA_HEADS * MLA_NOPE)
        uq_rope = jnp.pad(uq[:, :, MLA_NOPE:], ((0, 0), (0, 0), (0, LANES - MLA_ROPE)))
        uq_rope = uq_rope.reshape(MLA_Q_LORA, MLA_HEADS * LANES)
        ukv = w_ukv[e].reshape(MLA_KV_LORA, MLA_HEADS, MLA_NOPE + MLA_V)
        even.append(dict(
            w_in=w,
            w_lat=wlat,
            w_uq=jnp.concatenate([uq_nope, uq_rope], axis=1).astype(BF16),
            w_uk=ukv[:, :, :MLA_NOPE].reshape(MLA_KV_LORA, MLA_HEADS * MLA_NOPE).astype(BF16),
            w_uv=ukv[:, :, MLA_NOPE:].reshape(MLA_KV_LORA, MLA_HEADS * MLA_V).astype(BF16),
            w_out=w_out_even[e].astype(BF16),
        ))
    m_qk = M_HEADS * M_DK
    n_qkvo = 2 * m_qk + 2 * M_HEADS * M_DV
    odd = []
    for o in range(w_in_odd.shape[0]):
        w = w_in_odd[o].astype(BF16)
        odd.append(dict(
            w_in=w,
            w_gates=jnp.pad(w[:, n_qkvo:], ((0, 0), (0, LANES - 2 * M_HEADS))),
            b_gates=jnp.pad(b_gates_odd[o], (0, LANES - 2 * M_HEADS)),
            w_out=w_out_odd[o].astype(BF16),
        ))
    q_scale = jnp.concatenate([jnp.full((m_qk,), M_DK ** -0.5, F32), jnp.ones((n_qkvo - m_qk,), F32)])
    rk_scale = jnp.concatenate([jnp.ones((ret_w,), F32), jnp.full((ret_w,), RET_DK ** -0.5, F32)])
    shared = dict(w_xq=w_xq.astype(BF16), w_xo=w_xo.astype(BF16), w_mlp1=w_mlp1.astype(BF16),
                  w_mlp2=w_mlp2.astype(BF16), q_scale=q_scale, rk_scale=rk_scale, d=d,
                  n_qk=2 * ret_w, n_vg=n_qkvg - 2 * ret_w, n_qkvo=n_qkvo)
    return even, odd, shared


def _run_trunk(x, pos, mem_k, mem_v, ret_s0, c0, n0, m0, past, norm_g, gains, even_w, odd_w, shared):
    b, s, d = x.shape
    m = b * s
    depth = norm_g.shape[0]
    t128, t64 = _rope_tables(pos)
    if m % s or s % 64:
        raise ValueError("unsupported sequence length")
    if s < 256:
        t128 = tuple(jnp.tile(t, (b, 1)) for t in t128)
        t64 = tuple(jnp.tile(t, (b, 1)) for t in t64)
    xf = x.reshape(m, d)
    h = rms_rows(xf, norm_g[0, 0], BF16)
    n_even = len(even_w)
    ckv_slab = None
    krs, rets, cs, ns, ms = [], [], [], [], []
    for layer in range(depth):
        g = norm_g[layer]
        if layer % 2 == 0:
            e = layer // 2
            w = even_w[e]
            qk = matmul(h, w["w_in"], BF16, n_cols=shared["n_qk"], col_scale=shared["rk_scale"], rope=t128)
            vg = matmul(h, w["w_in"], BF16, col0=shared["n_qk"], n_cols=shared["n_vg"])
            ret, ret_s = retention(qk.reshape(b, s, -1), vg.reshape(b, s, -1), ret_s0[e], gains["ret_gn_g"][e])
            qcat, kcat, v, ckv_slab, kr = even_latent(h, w["w_lat"], gains["mla_q_norm_g"][e].reshape(1, -1),
                                                      gains["mla_kv_norm_g"][e].reshape(1, -1),
                                                      w["w_uq"], w["w_uk"], w["w_uv"], t64, e, n_even, ckv_slab)
            kcat = kcat.reshape(b, s, -1)
            v = v.reshape(b, s, -1)
            if past is None:
                q_off, sk = 0, s
            else:
                p_ckv, p_krp = past
                p_len = p_ckv.shape[2]
                q_off, sk = p_len, p_len + s
                pad = (-sk) % 256
                pk, pv = past_kv(p_ckv, p_krp, e, w["w_uk"], w["w_uv"], sk + pad)
                tail = lambda new: jnp.concatenate([new, jnp.zeros((b, pad, new.shape[-1]), BF16)], axis=1)
                kcat = lax.dynamic_update_slice(pk, tail(kcat), (0, p_len, 0))
                v = lax.dynamic_update_slice(pv, tail(v), (0, p_len, 0))
            att = mla_attention(qcat.reshape(b, s, -1), kcat, v, q_off=q_off, sk_valid=sk)
            xf, h = outproj([ret.reshape(m, -1), att.reshape(m, -1)], w["w_out"], xf, g[1], g[2])
            rets.append(ret_s)
            krs.append(kr.reshape(b, s, -1))
        else:
            o = layer // 2
            w = odd_w[o]
            qkvo = matmul(h, w["w_in"], BF16, n_cols=shared["n_qkvo"], col_scale=shared["q_scale"])
            gates = matmul(h, w["w_gates"], F32, col_bias=w["b_gates"])
            hh, c, n, mm = mlstm(qkvo.reshape(b, s, -1), gates.reshape(b, s, -1), c0[o], n0[o], m0[o],
                                 gains["mlstm_norm_g"][o])
            xf, h = outproj([hh.reshape(m, -1)], w["w_out"], xf, g[1], g[2])
            cs.append(c)
            ns.append(n)
            ms.append(mm)
        xf, h = xattn_block(h, xf, mem_k[layer], mem_v[layer], shared["w_xq"][layer], shared["w_xo"][layer],
                            g[3], g[4], s)
        g_next = norm_g[layer + 1, 0] if layer + 1 < depth else None
        xf, h = mlp_block(h, xf, shared["w_mlp1"], shared["w_mlp2"], layer, g[5], g_next)
    return (xf.reshape(b, s, d), ckv_slab.reshape(n_even, b, s, -1), jnp.stack(krs), jnp.stack(rets),
            jnp.stack(cs), jnp.stack(ns), jnp.stack(ms))


def kernel(x_prompt, x_sample, cache_mla_ckv, cache_mla_krope, state_ret, state_mlstm_C, state_mlstm_n, state_mlstm_m, cache_mem_k, cache_mem_v, mem_prompt, norm_g, mem_norm_g, w_in_even, mla_q_norm_g, mla_kv_norm_g, w_uq, w_ukv, ret_gn_g, w_out_even, w_in_odd, b_gates_odd, mlstm_norm_g, w_out_odd, w_xq, w_xk, w_xv, w_xo, w_mlp1, w_mlp2):
    even_w, odd_w, shared = _prep_weights(w_in_even, w_uq, w_ukv, w_out_even, w_in_odd, b_gates_odd, w_out_odd,
                                          w_xq, w_xo, w_mlp1, w_mlp2)
    gains = dict(ret_gn_g=ret_gn_g, mla_q_norm_g=mla_q_norm_g, mla_kv_norm_g=mla_kv_norm_g,
                 mlstm_norm_g=mlstm_norm_g)
    depth = norm_g.shape[0]
    b, s, d = x_prompt.shape
    n_even, n_odd = w_in_even.shape[0], w_in_odd.shape[0]
    xw = X_HEADS * X_DH

    bm, t, _ = mem_prompt.shape
    mem_flat = mem_prompt.reshape(bm * t, d)
    p_mem_k, p_mem_v = [], []
    for layer in range(depth):
        mn = rms_rows(mem_flat, mem_norm_g[layer], BF16)
        kv = matmul(mn, jnp.concatenate([w_xk[layer], w_xv[layer]], axis=1).astype(BF16), F32)
        p_mem_k.append(kv[:, :xw].reshape(bm, t, X_HEADS, X_DH))
        p_mem_v.append(kv[:, xw:].reshape(bm, t, X_HEADS, X_DH))
    p_mem_k, p_mem_v = jnp.stack(p_mem_k), jnp.stack(p_mem_v)

    pos_p = jnp.arange(s, dtype=jnp.int32)
    zeros = lambda *shape: jnp.zeros(shape, F32)
    y_prompt, p_ckv, p_kr, p_ret, p_c, p_n, p_m = _run_trunk(
        x_prompt, pos_p, p_mem_k.reshape(depth, bm, t, xw).astype(BF16), p_mem_v.reshape(depth, bm, t, xw).astype(BF16),
        zeros(n_even, b, RET_HEADS, RET_DK, RET_DV), zeros(n_odd, b, M_HEADS, M_DV, M_DK),
        zeros(n_odd, b, M_HEADS, M_DK), zeros(n_odd, b, M_HEADS), None, norm_g, gains, even_w, odd_w, shared)

    db, ds, _ = x_sample.shape
    past_len = cache_mla_ckv.shape[2]
    pos_s = past_len + jnp.arange(ds, dtype=jnp.int32)
    past = (cache_mla_ckv, jnp.pad(cache_mla_krope, ((0, 0), (0, 0), (0, 0), (0, LANES - MLA_ROPE))))
    y_sample, s_ckv, s_kr, s_ret, s_c, s_n, s_m = _run_trunk(
        x_sample, pos_s, cache_mem_k.reshape(depth, db, -1, xw).astype(BF16),
        cache_mem_v.reshape(depth, db, -1, xw).astype(BF16),
        state_ret, state_mlstm_C, state_mlstm_n, state_mlstm_m, past, norm_g, gains, even_w, odd_w, shared)

    return (y_prompt, y_sample, p_ckv, p_kr, p_ret, p_c, p_n, p_m, p_mem_k, p_mem_v,
            s_ckv, s_kr, s_ret, s_c, s_n, s_m)
```

```python
import functools

import jax
import jax.numpy as jnp
from jax import lax
from jax.experimental import pallas as pl
from jax.experimental.pallas import tpu as pltpu

F32 = jnp.float32
BF16 = jnp.bfloat16

RMS_EPS = 1e-6
ROPE_BASE = 10000.0
CHUNK = 64
CHUNK_SHIFT = 6
assert 1 << CHUNK_SHIFT == CHUNK

LOG2E = 1.4426950408889634
RET_HEADS, RET_DK, RET_DV = 8, 128, 128
MLA_HEADS, MLA_NOPE, MLA_ROPE, MLA_V = 8, 128, 64, 128
MLA_Q_LORA, MLA_KV_LORA = 768, 512
MLA_Q_PRESCALE = (MLA_NOPE + MLA_ROPE) ** -0.5 * LOG2E
M_HEADS, M_DK, M_DV = 8, 128, 256
X_HEADS, X_DH = 4, 128
X_Q_PRESCALE = X_DH ** -0.5 * LOG2E

LANES = 128
MXU_DIM = 256
V7X_VMEM_BYTES = 64 * 1024 * 1024
MIB = 1024 * 1024

MLA_QK_PAD = MXU_DIM
KV_PAD_MULTIPLE = MXU_DIM
KV_UNROLL = 2
MLA_HEADS_PER_STEP = 2
ATTN_SINGLE_STEP_ELEMS = 512 * 512
HEAD_GROUPS = (range(0, 4), range(4, 8))
ROW_CHAINS = 2
MIN_ROPE_TABLE_ROWS = 256


def _cparams(semantics, vmem_mib):
    assert vmem_mib * MIB < V7X_VMEM_BYTES
    return pltpu.CompilerParams(dimension_semantics=semantics, vmem_limit_bytes=vmem_mib * MIB)


def _pick(n, cands):
    for c in cands:
        if c <= n and n % c == 0:
            return c
    raise ValueError(f"no tile for {n} in {cands}")


def _rms(x, g):
    ms = jnp.mean(x * x, axis=-1, keepdims=True)
    return x * lax.rsqrt(ms + RMS_EPS) * g


def _sigmoid(x):
    return 1.0 / (1.0 + jnp.exp(-x))


def _dot(a, b):
    return jnp.dot(a, b, preferred_element_type=F32)


def _dot_nt(a, b):
    return lax.dot_general(a, b, (((1,), (1,)), ((), ())), preferred_element_type=F32)


def _dot_tn(a, b):
    return lax.dot_general(a, b, (((0,), (0,)), ((), ())), preferred_element_type=F32)


def _rope128(x, cos2, sin2):
    return x * cos2 + pltpu.roll(x, 64, 1) * sin2


def _rope64(x, cos_p, sin_p):
    return x * cos_p + (pltpu.roll(x, 32, 1) - pltpu.roll(x, 96, 1)) * sin_p


def _norm_kernel(x_ref, g_ref, o_ref):
    o_ref[...] = _rms(x_ref[...].astype(F32), g_ref[...]).astype(o_ref.dtype)


def rms_rows(x, g, out_dtype):
    m, k = x.shape
    tm = _pick(m, (512, 256, 128, 64, 32, 16, 8))
    return pl.pallas_call(
        _norm_kernel,
        grid=(m // tm,),
        in_specs=[pl.BlockSpec((tm, k), lambda i: (i, 0)), pl.BlockSpec((1, k), lambda i: (0, 0))],
        out_specs=pl.BlockSpec((tm, k), lambda i: (i, 0)),
        out_shape=jax.ShapeDtypeStruct((m, k), out_dtype),
        compiler_params=_cparams(("parallel",), 32),
        name="rms_rows",
    )(x, g.reshape(1, k).astype(F32))


def _mm_kernel(*refs, has_scale, has_bias, n_rope, tn):
    it = iter(refs)
    h_ref, w_ref = next(it), next(it)
    scale_ref = next(it) if has_scale else None
    bias_ref = next(it) if has_bias else None
    cos_ref, sin_ref = (next(it), next(it)) if n_rope else (None, None)
    o_ref = next(it)
    acc = _dot(h_ref[...], w_ref[...])
    if has_scale:
        acc = acc * scale_ref[...]
    if has_bias:
        acc = acc + bias_ref[...]
    if n_rope:
        c, s = cos_ref[...], sin_ref[...]
        for t in range(tn // LANES):
            sl = slice(t * LANES, (t + 1) * LANES)
            o_ref[:, sl] = _rope128(acc[:, sl], c, s).astype(o_ref.dtype)
    else:
        o_ref[...] = acc.astype(o_ref.dtype)


def matmul(h, w, out_dtype, *, col0=0, n_cols=None, col_scale=None, col_bias=None, rope=None):
    m, k = h.shape
    n = w.shape[1] - col0 if n_cols is None else n_cols
    tm = _pick(m, (1024, 512, 256, 128, 64))
    tn = _pick(n, (512, 256, 128))
    assert col0 % tn == 0 and col0 + n <= w.shape[1]
    jb = col0 // tn
    args = [h, w]
    in_specs = [pl.BlockSpec((tm, k), lambda i, j: (i, 0)), pl.BlockSpec((k, tn), lambda i, j: (0, j + jb))]
    for vec in (col_scale, col_bias):
        if vec is not None:
            args.append(vec.reshape(1, n).astype(F32))
            in_specs.append(pl.BlockSpec((1, tn), lambda i, j: (0, j)))
    if rope is not None:
        r = rope[0].shape[0]
        tm = _pick(m, tuple(c for c in (1024, 512, 256, 128, 64) if r % c == 0))
        in_specs[0] = pl.BlockSpec((tm, k), lambda i, j: (i, 0))
        nrb = r // tm
        for tab in rope:
            args.append(tab)
            in_specs.append(pl.BlockSpec((tm, LANES), lambda i, j: (i % nrb, 0)))
    kern = functools.partial(_mm_kernel, has_scale=col_scale is not None, has_bias=col_bias is not None,
                             n_rope=rope is not None, tn=tn)
    return pl.pallas_call(
        kern,
        grid=(m // tm, n // tn),
        in_specs=in_specs,
        out_specs=pl.BlockSpec((tm, tn), lambda i, j: (i, j)),
        out_shape=jax.ShapeDtypeStruct((m, n), out_dtype),
        compiler_params=_cparams(("parallel", "arbitrary"), 40),
        name="matmul",
    )(*args)


def _kv_expand(ckvn_bf16, krr_bf16, wuk_ref, wuv_ref, kcat_ref, v_ref):
    kn = _dot(ckvn_bf16, wuk_ref[...])
    for hd in range(MLA_HEADS):
        kcat_ref[:, hd * MLA_QK_PAD: hd * MLA_QK_PAD + MLA_NOPE] = (
            kn[:, hd * MLA_NOPE:(hd + 1) * MLA_NOPE].astype(BF16))
        kcat_ref[:, hd * MLA_QK_PAD + MLA_NOPE:(hd + 1) * MLA_QK_PAD] = krr_bf16
    v_ref[...] = _dot(ckvn_bf16, wuv_ref[...]).astype(BF16)


def _even_latent_kernel(h_ref, wlat_ref, gq_ref, gkv_ref, wuq_ref, wuk_ref, wuv_ref, cos_ref, sin_ref, *refs):
    qcat_ref, kcat_ref, v_ref, ckv_ref, kr_ref = refs[-5:]
    lat = _dot(h_ref[...], wlat_ref[...])
    cq = lat[:, :MLA_Q_LORA]
    ckv = lat[:, MLA_Q_LORA:MLA_Q_LORA + MLA_KV_LORA]
    krp = lat[:, MLA_Q_LORA + MLA_KV_LORA:]
    c, s = cos_ref[...], sin_ref[...]
    q = _dot(_rms(cq, gq_ref[...]).astype(BF16), wuq_ref[...]) * MLA_Q_PRESCALE
    nope_w = MLA_HEADS * MLA_NOPE
    for hd in range(MLA_HEADS):
        qcat_ref[:, hd * MLA_QK_PAD: hd * MLA_QK_PAD + MLA_NOPE] = (
            q[:, hd * MLA_NOPE:(hd + 1) * MLA_NOPE].astype(BF16))
        qr = q[:, nope_w + hd * LANES: nope_w + (hd + 1) * LANES]
        qcat_ref[:, hd * MLA_QK_PAD + MLA_NOPE:(hd + 1) * MLA_QK_PAD] = _rope64(qr, c, s).astype(BF16)
    ckvn = _rms(ckv, gkv_ref[...])
    ckv_ref[...] = ckvn
    krr = _rope64(krp, c, s)
    kr_ref[...] = krr[:, :MLA_ROPE]
    _kv_expand(ckvn.astype(BF16), krr.astype(BF16), wuk_ref, wuv_ref, kcat_ref, v_ref)


def even_latent(h, wlat, gq, gkv, wuq, wuk, wuv, rope64, e, n_even, ckv_slab):
    m, d = h.shape
    r = rope64[0].shape[0]
    tm = _pick(m, tuple(c for c in (256, 128, 64) if r % c == 0))
    nrb = r // tm
    full = lambda a: pl.BlockSpec(a.shape, lambda i: (0,) * a.ndim)
    row = lambda w: pl.BlockSpec((tm, w), lambda i: (i, 0))
    tab = pl.BlockSpec((tm, LANES), lambda i: (i % nrb, 0))
    qk_w = MLA_HEADS * MLA_QK_PAD
    v_w = MLA_HEADS * MLA_V
    args = [h, wlat, gq, gkv, wuq, wuk, wuv, rope64[0], rope64[1]]
    in_specs = [row(d), full(wlat), full(gq), full(gkv), full(wuq), full(wuk), full(wuv), tab, tab]
    aliases = {}
    if ckv_slab is not None:
        aliases = {len(args): 3}
        args.append(ckv_slab)
        in_specs.append(pl.BlockSpec(memory_space=pl.ANY))
    return pl.pallas_call(
        _even_latent_kernel,
        grid=(m // tm,),
        in_specs=in_specs,
        out_specs=[row(qk_w), row(qk_w), row(v_w),
                   pl.BlockSpec((None, tm, MLA_KV_LORA), lambda i: (e, i, 0)), row(MLA_ROPE)],
        out_shape=[jax.ShapeDtypeStruct((m, qk_w), BF16), jax.ShapeDtypeStruct((m, qk_w), BF16),
                   jax.ShapeDtypeStruct((m, v_w), BF16), jax.ShapeDtypeStruct((n_even, m, MLA_KV_LORA), F32),
                   jax.ShapeDtypeStruct((m, MLA_ROPE), F32)],
        input_output_aliases=aliases,
        compiler_params=_cparams(("parallel",), 48),
        name="even_latent",
    )(*args)


def _past_kv_kernel(ckv_ref, krp_ref, wuk_ref, wuv_ref, kcat_ref, v_ref):
    _kv_expand(ckv_ref[...].astype(BF16), krp_ref[...].astype(BF16), wuk_ref, wuv_ref, kcat_ref, v_ref)


def past_kv(ckvn, kr_pad, e, wuk, wuv, skp):
    _, b, p_len, _ = ckvn.shape
    tm = _pick(p_len, (512, 256, 128, 64))
    full = lambda a: pl.BlockSpec(a.shape, lambda bi, i: (0,) * a.ndim)
    src = lambda w: pl.BlockSpec((None, None, tm, w), lambda bi, i: (e, bi, i, 0))
    dst = lambda w: pl.BlockSpec((None, tm, w), lambda bi, i: (bi, i, 0))
    qk_w = MLA_HEADS * MLA_QK_PAD
    v_w = MLA_HEADS * MLA_V
    return pl.pallas_call(
        _past_kv_kernel,
        grid=(b, p_len // tm),
        in_specs=[src(MLA_KV_LORA), src(LANES), full(wuk), full(wuv)],
        out_specs=[dst(qk_w), dst(v_w)],
        out_shape=[jax.ShapeDtypeStruct((b, skp, qk_w), BF16), jax.ShapeDtypeStruct((b, skp, v_w), BF16)],
        compiler_params=_cparams(("parallel", "parallel"), 32),
        name="past_kv",
    )(ckvn, kr_pad, wuk, wuv)


def _mla_kernel(q_ref, k_ref, v_ref, o_ref, m_scr, l_scr, acc_scr, a_scr, p_scr, *, tq, tk, hb, q_off, sk_valid):
    qi = pl.program_id(2)
    q0 = qi * tq
    cq_lo = (q_off + q0) // CHUNK
    cq_hi = (q_off + q0 + tq - 1) // CHUNK
    n_full = jnp.minimum(lax.div((cq_lo + 1) * CHUNK, tk), sk_valid // tk)
    n_vis = lax.div(jnp.minimum((cq_hi + 1) * CHUNK, sk_valid) + tk - 1, tk)
    m_scr[...] = jnp.full(m_scr.shape, -1e30, F32)
    l_scr[...] = jnp.zeros(l_scr.shape, F32)
    acc_scr[...] = jnp.zeros(acc_scr.shape, F32)
    a_scr[...] = jnp.ones(a_scr.shape, F32)
    p_scr[...] = jnp.zeros(p_scr.shape, BF16)

    def apply_prev(j, ki_prev):
        v = v_ref[0, pl.ds(pl.multiple_of(ki_prev * tk, tk), tk), j * MLA_V:(j + 1) * MLA_V]
        acc_scr[j] = a_scr[j] * acc_scr[j] + _dot(p_scr[j], v)

    def step(ki, masked):
        k0 = pl.multiple_of(ki * tk, tk)
        ki_prev = jnp.maximum(ki - 1, 0)
        if masked:
            qpos = q_off + q0 + lax.broadcasted_iota(jnp.int32, (tq, tk), 0)
            kpos = k0 + lax.broadcasted_iota(jnp.int32, (tq, tk), 1)
            vis = jnp.logical_and((kpos >> CHUNK_SHIFT) <= (qpos >> CHUNK_SHIFT), kpos < sk_valid)
        for j in range(hb):
            q = q_ref[0, :, j * MLA_QK_PAD:(j + 1) * MLA_QK_PAD]
            k = k_ref[0, pl.ds(k0, tk), j * MLA_QK_PAD:(j + 1) * MLA_QK_PAD]
            s = _dot_nt(q, k)
            apply_prev(j, ki_prev)
            if masked:
                s = jnp.where(vis, s, -jnp.inf)
            m_prev = m_scr[j]
            m_new = jnp.maximum(m_prev, jnp.max(s, axis=-1, keepdims=True))
            alpha = jnp.exp2(m_prev - m_new)
            p = [jnp.exp2(s[:, c * LANES:(c + 1) * LANES] - m_new) for c in range(tk // LANES)]
            l_scr[j] = alpha * l_scr[j] + sum(p[1:], p[0])
            p_scr[j] = jnp.concatenate([pc.astype(BF16) for pc in p], axis=-1)
            a_scr[j] = alpha
            m_scr[j] = m_new

    def body_full(ki, carry):
        step(ki, False)
        return carry

    def body_masked(ki, carry):
        step(ki, True)
        return carry

    def body_full_group(kg, carry):
        for u in range(KV_UNROLL):
            step(KV_UNROLL * kg + u, False)
        return carry

    n_groups = lax.div(n_full, KV_UNROLL)
    lax.fori_loop(0, n_groups, body_full_group, 0)
    lax.fori_loop(KV_UNROLL * n_groups, n_full, body_full, 0)
    lax.fori_loop(n_full, n_vis, body_masked, 0)
    for j in range(hb):
        apply_prev(j, n_vis - 1)
        l = jnp.sum(l_scr[j], axis=-1, keepdims=True)
        o_ref[0, :, j * MLA_V:(j + 1) * MLA_V] = (acc_scr[j] / l).astype(o_ref.dtype)


def mla_attention(qcat, kcat, v, *, q_off, sk_valid):
    b, sq, _ = qcat.shape
    skp = kcat.shape[1]
    tq = _pick(sq, (512, 256, 128, 64))
    tk = skp if tq * skp <= ATTN_SINGLE_STEP_ELEMS else _pick(skp, (512, 256, 128))
    hb = MLA_HEADS if tk == skp else MLA_HEADS_PER_STEP
    kern = functools.partial(_mla_kernel, tq=tq, tk=tk, hb=hb, q_off=q_off, sk_valid=sk_valid)
    return pl.pallas_call(
        kern,
        grid=(b, MLA_HEADS // hb, sq // tq),
        in_specs=[pl.BlockSpec((1, tq, hb * MLA_QK_PAD), lambda bi, h, qi: (bi, qi, h)),
                  pl.BlockSpec((1, skp, hb * MLA_QK_PAD), lambda bi, h, qi: (bi, 0, h)),
                  pl.BlockSpec((1, skp, hb * MLA_V), lambda bi, h, qi: (bi, 0, h))],
        out_specs=pl.BlockSpec((1, tq, hb * MLA_V), lambda bi, h, qi: (bi, qi, h)),
        out_shape=jax.ShapeDtypeStruct((b, sq, MLA_HEADS * MLA_V), BF16),
        scratch_shapes=[pltpu.VMEM((hb, tq, LANES), F32), pltpu.VMEM((hb, tq, LANES), F32),
                        pltpu.VMEM((hb, tq, MLA_V), F32), pltpu.VMEM((hb, tq, LANES), F32),
                        pltpu.VMEM((hb, tq, tk), BF16)],
        compiler_params=_cparams(("parallel", "parallel", "arbitrary"), 48),
        name="mla_attention",
    )(qcat, kcat, v)


def _retention_kernel(lg_ref, qk_ref, vg_ref, s0_ref, gn_ref, out_ref, sfin_ref,
                      state_scr, decay_scr, qdec_scr, kdec_scr, *, lc):
    ci = pl.program_id(1)
    heads = range(RET_HEADS)

    @pl.when(ci == 0)
    def _():
        state_scr[...] = s0_ref[0]
        ti = lax.broadcasted_iota(jnp.int32, (lc, lc), 0)
        si = lax.broadcasted_iota(jnp.int32, (lc, lc), 1)
        dpos = jnp.maximum(ti - si, 0).astype(F32)
        trow = lax.broadcasted_iota(jnp.int32, (lc, LANES), 0).astype(F32)
        for h in heads:
            lg = lg_ref[h:h + 1, :]
            decay_scr[h] = jnp.where(ti >= si, jnp.exp(lg[:, 0:1] * dpos), 0.0)
            qdec_scr[h] = jnp.exp(lg * (trow + 1.0))
            kdec_scr[h] = jnp.exp(lg * (lc - 1.0 - trow))

    qk_w = RET_HEADS * RET_DK
    v_w = RET_HEADS * RET_DV
    q = [qk_ref[0, :, h * RET_DK:(h + 1) * RET_DK] for h in heads]
    k = [qk_ref[0, :, qk_w + h * RET_DK: qk_w + (h + 1) * RET_DK] for h in heads]
    v = [vg_ref[0, :, h * RET_DV:(h + 1) * RET_DV] for h in heads]
    st = [state_scr[h] for h in heads]
    sc = [_dot_nt(q[h], k[h]) for h in heads]
    inter = [_dot(q[h], st[h].astype(BF16)) for h in heads]
    scd = [(sc[h] * decay_scr[h]).astype(BF16) for h in heads]
    kd = [(k[h].astype(F32) * kdec_scr[h]).astype(BF16) for h in heads]
    intra = [_dot(scd[h], v[h]) for h in heads]
    upd = [_dot_tn(kd[h], v[h]) for h in heads]
    for h in heads:
        state_scr[h] = jnp.exp(lg_ref[h:h + 1, :] * lc) * st[h] + upd[h]
        out = intra[h] + qdec_scr[h] * inter[h]
        g = vg_ref[0, :, v_w + h * RET_DV: v_w + (h + 1) * RET_DV].astype(F32)
        xc = out - jnp.mean(out, axis=-1, keepdims=True)
        y = xc * lax.rsqrt(jnp.mean(xc * xc, axis=-1, keepdims=True) + RMS_EPS)
        y = y * gn_ref[:, h * RET_DV:(h + 1) * RET_DV]
        out_ref[0, :, h * RET_DV:(h + 1) * RET_DV] = (g * _sigmoid(g) * y).astype(out_ref.dtype)

    @pl.when(ci == pl.num_programs(1) - 1)
    def _():
        sfin_ref[0] = state_scr[...]


def retention(qk, vg, s0, gn_g):
    b, s, _ = qk.shape
    lc = _pick(s, (256, 128, 64))
    log_g = jnp.log1p(-jnp.exp2(-5.0 - jnp.arange(RET_HEADS, dtype=F32)))
    lg_tab = jnp.broadcast_to(log_g[:, None], (RET_HEADS, LANES))
    v_w = RET_HEADS * RET_DV
    kern = functools.partial(_retention_kernel, lc=lc)
    st_spec = pl.BlockSpec((1, RET_HEADS, RET_DK, RET_DV), lambda bi, ci: (bi, 0, 0, 0))
    return pl.pallas_call(
        kern,
        grid=(b, s // lc),
        in_specs=[pl.BlockSpec((RET_HEADS, LANES), lambda bi, ci: (0, 0)),
                  pl.BlockSpec((1, lc, qk.shape[2]), lambda bi, ci: (bi, ci, 0)),
                  pl.BlockSpec((1, lc, vg.shape[2]), lambda bi, ci: (bi, ci, 0)),
                  st_spec,
                  pl.BlockSpec((1, v_w), lambda bi, ci: (0, 0))],
        out_specs=[pl.BlockSpec((1, lc, v_w), lambda bi, ci: (bi, ci, 0)), st_spec],
        out_shape=[jax.ShapeDtypeStruct((b, s, v_w), BF16),
                   jax.ShapeDtypeStruct((b, RET_HEADS, RET_DK, RET_DV), F32)],
        scratch_shapes=[pltpu.VMEM((RET_HEADS, RET_DK, RET_DV), F32), pltpu.VMEM((RET_HEADS, lc, lc), F32),
                        pltpu.VMEM((RET_HEADS, lc, LANES), F32), pltpu.VMEM((RET_HEADS, lc, LANES), F32)],
        compiler_params=_cparams(("parallel", "arbitrary"), 40),
        name="retention",
    )(lg_tab, qk, vg, s0, gn_g.reshape(1, v_w).astype(F32))


def _split3(x):
    hi = x.astype(BF16)
    r1 = x - hi.astype(F32)
    mid = r1.astype(BF16)
    lo = (r1 - mid.astype(F32)).astype(BF16)
    return hi, mid, lo


def _log_sigmoid(x):
    return jnp.minimum(x, 0.0) - jnp.log(1.0 + jnp.exp(-jnp.abs(x)))


def _widen(x, width):
    if width < LANES:
        return x[:, :width]
    return x if width == LANES else jnp.concatenate([x] * (width // LANES), axis=1)


def _mlstm_kernel(qkvo_ref, gates_ref, gates_t_ref, c0_ref, n0_ref, m0_ref, ng_ref,
                  out_ref, cfin_ref, nfin_ref, mfin_ref, c_scr, n_scr, m_scr, *, lc):
    ci = pl.program_id(1)

    @pl.when(ci == 0)
    def _():
        c_scr[...] = c0_ref[0]
        n_scr[...] = n0_ref[0]
        m_scr[...] = m0_ref[0]

    gates = gates_ref[0]
    gates_t = gates_t_ref[0]
    ti = lax.broadcasted_iota(jnp.int32, (lc, lc), 0)
    si = lax.broadcasted_iota(jnp.int32, (lc, lc), 1)
    tril = si <= ti
    ones_l = jnp.where(tril, 1.0, 0.0).astype(BF16)
    ones_u = jnp.where(ti <= si, 1.0, 0.0).astype(BF16)
    b_col = sum(_dot(ones_l, part) for part in _split3(_log_sigmoid(gates)))
    b_row = sum(_dot(part, ones_u) for part in _split3(_log_sigmoid(gates_t)))[M_HEADS:2 * M_HEADS]
    i_row = gates_t[0:M_HEADS]
    qk_w = M_HEADS * M_DK
    v_w = M_HEADS * M_DV
    for heads in HEAD_GROUPS:
        q = {h: qkvo_ref[0, :, h * M_DK:(h + 1) * M_DK] for h in heads}
        k = {h: qkvo_ref[0, :, qk_w + h * M_DK: qk_w + (h + 1) * M_DK] for h in heads}
        v = {h: qkvo_ref[0, :, 2 * qk_w + h * M_DV: 2 * qk_w + (h + 1) * M_DV] for h in heads}
        c_st = {h: c_scr[h] for h in heads}
        n_st = {h: n_scr[h:h + 1, :] for h in heads}
        m_prev = {h: m_scr[h:h + 1, :] for h in heads}
        qk = {h: _dot_nt(q[h], k[h]) for h in heads}
        qc = {h: _dot(q[h], c_st[h].astype(BF16)) for h in heads}
        bt = {h: jnp.broadcast_to(b_col[:, M_HEADS + h:M_HEADS + h + 1], (lc, LANES)) for h in heads}
        i_t = {h: jnp.broadcast_to(gates[:, h:h + 1], (lc, LANES)) for h in heads}
        m_t, w_inter, sc = {}, {}, {}
        for h in heads:
            log_intra = jnp.where(tril, _widen(bt[h], lc) - b_row[h:h + 1, :] + i_row[h:h + 1, :], -jnp.inf)
            log_inter = bt[h] + m_prev[h]
            m_t[h] = jnp.maximum(log_inter, jnp.max(log_intra, axis=-1, keepdims=True))
            w_inter[h] = jnp.exp(log_inter - m_t[h])
            sc[h] = qk[h] * jnp.exp(log_intra - _widen(m_t[h], lc))
        num_intra = {h: _dot(sc[h].astype(BF16), v[h]) for h in heads}
        kw = {}
        for h in heads:
            m_new = m_t[h][lc - 1:lc, :]
            b_last = bt[h][lc - 1:lc, :]
            w_state = jnp.exp(b_last + m_prev[h] - m_new)
            kw[h] = k[h].astype(F32) * jnp.exp(b_last - bt[h] + i_t[h] - m_new)
            n_scr[h:h + 1, :] = w_state * n_st[h] + jnp.sum(kw[h], axis=0, keepdims=True)
            m_scr[h:h + 1, :] = m_new
            c_scr[h] = _widen(w_state, M_DV) * c_st[h] + _dot_tn(kw[h].astype(BF16), v[h])
        for h in heads:
            num = num_intra[h] + _widen(w_inter[h], M_DV) * qc[h]
            den = (jnp.sum(sc[h], axis=-1, keepdims=True)
                   + w_inter[h] * jnp.sum(q[h].astype(F32) * n_st[h], axis=-1, keepdims=True))
            hh = num * _widen(1.0 / jnp.maximum(jnp.abs(den), jnp.exp(-m_t[h])), M_DV)
            y = hh * lax.rsqrt(jnp.mean(hh * hh, axis=-1, keepdims=True) + RMS_EPS)
            y = y * ng_ref[:, h * M_DV:(h + 1) * M_DV]
            og = qkvo_ref[0, :, 2 * qk_w + v_w + h * M_DV: 2 * qk_w + v_w + (h + 1) * M_DV].astype(F32)
            out_ref[0, :, h * M_DV:(h + 1) * M_DV] = (y * _sigmoid(og)).astype(out_ref.dtype)

    @pl.when(ci == pl.num_programs(1) - 1)
    def _():
        cfin_ref[0] = c_scr[...]
        nfin_ref[0] = n_scr[...]
        mfin_ref[0] = m_scr[...]


def mlstm(qkvo, gates, c0, n0, m0, norm_g):
    b, s, w = qkvo.shape
    lc = _pick(s, (256, 128, 64))
    v_w = M_HEADS * M_DV
    gates_t = jnp.swapaxes(gates[:, :, :2 * M_HEADS], 1, 2)
    kern = functools.partial(_mlstm_kernel, lc=lc)
    c_spec = pl.BlockSpec((1, M_HEADS, M_DK, M_DV), lambda bi, ci: (bi, 0, 0, 0))
    n_spec = pl.BlockSpec((1, M_HEADS, M_DK), lambda bi, ci: (bi, 0, 0))
    m_spec = pl.BlockSpec((1, M_HEADS, LANES), lambda bi, ci: (bi, 0, 0))
    hh, c_t, n, m = pl.pallas_call(
        kern,
        grid=(b, s // lc),
        in_specs=[pl.BlockSpec((1, lc, w), lambda bi, ci: (bi, ci, 0)),
                  pl.BlockSpec((1, lc, LANES), lambda bi, ci: (bi, ci, 0)),
                  pl.BlockSpec((1, 2 * M_HEADS, lc), lambda bi, ci: (bi, 0, ci)),
                  c_spec, n_spec, m_spec,
                  pl.BlockSpec((1, v_w), lambda bi, ci: (0, 0))],
        out_specs=[pl.BlockSpec((1, lc, v_w), lambda bi, ci: (bi, ci, 0)), c_spec, n_spec, m_spec],
        out_shape=[jax.ShapeDtypeStruct((b, s, v_w), BF16),
                   jax.ShapeDtypeStruct((b, M_HEADS, M_DK, M_DV), F32),
                   jax.ShapeDtypeStruct((b, M_HEADS, M_DK), F32),
                   jax.ShapeDtypeStruct((b, M_HEADS, LANES), F32)],
        scratch_shapes=[pltpu.VMEM((M_HEADS, M_DK, M_DV), F32), pltpu.VMEM((M_HEADS, M_DK), F32),
                        pltpu.VMEM((M_HEADS, LANES), F32)],
        compiler_params=_cparams(("parallel", "arbitrary"), 48),
        name="mlstm",
    )(qkvo, gates, gates_t, jnp.swapaxes(c0, 2, 3), n0, jnp.broadcast_to(m0[..., None], m0.shape + (LANES,)),
      norm_g.reshape(1, v_w).astype(F32))
    return hh, jnp.swapaxes(c_t, 2, 3), n, m[..., 0]


def _residual_out(acc, x_ref, gp_ref, gn_ref, xo_ref, ho_ref, rows=slice(None)):
    xn = x_ref[rows, :] + _rms(acc, gp_ref[...])
    xo_ref[rows, :] = xn
    if ho_ref is not None:
        ho_ref[rows, :] = _rms(xn, gn_ref[...]).astype(ho_ref.dtype)


def _row_chains(tm):
    n = ROW_CHAINS if tm % (ROW_CHAINS * 128) == 0 else 1
    return [slice(c * (tm // n), (c + 1) * (tm // n)) for c in range(n)]


def _outproj_kernel(*refs, n_in):
    a_refs, w_refs = refs[:n_in], refs[n_in:2 * n_in]
    x_ref, gp_ref, gn_ref, xo_ref, ho_ref = refs[2 * n_in:]
    for rows in _row_chains(x_ref.shape[0]):
        acc = _dot(a_refs[0][rows, :], w_refs[0][...])
        for a_ref, w_ref in zip(a_refs[1:], w_refs[1:]):
            acc = acc + _dot(a_ref[rows, :], w_ref[...])
        _residual_out(acc, x_ref, gp_ref, gn_ref, xo_ref, ho_ref, rows)


def outproj(acts, w, x, g_post, g_next):
    m, d = x.shape
    tm = _pick(m, (512, 256, 128, 64))
    n_in = len(acts)
    row = lambda wd: pl.BlockSpec((tm, wd), lambda i: (i, 0))
    vec = pl.BlockSpec((1, d), lambda i: (0, 0))
    w_specs, r0 = [], 0
    for a in acts:
        ka = a.shape[1]
        assert r0 % ka == 0
        w_specs.append(pl.BlockSpec((ka, d), functools.partial(lambda blk, i: (blk, 0), r0 // ka)))
        r0 += ka
    assert r0 == w.shape[0]
    return pl.pallas_call(
        functools.partial(_outproj_kernel, n_in=n_in),
        grid=(m // tm,),
        in_specs=[row(a.shape[1]) for a in acts] + w_specs + [row(d), vec, vec],
        out_specs=[row(d), row(d)],
        out_shape=[jax.ShapeDtypeStruct((m, d), F32), jax.ShapeDtypeStruct((m, d), BF16)],
        compiler_params=_cparams(("parallel",), 56),
        name="outproj",
    )(*acts, *([w] * n_in), x, g_post.reshape(1, d), g_next.reshape(1, d))


def _xattn_kernel(h_ref, wq_ref, mk_ref, mv_ref, wo_ref, x_ref, gp_ref, gn_ref, xo_ref, ho_ref):
    mk, mv = mk_ref[0], mv_ref[0]
    t = mk.shape[0]
    chains = _row_chains(x_ref.shape[0])
    heads = [slice(hd * X_DH, (hd + 1) * X_DH) for hd in range(X_HEADS)]
    q = [(_dot(h_ref[rows, :], wq_ref[...]) * X_Q_PRESCALE).astype(BF16) for rows in chains]
    s = [[_dot_nt(qc[:, sl], mk[:, sl]) for sl in heads] for qc in q]
    p, inv_l = [], []
    for sc in s:
        pc, lc = [], []
        for sh in sc:
            e = jnp.exp2(sh - jnp.max(sh, axis=-1, keepdims=True))
            part = e[:, :LANES]
            for c in range(1, t // LANES):
                part = part + e[:, c * LANES:(c + 1) * LANES]
            pc.append(e.astype(BF16))
            lc.append(1.0 / jnp.sum(part, axis=-1, keepdims=True))
        p.append(pc)
        inv_l.append(lc)
    o = [jnp.concatenate([(_dot(pc[hd], mv[:, heads[hd]]) * lc[hd]).astype(BF16) for hd in range(X_HEADS)], axis=-1)
         for pc, lc in zip(p, inv_l)]
    acc = [_dot(oc, wo_ref[...]) for oc in o]
    for rows, ac in zip(chains, acc):
        _residual_out(ac, x_ref, gp_ref, gn_ref, xo_ref, ho_ref, rows)


def xattn_block(h, x, mem_k, mem_v, wq, wo, g_post, g_next, seq):
    m, d = x.shape
    tm = _pick(seq, (512, 256, 128, 64))
    per_b = seq // tm
    t, xw = mem_k.shape[1], mem_k.shape[2]
    full = lambda a: pl.BlockSpec(a.shape, lambda i: (0,) * a.ndim)
    row = lambda w: pl.BlockSpec((tm, w), lambda i: (i, 0))
    mem = pl.BlockSpec((1, t, xw), lambda i: (i // per_b, 0, 0))
    vec = pl.BlockSpec((1, d), lambda i: (0, 0))
    return pl.pallas_call(
        _xattn_kernel,
        grid=(m // tm,),
        in_specs=[row(d), full(wq), mem, mem, full(wo), row(d), vec, vec],
        out_specs=[row(d), row(d)],
        out_shape=[jax.ShapeDtypeStruct((m, d), F32), jax.ShapeDtypeStruct((m, d), BF16)],
        compiler_params=_cparams(("parallel",), 48),
        name="xattn_block",
    )(h, wq, mem_k, mem_v, wo, x, g_post.reshape(1, d), g_next.reshape(1, d))


def _mlp_kernel(h_ref, w1_ref, w2_ref, x_ref, gp_ref, gn_ref, xo_ref, *rest, emit_h):
    ho_ref, acc_scr = (rest[0], rest[1]) if emit_h else (None, rest[0])
    f = pl.program_id(1)
    @pl.when(f == 0)
    def _():
        acc_scr[...] = jnp.zeros(acc_scr.shape, F32)

    a = jnp.maximum(_dot(h_ref[...], w1_ref[...]), 0.0)
    acc_scr[...] += _dot((a * a).astype(BF16), w2_ref[...])

    @pl.when(f == pl.num_programs(1) - 1)
    def _():
        _residual_out(acc_scr[...], x_ref, gp_ref, gn_ref, xo_ref, ho_ref)


def mlp_block(h, x, w1, w2, layer, g_post, g_next):
    m, d = x.shape
    ff = w1.shape[2]
    tm = _pick(m, (512, 256, 128, 64))
    tf = _pick(ff, (1024, 512, 256, 128))
    emit_h = g_next is not None
    row = pl.BlockSpec((tm, d), lambda i, f: (i, 0))
    vec = pl.BlockSpec((1, d), lambda i, f: (0, 0))
    out_specs = [row] + ([row] if emit_h else [])
    out_shape = [jax.ShapeDtypeStruct((m, d), F32)] + ([jax.ShapeDtypeStruct((m, d), BF16)] if emit_h else [])
    g_n = (g_next if emit_h else g_post).reshape(1, d)
    res = pl.pallas_call(
        functools.partial(_mlp_kernel, emit_h=emit_h),
        grid=(m // tm, ff // tf),
        in_specs=[row, pl.BlockSpec((None, d, tf), lambda i, f: (layer, 0, f)),
                  pl.BlockSpec((None, tf, d), lambda i, f: (layer, f, 0)), row, vec, vec],
        out_specs=out_specs,
        out_shape=out_shape,
        scratch_shapes=[pltpu.VMEM((tm, d), F32)],
        compiler_params=_cparams(("parallel", "arbitrary"), 56),
        name="mlp_block",
    )(h, w1, w2, x, g_post.reshape(1, d), g_n)
    return (res[0], res[1]) if emit_h else (res[0], None)


def _rope_tables(pos):
    def angles(half):
        inv = jnp.power(ROPE_BASE, -jnp.arange(half, dtype=F32) / half)
        return pos.astype(F32)[:, None] * inv[None, :]
    a = angles(RET_DK // 2)
    c, s = jnp.cos(a), jnp.sin(a)
    t128 = (jnp.concatenate([c, c], axis=1), jnp.concatenate([-s, s], axis=1))
    a = angles(MLA_ROPE // 2)
    c, s = jnp.cos(a), jnp.sin(a)
    z = jnp.zeros((pos.shape[0], LANES - MLA_ROPE), F32)
    t64 = (jnp.concatenate([c, c, z], axis=1), jnp.concatenate([s, s, z], axis=1))
    return t128, t64


def _prep_weights(w_in_even, w_uq, w_ukv, w_out_even, w_in_odd, b_gates_odd, w_out_odd, w_xq, w_xo, w_mlp1, w_mlp2):
    d = w_in_even.shape[1]
    ret_w = RET_HEADS * RET_DK
    n_qkvg = 2 * ret_w + 2 * RET_HEADS * RET_DV
    n_lat = MLA_Q_LORA + MLA_KV_LORA + MLA_ROPE
    even = []
    for e in range(w_in_even.shape[0]):
        w = w_in_even[e].astype(BF16)
        wlat = jnp.pad(w[:, n_qkvg:n_qkvg + n_lat], ((0, 0), (0, LANES - MLA_ROPE)))
        uq = w_uq[e].reshape(MLA_Q_LORA, MLA_HEADS, MLA_NOPE + MLA_ROPE)
        uq_nope = uq[:, :, :MLA_NOPE].reshape(MLA_Q_LORA, MLA_HEADS * MLA_NOPE)
        uq_rope = jnp.pad(uq[:, :, MLA_NOPE:], ((0, 0), (0, 0), (0, LANES - MLA_ROPE)))
        uq_rope = uq_rope.reshape(MLA_Q_LORA, MLA_HEADS * LANES)
        ukv = w_ukv[e].reshape(MLA_KV_LORA, MLA_HEADS, MLA_NOPE + MLA_V)
        even.append(dict(
            w_in=w,
            w_lat=wlat,
            w_uq=jnp.concatenate([uq_nope, uq_rope], axis=1).astype(BF16),
            w_uk=ukv[:, :, :MLA_NOPE].reshape(MLA_KV_LORA, MLA_HEADS * MLA_NOPE).astype(BF16),
            w_uv=ukv[:, :, MLA_NOPE:].reshape(MLA_KV_LORA, MLA_HEADS * MLA_V).astype(BF16),
            w_out=w_out_even[e].astype(BF16),
        ))
    m_qk = M_HEADS * M_DK
    n_qkvo = 2 * m_qk + 2 * M_HEADS * M_DV
    odd = []
    for o in range(w_in_odd.shape[0]):
        w = w_in_odd[o].astype(BF16)
        odd.append(dict(
            w_in=w,
            w_gates=jnp.pad(w[:, n_qkvo:], ((0, 0), (0, LANES - 2 * M_HEADS))),
            b_gates=jnp.pad(b_gates_odd[o], (0, LANES - 2 * M_HEADS)),
            w_out=w_out_odd[o].astype(BF16),
        ))
    q_scale = jnp.concatenate([jnp.full((m_qk,), M_DK ** -0.5, F32), jnp.ones((n_qkvo - m_qk,), F32)])
    rk_scale = jnp.concatenate([jnp.ones((ret_w,), F32), jnp.full((ret_w,), RET_DK ** -0.5, F32)])
    shared = dict(w_xq=w_xq.astype(BF16), w_xo=w_xo.astype(BF16), w_mlp1=w_mlp1.astype(BF16),
                  w_mlp2=w_mlp2.astype(BF16), q_scale=q_scale, rk_scale=rk_scale, d=d,
                  n_qk=2 * ret_w, n_vg=n_qkvg - 2 * ret_w, n_qkvo=n_qkvo)
    return even, odd, shared


def _run_trunk(x, pos, mem_k, mem_v, ret_s0, c0, n0, m0, past, norm_g, gains, even_w, odd_w, shared):
    b, s, d = x.shape
    m = b * s
    depth = norm_g.shape[0]
    t128, t64 = _rope_tables(pos)
    if m % s or s % 64:
        raise ValueError("unsupported sequence length")
    if s < MIN_ROPE_TABLE_ROWS:
        t128 = tuple(jnp.tile(t, (b, 1)) for t in t128)
        t64 = tuple(jnp.tile(t, (b, 1)) for t in t64)
    xf = x.reshape(m, d)
    h = rms_rows(xf, norm_g[0, 0], BF16)
    n_even = len(even_w)
    ckv_slab = None
    krs, rets, cs, ns, ms = [], [], [], [], []
    for layer in range(depth):
        g = norm_g[layer]
        if layer % 2 == 0:
            e = layer // 2
            w = even_w[e]
            qk = matmul(h, w["w_in"], BF16, n_cols=shared["n_qk"], col_scale=shared["rk_scale"], rope=t128)
            vg = matmul(h, w["w_in"], BF16, col0=shared["n_qk"], n_cols=shared["n_vg"])
            ret, ret_s = retention(qk.reshape(b, s, -1), vg.reshape(b, s, -1), ret_s0[e], gains["ret_gn_g"][e])
            qcat, kcat, v, ckv_slab, kr = even_latent(h, w["w_lat"], gains["mla_q_norm_g"][e].reshape(1, -1),
                                                      gains["mla_kv_norm_g"][e].reshape(1, -1),
                                                      w["w_uq"], w["w_uk"], w["w_uv"], t64, e, n_even, ckv_slab)
            kcat = kcat.reshape(b, s, -1)
            v = v.reshape(b, s, -1)
            if past is None:
                q_off, sk = 0, s
            else:
                p_ckv, p_krp = past
                p_len = p_ckv.shape[2]
                q_off, sk = p_len, p_len + s
                pad = (-sk) % KV_PAD_MULTIPLE
                pk, pv = past_kv(p_ckv, p_krp, e, w["w_uk"], w["w_uv"], sk + pad)
                tail = lambda new: jnp.concatenate([new, jnp.zeros((b, pad, new.shape[-1]), BF16)], axis=1)
                kcat = lax.dynamic_update_slice(pk, tail(kcat), (0, p_len, 0))
                v = lax.dynamic_update_slice(pv, tail(v), (0, p_len, 0))
            att = mla_attention(qcat.reshape(b, s, -1), kcat, v, q_off=q_off, sk_valid=sk)
            xf, h = outproj([ret.reshape(m, -1), att.reshape(m, -1)], w["w_out"], xf, g[1], g[2])
            rets.append(ret_s)
            krs.append(kr.reshape(b, s, -1))
        else:
            o = layer // 2
            w = odd_w[o]
            qkvo = matmul(h, w["w_in"], BF16, n_cols=shared["n_qkvo"], col_scale=shared["q_scale"])
            gates = matmul(h, w["w_gates"], F32, col_bias=w["b_gates"])
            hh, c, n, mm = mlstm(qkvo.reshape(b, s, -1), gates.reshape(b, s, -1), c0[o], n0[o], m0[o],
                                 gains["mlstm_norm_g"][o])
            xf, h = outproj([hh.reshape(m, -1)], w["w_out"], xf, g[1], g[2])
            cs.append(c)
            ns.append(n)
            ms.append(mm)
        xf, h = xattn_block(h, xf, mem_k[layer], mem_v[layer], shared["w_xq"][layer], shared["w_xo"][layer],
                            g[3], g[4], s)
        g_next = norm_g[layer + 1, 0] if layer + 1 < depth else None
        xf, h = mlp_block(h, xf, shared["w_mlp1"], shared["w_mlp2"], layer, g[5], g_next)
    return (xf.reshape(b, s, d), ckv_slab.reshape(n_even, b, s, -1), jnp.stack(krs), jnp.stack(rets),
            jnp.stack(cs), jnp.stack(ns), jnp.stack(ms))


def kernel(x_prompt, x_sample, cache_mla_ckv, cache_mla_krope, state_ret, state_mlstm_C, state_mlstm_n, state_mlstm_m, cache_mem_k, cache_mem_v, mem_prompt, norm_g, mem_norm_g, w_in_even, mla_q_norm_g, mla_kv_norm_g, w_uq, w_ukv, ret_gn_g, w_out_even, w_in_odd, b_gates_odd, mlstm_norm_g, w_out_odd, w_xq, w_xk, w_xv, w_xo, w_mlp1, w_mlp2):
    even_w, odd_w, shared = _prep_weights(w_in_even, w_uq, w_ukv, w_out_even, w_in_odd, b_gates_odd, w_out_odd,
                                          w_xq, w_xo, w_mlp1, w_mlp2)
    gains = dict(ret_gn_g=ret_gn_g, mla_q_norm_g=mla_q_norm_g, mla_kv_norm_g=mla_kv_norm_g,
                 mlstm_norm_g=mlstm_norm_g)
    depth = norm_g.shape[0]
    b, s, d = x_prompt.shape
    n_even, n_odd = w_in_even.shape[0], w_in_odd.shape[0]
    xw = X_HEADS * X_DH

    bm, t, _ = mem_prompt.shape
    mem_flat = mem_prompt.reshape(bm * t, d)
    p_mem_k, p_mem_v = [], []
    for layer in range(depth):
        mn = rms_rows(mem_flat, mem_norm_g[layer], BF16)
        kv = matmul(mn, jnp.concatenate([w_xk[layer], w_xv[layer]], axis=1).astype(BF16), F32)
        p_mem_k.append(kv[:, :xw].reshape(bm, t, X_HEADS, X_DH))
        p_mem_v.append(kv[:, xw:].reshape(bm, t, X_HEADS, X_DH))
    p_mem_k, p_mem_v = jnp.stack(p_mem_k), jnp.stack(p_mem_v)

    pos_p = jnp.arange(s, dtype=jnp.int32)
    zeros = lambda *shape: jnp.zeros(shape, F32)
    y_prompt, p_ckv, p_kr, p_ret, p_c, p_n, p_m = _run_trunk(
        x_prompt, pos_p, p_mem_k.reshape(depth, bm, t, xw).astype(BF16), p_mem_v.reshape(depth, bm, t, xw).astype(BF16),
        zeros(n_even, b, RET_HEADS, RET_DK, RET_DV), zeros(n_odd, b, M_HEADS, M_DV, M_DK),
        zeros(n_odd, b, M_HEADS, M_DK), zeros(n_odd, b, M_HEADS), None, norm_g, gains, even_w, odd_w, shared)

    db, ds, _ = x_sample.shape
    past_len = cache_mla_ckv.shape[2]
    pos_s = past_len + jnp.arange(ds, dtype=jnp.int32)
    past = (cache_mla_ckv, jnp.pad(cache_mla_krope, ((0, 0), (0, 0), (0, 0), (0, LANES - MLA_ROPE))))
    y_sample, s_ckv, s_kr, s_ret, s_c, s_n, s_m = _run_trunk(
        x_sample, pos_s, cache_mem_k.reshape(depth, db, -1, xw).astype(BF16),
        cache_mem_v.reshape(depth, db, -1, xw).astype(BF16),
        state_ret, state_mlstm_C, state_mlstm_n, state_mlstm_m, past, norm_g, gains, even_w, odd_w, shared)

    return (y_prompt, y_sample, p_ckv, p_kr, p_ret, p_c, p_n, p_m, p_mem_k, p_mem_v,
            s_ckv, s_kr, s_ret, s_c, s_n, s_m)
```

```python
import functools

import jax
import jax.numpy as jnp
from jax import lax
from jax.experimental import pallas as pl
from jax.experimental.pallas import tpu as pltpu

F32 = jnp.float32
BF16 = jnp.bfloat16

RMS_EPS = 1e-6
ROPE_BASE = 10000.0
CHUNK = 64
CHUNK_SHIFT = 6
assert 1 << CHUNK_SHIFT == CHUNK

LOG2E = 1.4426950408889634
RET_HEADS, RET_DK, RET_DV = 8, 128, 128
MLA_HEADS, MLA_NOPE, MLA_ROPE, MLA_V = 8, 128, 64, 128
MLA_Q_LORA, MLA_KV_LORA = 768, 512
MLA_Q_PRESCALE = (MLA_NOPE + MLA_ROPE) ** -0.5 * LOG2E
M_HEADS, M_DK, M_DV = 8, 128, 256
X_HEADS, X_DH = 4, 128
X_Q_PRESCALE = X_DH ** -0.5 * LOG2E

LANES = 128
MXU_DIM = 256
V7X_VMEM_BYTES = 64 * 1024 * 1024
MIB = 1024 * 1024

MLA_QK_PAD = MXU_DIM
KV_PAD_MULTIPLE = MXU_DIM
KV_UNROLL = 2
MLA_HEADS_PER_STEP = 2
ATTN_SINGLE_STEP_ELEMS = 512 * 512
HEAD_GROUPS = (range(0, 4), range(4, 8))
ROW_CHAINS = 2
MIN_ROPE_TABLE_ROWS = 256


def _cparams(semantics, vmem_mib):
    assert vmem_mib * MIB < V7X_VMEM_BYTES
    return pltpu.CompilerParams(dimension_semantics=semantics, vmem_limit_bytes=vmem_mib * MIB)


def _pick(n, cands):
    for c in cands:
        if c <= n and n % c == 0:
            return c
    raise ValueError(f"no tile for {n} in {cands}")


def _rms(x, g):
    ms = jnp.mean(x * x, axis=-1, keepdims=True)
    return x * lax.rsqrt(ms + RMS_EPS) * g


def _sigmoid(x):
    return 1.0 / (1.0 + jnp.exp(-x))


def _dot(a, b):
    return jnp.dot(a, b, preferred_element_type=F32)


def _dot_nt(a, b):
    return lax.dot_general(a, b, (((1,), (1,)), ((), ())), preferred_element_type=F32)


def _dot_tn(a, b):
    return lax.dot_general(a, b, (((0,), (0,)), ((), ())), preferred_element_type=F32)


def _rope128(x, cos2, sin2):
    return x * cos2 + pltpu.roll(x, 64, 1) * sin2


def _rope64(x, cos_p, sin_p):
    return x * cos_p + (pltpu.roll(x, 32, 1) - pltpu.roll(x, 96, 1)) * sin_p


def _norm_kernel(x_ref, g_ref, o_ref):
    o_ref[...] = _rms(x_ref[...].astype(F32), g_ref[...]).astype(o_ref.dtype)


def rms_rows(x, g, out_dtype):
    m, k = x.shape
    tm = _pick(m, (512, 256, 128, 64, 32, 16, 8))
    return pl.pallas_call(
        _norm_kernel,
        grid=(m // tm,),
        in_specs=[pl.BlockSpec((tm, k), lambda i: (i, 0)), pl.BlockSpec((1, k), lambda i: (0, 0))],
        out_specs=pl.BlockSpec((tm, k), lambda i: (i, 0)),
        out_shape=jax.ShapeDtypeStruct((m, k), out_dtype),
        compiler_params=_cparams(("parallel",), 32),
        name="rms_rows",
    )(x, g.reshape(1, k).astype(F32))


def _mm_kernel(*refs, has_scale, has_bias, n_rope, tn):
    it = iter(refs)
    h_ref, w_ref = next(it), next(it)
    scale_ref = next(it) if has_scale else None
    bias_ref = next(it) if has_bias else None
    cos_ref, sin_ref = (next(it), next(it)) if n_rope else (None, None)
    o_ref = next(it)
    acc = _dot(h_ref[...], w_ref[...])
    if has_scale:
        acc = acc * scale_ref[...]
    if has_bias:
        acc = acc + bias_ref[...]
    if n_rope:
        c, s = cos_ref[...], sin_ref[...]
        for t in range(tn // LANES):
            sl = slice(t * LANES, (t + 1) * LANES)
            o_ref[:, sl] = _rope128(acc[:, sl], c, s).astype(o_ref.dtype)
    else:
        o_ref[...] = acc.astype(o_ref.dtype)


def matmul(h, w, out_dtype, *, col0=0, n_cols=None, col_scale=None, col_bias=None, rope=None):
    m, k = h.shape
    n = w.shape[1] - col0 if n_cols is None else n_cols
    tm = _pick(m, (1024, 512, 256, 128, 64))
    tn = _pick(n, (1024, 512, 256, 128))
    assert col0 % tn == 0 and col0 + n <= w.shape[1]
    jb = col0 // tn
    args = [h, w]
    in_specs = [pl.BlockSpec((tm, k), lambda i, j: (i, 0)), pl.BlockSpec((k, tn), lambda i, j: (0, j + jb))]
    for vec in (col_scale, col_bias):
        if vec is not None:
            args.append(vec.reshape(1, n).astype(F32))
            in_specs.append(pl.BlockSpec((1, tn), lambda i, j: (0, j)))
    if rope is not None:
        r = rope[0].shape[0]
        tm = _pick(m, tuple(c for c in (1024, 512, 256, 128, 64) if r % c == 0))
        in_specs[0] = pl.BlockSpec((tm, k), lambda i, j: (i, 0))
        nrb = r // tm
        for tab in rope:
            args.append(tab)
            in_specs.append(pl.BlockSpec((tm, LANES), lambda i, j: (i % nrb, 0)))
    kern = functools.partial(_mm_kernel, has_scale=col_scale is not None, has_bias=col_bias is not None,
                             n_rope=rope is not None, tn=tn)
    return pl.pallas_call(
        kern,
        grid=(m // tm, n // tn),
        in_specs=in_specs,
        out_specs=pl.BlockSpec((tm, tn), lambda i, j: (i, j)),
        out_shape=jax.ShapeDtypeStruct((m, n), out_dtype),
        compiler_params=_cparams(("parallel", "arbitrary"), 40),
        name="matmul",
    )(*args)


def _kv_expand(ckvn_bf16, krr_bf16, wuk_ref, wuv_ref, kcat_ref, v_ref):
    kn = _dot(ckvn_bf16, wuk_ref[...])
    for hd in range(MLA_HEADS):
        kcat_ref[:, hd * MLA_QK_PAD: hd * MLA_QK_PAD + MLA_NOPE] = (
            kn[:, hd * MLA_NOPE:(hd + 1) * MLA_NOPE].astype(BF16))
        kcat_ref[:, hd * MLA_QK_PAD + MLA_NOPE:(hd + 1) * MLA_QK_PAD] = krr_bf16
    v_ref[...] = _dot(ckvn_bf16, wuv_ref[...]).astype(BF16)


def _even_latent_kernel(h_ref, wlat_ref, gq_ref, gkv_ref, wuq_ref, wuk_ref, wuv_ref, cos_ref, sin_ref, *refs):
    qcat_ref, kcat_ref, v_ref, ckv_ref, kr_ref = refs[-5:]
    lat = _dot(h_ref[...], wlat_ref[...])
    cq = lat[:, :MLA_Q_LORA]
    ckv = lat[:, MLA_Q_LORA:MLA_Q_LORA + MLA_KV_LORA]
    krp = lat[:, MLA_Q_LORA + MLA_KV_LORA:]
    c, s = cos_ref[...], sin_ref[...]
    q = _dot(_rms(cq, gq_ref[...]).astype(BF16), wuq_ref[...]) * MLA_Q_PRESCALE
    nope_w = MLA_HEADS * MLA_NOPE
    for hd in range(MLA_HEADS):
        qcat_ref[:, hd * MLA_QK_PAD: hd * MLA_QK_PAD + MLA_NOPE] = (
            q[:, hd * MLA_NOPE:(hd + 1) * MLA_NOPE].astype(BF16))
        qr = q[:, nope_w + hd * LANES: nope_w + (hd + 1) * LANES]
        qcat_ref[:, hd * MLA_QK_PAD + MLA_NOPE:(hd + 1) * MLA_QK_PAD] = _rope64(qr, c, s).astype(BF16)
    ckvn = _rms(ckv, gkv_ref[...])
    ckv_ref[...] = ckvn
    krr = _rope64(krp, c, s)
    kr_ref[...] = krr[:, :MLA_ROPE]
    _kv_expand(ckvn.astype(BF16), krr.astype(BF16), wuk_ref, wuv_ref, kcat_ref, v_ref)


def even_latent(h, wlat, gq, gkv, wuq, wuk, wuv, rope64, e, n_even, ckv_slab):
    m, d = h.shape
    r = rope64[0].shape[0]
    tm = _pick(m, tuple(c for c in (512, 256, 128, 64) if r % c == 0))
    nrb = r // tm
    full = lambda a: pl.BlockSpec(a.shape, lambda i: (0,) * a.ndim)
    row = lambda w: pl.BlockSpec((tm, w), lambda i: (i, 0))
    tab = pl.BlockSpec((tm, LANES), lambda i: (i % nrb, 0))
    qk_w = MLA_HEADS * MLA_QK_PAD
    v_w = MLA_HEADS * MLA_V
    args = [h, wlat, gq, gkv, wuq, wuk, wuv, rope64[0], rope64[1]]
    in_specs = [row(d), full(wlat), full(gq), full(gkv), full(wuq), full(wuk), full(wuv), tab, tab]
    aliases = {}
    if ckv_slab is not None:
        aliases = {len(args): 3}
        args.append(ckv_slab)
        in_specs.append(pl.BlockSpec(memory_space=pl.ANY))
    return pl.pallas_call(
        _even_latent_kernel,
        grid=(m // tm,),
        in_specs=in_specs,
        out_specs=[row(qk_w), row(qk_w), row(v_w),
                   pl.BlockSpec((None, tm, MLA_KV_LORA), lambda i: (e, i, 0)), row(MLA_ROPE)],
        out_shape=[jax.ShapeDtypeStruct((m, qk_w), BF16), jax.ShapeDtypeStruct((m, qk_w), BF16),
                   jax.ShapeDtypeStruct((m, v_w), BF16), jax.ShapeDtypeStruct((n_even, m, MLA_KV_LORA), F32),
                   jax.ShapeDtypeStruct((m, MLA_ROPE), F32)],
        input_output_aliases=aliases,
        compiler_params=_cparams(("parallel",), 48),
        name="even_latent",
    )(*args)


def _past_kv_kernel(ckv_ref, krp_ref, wuk_ref, wuv_ref, kcat_ref, v_ref):
    _kv_expand(ckv_ref[...].astype(BF16), krp_ref[...].astype(BF16), wuk_ref, wuv_ref, kcat_ref, v_ref)


def past_kv(ckvn, kr_pad, e, wuk, wuv, skp):
    _, b, p_len, _ = ckvn.shape
    tm = _pick(p_len, (512, 256, 128, 64))
    full = lambda a: pl.BlockSpec(a.shape, lambda bi, i: (0,) * a.ndim)
    src = lambda w: pl.BlockSpec((None, None, tm, w), lambda bi, i: (e, bi, i, 0))
    dst = lambda w: pl.BlockSpec((None, tm, w), lambda bi, i: (bi, i, 0))
    qk_w = MLA_HEADS * MLA_QK_PAD
    v_w = MLA_HEADS * MLA_V
    return pl.pallas_call(
        _past_kv_kernel,
        grid=(b, p_len // tm),
        in_specs=[src(MLA_KV_LORA), src(LANES), full(wuk), full(wuv)],
        out_specs=[dst(qk_w), dst(v_w)],
        out_shape=[jax.ShapeDtypeStruct((b, skp, qk_w), BF16), jax.ShapeDtypeStruct((b, skp, v_w), BF16)],
        compiler_params=_cparams(("parallel", "parallel"), 32),
        name="past_kv",
    )(ckvn, kr_pad, wuk, wuv)


def _mla_kernel(q_ref, k_ref, v_ref, o_ref, m_scr, l_scr, acc_scr, a_scr, p_scr, *, tq, tk, hb, q_off, sk_valid):
    qi = pl.program_id(2)
    q0 = qi * tq
    cq_lo = (q_off + q0) // CHUNK
    cq_hi = (q_off + q0 + tq - 1) // CHUNK
    n_full = jnp.minimum(lax.div((cq_lo + 1) * CHUNK, tk), sk_valid // tk)
    n_vis = lax.div(jnp.minimum((cq_hi + 1) * CHUNK, sk_valid) + tk - 1, tk)
    m_scr[...] = jnp.full(m_scr.shape, -1e30, F32)
    l_scr[...] = jnp.zeros(l_scr.shape, F32)
    acc_scr[...] = jnp.zeros(acc_scr.shape, F32)
    a_scr[...] = jnp.ones(a_scr.shape, F32)
    p_scr[...] = jnp.zeros(p_scr.shape, BF16)

    def apply_prev(j, ki_prev):
        v = v_ref[0, pl.ds(pl.multiple_of(ki_prev * tk, tk), tk), j * MLA_V:(j + 1) * MLA_V]
        acc_scr[j] = a_scr[j] * acc_scr[j] + _dot(p_scr[j], v)

    def step(ki, masked):
        k0 = pl.multiple_of(ki * tk, tk)
        ki_prev = jnp.maximum(ki - 1, 0)
        if masked:
            qpos = q_off + q0 + lax.broadcasted_iota(jnp.int32, (tq, tk), 0)
            kpos = k0 + lax.broadcasted_iota(jnp.int32, (tq, tk), 1)
            vis = jnp.logical_and((kpos >> CHUNK_SHIFT) <= (qpos >> CHUNK_SHIFT), kpos < sk_valid)
        for j in range(hb):
            q = q_ref[0, :, j * MLA_QK_PAD:(j + 1) * MLA_QK_PAD]
            k = k_ref[0, pl.ds(k0, tk), j * MLA_QK_PAD:(j + 1) * MLA_QK_PAD]
            s = _dot_nt(q, k)
            apply_prev(j, ki_prev)
            if masked:
                s = jnp.where(vis, s, -jnp.inf)
            m_prev = m_scr[j]
            m_new = jnp.maximum(m_prev, jnp.max(s, axis=-1, keepdims=True))
            alpha = jnp.exp2(m_prev - m_new)
            p = [jnp.exp2(s[:, c * LANES:(c + 1) * LANES] - m_new) for c in range(tk // LANES)]
            l_scr[j] = alpha * l_scr[j] + sum(p[1:], p[0])
            p_scr[j] = jnp.concatenate([pc.astype(BF16) for pc in p], axis=-1)
            a_scr[j] = alpha
            m_scr[j] = m_new

    def body_full(ki, carry):
        step(ki, False)
        return carry

    def body_masked(ki, carry):
        step(ki, True)
        return carry

    def body_full_group(kg, carry):
        for u in range(KV_UNROLL):
            step(KV_UNROLL * kg + u, False)
        return carry

    n_groups = lax.div(n_full, KV_UNROLL)
    lax.fori_loop(0, n_groups, body_full_group, 0)
    lax.fori_loop(KV_UNROLL * n_groups, n_full, body_full, 0)
    lax.fori_loop(n_full, n_vis, body_masked, 0)
    for j in range(hb):
        apply_prev(j, n_vis - 1)
        l = jnp.sum(l_scr[j], axis=-1, keepdims=True)
        o_ref[0, :, j * MLA_V:(j + 1) * MLA_V] = (acc_scr[j] / l).astype(o_ref.dtype)


def mla_attention(qcat, kcat, v, *, q_off, sk_valid):
    b, sq, _ = qcat.shape
    skp = kcat.shape[1]
    tq = _pick(sq, (512, 256, 128, 64))
    tk = skp if tq * skp <= ATTN_SINGLE_STEP_ELEMS else _pick(skp, (512, 256, 128))
    hb = MLA_HEADS if tk == skp else MLA_HEADS_PER_STEP
    kern = functools.partial(_mla_kernel, tq=tq, tk=tk, hb=hb, q_off=q_off, sk_valid=sk_valid)
    return pl.pallas_call(
        kern,
        grid=(b, MLA_HEADS // hb, sq // tq),
        in_specs=[pl.BlockSpec((1, tq, hb * MLA_QK_PAD), lambda bi, h, qi: (bi, qi, h)),
                  pl.BlockSpec((1, skp, hb * MLA_QK_PAD), lambda bi, h, qi: (bi, 0, h)),
                  pl.BlockSpec((1, skp, hb * MLA_V), lambda bi, h, qi: (bi, 0, h))],
        out_specs=pl.BlockSpec((1, tq, hb * MLA_V), lambda bi, h, qi: (bi, qi, h)),
        out_shape=jax.ShapeDtypeStruct((b, sq, MLA_HEADS * MLA_V), BF16),
        scratch_shapes=[pltpu.VMEM((hb, tq, LANES), F32), pltpu.VMEM((hb, tq, LANES), F32),
                        pltpu.VMEM((hb, tq, MLA_V), F32), pltpu.VMEM((hb, tq, LANES), F32),
                        pltpu.VMEM((hb, tq, tk), BF16)],
        compiler_params=_cparams(("parallel", "parallel", "arbitrary"), 48),
        name="mla_attention",
    )(qcat, kcat, v)


def _retention_kernel(lg_ref, qk_ref, vg_ref, s0_ref, gn_ref, out_ref, sfin_ref,
                      state_scr, decay_scr, qdec_scr, kdec_scr, *, lc):
    ci = pl.program_id(1)
    heads = range(RET_HEADS)

    @pl.when(ci == 0)
    def _():
        state_scr[...] = s0_ref[0]
        ti = lax.broadcasted_iota(jnp.int32, (lc, lc), 0)
        si = lax.broadcasted_iota(jnp.int32, (lc, lc), 1)
        dpos = jnp.maximum(ti - si, 0).astype(F32)
        trow = lax.broadcasted_iota(jnp.int32, (lc, LANES), 0).astype(F32)
        for h in heads:
            lg = lg_ref[h:h + 1, :]
            decay_scr[h] = jnp.where(ti >= si, jnp.exp(lg[:, 0:1] * dpos), 0.0)
            qdec_scr[h] = jnp.exp(lg * (trow + 1.0))
            kdec_scr[h] = jnp.exp(lg * (lc - 1.0 - trow))

    qk_w = RET_HEADS * RET_DK
    v_w = RET_HEADS * RET_DV
    q = [qk_ref[0, :, h * RET_DK:(h + 1) * RET_DK] for h in heads]
    k = [qk_ref[0, :, qk_w + h * RET_DK: qk_w + (h + 1) * RET_DK] for h in heads]
    v = [vg_ref[0, :, h * RET_DV:(h + 1) * RET_DV] for h in heads]
    st = [state_scr[h] for h in heads]
    sc = [_dot_nt(q[h], k[h]) for h in heads]
    inter = [_dot(q[h], st[h].astype(BF16)) for h in heads]
    scd = [(sc[h] * decay_scr[h]).astype(BF16) for h in heads]
    kd = [(k[h].astype(F32) * kdec_scr[h]).astype(BF16) for h in heads]
    intra = [_dot(scd[h], v[h]) for h in heads]
    upd = [_dot_tn(kd[h], v[h]) for h in heads]
    for h in heads:
        state_scr[h] = jnp.exp(lg_ref[h:h + 1, :] * lc) * st[h] + upd[h]
        out = intra[h] + qdec_scr[h] * inter[h]
        g = vg_ref[0, :, v_w + h * RET_DV: v_w + (h + 1) * RET_DV].astype(F32)
        xc = out - jnp.mean(out, axis=-1, keepdims=True)
        y = xc * lax.rsqrt(jnp.mean(xc * xc, axis=-1, keepdims=True) + RMS_EPS)
        y = y * gn_ref[:, h * RET_DV:(h + 1) * RET_DV]
        out_ref[0, :, h * RET_DV:(h + 1) * RET_DV] = (g * _sigmoid(g) * y).astype(out_ref.dtype)

    @pl.when(ci == pl.num_programs(1) - 1)
    def _():
        sfin_ref[0] = state_scr[...]


def retention(qk, vg, s0, gn_g):
    b, s, _ = qk.shape
    lc = _pick(s, (256, 128, 64))
    log_g = jnp.log1p(-jnp.exp2(-5.0 - jnp.arange(RET_HEADS, dtype=F32)))
    lg_tab = jnp.broadcast_to(log_g[:, None], (RET_HEADS, LANES))
    v_w = RET_HEADS * RET_DV
    kern = functools.partial(_retention_kernel, lc=lc)
    st_spec = pl.BlockSpec((1, RET_HEADS, RET_DK, RET_DV), lambda bi, ci: (bi, 0, 0, 0))
    return pl.pallas_call(
        kern,
        grid=(b, s // lc),
        in_specs=[pl.BlockSpec((RET_HEADS, LANES), lambda bi, ci: (0, 0)),
                  pl.BlockSpec((1, lc, qk.shape[2]), lambda bi, ci: (bi, ci, 0)),
                  pl.BlockSpec((1, lc, vg.shape[2]), lambda bi, ci: (bi, ci, 0)),
                  st_spec,
                  pl.BlockSpec((1, v_w), lambda bi, ci: (0, 0))],
        out_specs=[pl.BlockSpec((1, lc, v_w), lambda bi, ci: (bi, ci, 0)), st_spec],
        out_shape=[jax.ShapeDtypeStruct((b, s, v_w), BF16),
                   jax.ShapeDtypeStruct((b, RET_HEADS, RET_DK, RET_DV), F32)],
        scratch_shapes=[pltpu.VMEM((RET_HEADS, RET_DK, RET_DV), F32), pltpu.VMEM((RET_HEADS, lc, lc), F32),
                        pltpu.VMEM((RET_HEADS, lc, LANES), F32), pltpu.VMEM((RET_HEADS, lc, LANES), F32)],
        compiler_params=_cparams(("parallel", "arbitrary"), 40),
        name="retention",
    )(lg_tab, qk, vg, s0, gn_g.reshape(1, v_w).astype(F32))


def _split3(x):
    hi = x.astype(BF16)
    r1 = x - hi.astype(F32)
    mid = r1.astype(BF16)
    lo = (r1 - mid.astype(F32)).astype(BF16)
    return hi, mid, lo


def _log_sigmoid(x):
    return jnp.minimum(x, 0.0) - jnp.log(1.0 + jnp.exp(-jnp.abs(x)))


def _widen(x, width):
    if width < LANES:
        return x[:, :width]
    return x if width == LANES else jnp.concatenate([x] * (width // LANES), axis=1)


def _mlstm_kernel(qkvo_ref, gates_ref, gates_t_ref, c0_ref, n0_ref, m0_ref, ng_ref,
                  out_ref, cfin_ref, nfin_ref, mfin_ref, c_scr, n_scr, m_scr, *, lc):
    ci = pl.program_id(1)

    @pl.when(ci == 0)
    def _():
        c_scr[...] = c0_ref[0]
        n_scr[...] = n0_ref[0]
        m_scr[...] = m0_ref[0]

    gates = gates_ref[0]
    gates_t = gates_t_ref[0]
    ti = lax.broadcasted_iota(jnp.int32, (lc, lc), 0)
    si = lax.broadcasted_iota(jnp.int32, (lc, lc), 1)
    tril = si <= ti
    ones_l = jnp.where(tril, 1.0, 0.0).astype(BF16)
    ones_u = jnp.where(ti <= si, 1.0, 0.0).astype(BF16)
    b_col = sum(_dot(ones_l, part) for part in _split3(_log_sigmoid(gates)))
    b_row = sum(_dot(part, ones_u) for part in _split3(_log_sigmoid(gates_t)))[M_HEADS:2 * M_HEADS]
    i_row = gates_t[0:M_HEADS]
    qk_w = M_HEADS * M_DK
    v_w = M_HEADS * M_DV
    for heads in HEAD_GROUPS:
        q = {h: qkvo_ref[0, :, h * M_DK:(h + 1) * M_DK] for h in heads}
        k = {h: qkvo_ref[0, :, qk_w + h * M_DK: qk_w + (h + 1) * M_DK] for h in heads}
        v = {h: qkvo_ref[0, :, 2 * qk_w + h * M_DV: 2 * qk_w + (h + 1) * M_DV] for h in heads}
        c_st = {h: c_scr[h] for h in heads}
        n_st = {h: n_scr[h:h + 1, :] for h in heads}
        m_prev = {h: m_scr[h:h + 1, :] for h in heads}
        qk = {h: _dot_nt(q[h], k[h]) for h in heads}
        qc = {h: _dot(q[h], c_st[h].astype(BF16)) for h in heads}
        bt = {h: jnp.broadcast_to(b_col[:, M_HEADS + h:M_HEADS + h + 1], (lc, LANES)) for h in heads}
        i_t = {h: jnp.broadcast_to(gates[:, h:h + 1], (lc, LANES)) for h in heads}
        m_t, w_inter, sc = {}, {}, {}
        for h in heads:
            log_intra = jnp.where(tril, _widen(bt[h], lc) - b_row[h:h + 1, :] + i_row[h:h + 1, :], -jnp.inf)
            log_inter = bt[h] + m_prev[h]
            m_t[h] = jnp.maximum(log_inter, jnp.max(log_intra, axis=-1, keepdims=True))
            w_inter[h] = jnp.exp(log_inter - m_t[h])
            sc[h] = qk[h] * jnp.exp(log_intra - _widen(m_t[h], lc))
        num_intra = {h: _dot(sc[h].astype(BF16), v[h]) for h in heads}
        kw = {}
        for h in heads:
            m_new = m_t[h][lc - 1:lc, :]
            b_last = bt[h][lc - 1:lc, :]
            w_state = jnp.exp(b_last + m_prev[h] - m_new)
            kw[h] = k[h].astype(F32) * jnp.exp(b_last - bt[h] + i_t[h] - m_new)
            n_scr[h:h + 1, :] = w_state * n_st[h] + jnp.sum(kw[h], axis=0, keepdims=True)
            m_scr[h:h + 1, :] = m_new
            c_scr[h] = _widen(w_state, M_DV) * c_st[h] + _dot_tn(kw[h].astype(BF16), v[h])
        for h in heads:
            num = num_intra[h] + _widen(w_inter[h], M_DV) * qc[h]
            den = (jnp.sum(sc[h], axis=-1, keepdims=True)
                   + w_inter[h] * jnp.sum(q[h].astype(F32) * n_st[h], axis=-1, keepdims=True))
            hh = num * _widen(1.0 / jnp.maximum(jnp.abs(den), jnp.exp(-m_t[h])), M_DV)
            y = hh * lax.rsqrt(jnp.mean(hh * hh, axis=-1, keepdims=True) + RMS_EPS)
            y = y * ng_ref[:, h * M_DV:(h + 1) * M_DV]
            og = qkvo_ref[0, :, 2 * qk_w + v_w + h * M_DV: 2 * qk_w + v_w + (h + 1) * M_DV].astype(F32)
            out_ref[0, :, h * M_DV:(h + 1) * M_DV] = (y * _sigmoid(og)).astype(out_ref.dtype)

    @pl.when(ci == pl.num_programs(1) - 1)
    def _():
        cfin_ref[0] = c_scr[...]
        nfin_ref[0] = n_scr[...]
        mfin_ref[0] = m_scr[...]


def mlstm(qkvo, gates, c0, n0, m0, norm_g):
    b, s, w = qkvo.shape
    lc = _pick(s, (256, 128, 64))
    v_w = M_HEADS * M_DV
    gates_t = jnp.swapaxes(gates[:, :, :2 * M_HEADS], 1, 2)
    kern = functools.partial(_mlstm_kernel, lc=lc)
    c_spec = pl.BlockSpec((1, M_HEADS, M_DK, M_DV), lambda bi, ci: (bi, 0, 0, 0))
    n_spec = pl.BlockSpec((1, M_HEADS, M_DK), lambda bi, ci: (bi, 0, 0))
    m_spec = pl.BlockSpec((1, M_HEADS, LANES), lambda bi, ci: (bi, 0, 0))
    hh, c_t, n, m = pl.pallas_call(
        kern,
        grid=(b, s // lc),
        in_specs=[pl.BlockSpec((1, lc, w), lambda bi, ci: (bi, ci, 0)),
                  pl.BlockSpec((1, lc, LANES), lambda bi, ci: (bi, ci, 0)),
                  pl.BlockSpec((1, 2 * M_HEADS, lc), lambda bi, ci: (bi, 0, ci)),
                  c_spec, n_spec, m_spec,
                  pl.BlockSpec((1, v_w), lambda bi, ci: (0, 0))],
        out_specs=[pl.BlockSpec((1, lc, v_w), lambda bi, ci: (bi, ci, 0)), c_spec, n_spec, m_spec],
        out_shape=[jax.ShapeDtypeStruct((b, s, v_w), BF16),
                   jax.ShapeDtypeStruct((b, M_HEADS, M_DK, M_DV), F32),
                   jax.ShapeDtypeStruct((b, M_HEADS, M_DK), F32),
                   jax.ShapeDtypeStruct((b, M_HEADS, LANES), F32)],
        scratch_shapes=[pltpu.VMEM((M_HEADS, M_DK, M_DV), F32), pltpu.VMEM((M_HEADS, M_DK), F32),
                        pltpu.VMEM((M_HEADS, LANES), F32)],
        compiler_params=_cparams(("parallel", "arbitrary"), 48),
        name="mlstm",
    )(qkvo, gates, gates_t, jnp.swapaxes(c0, 2, 3), n0, jnp.broadcast_to(m0[..., None], m0.shape + (LANES,)),
      norm_g.reshape(1, v_w).astype(F32))
    return hh, jnp.swapaxes(c_t, 2, 3), n, m[..., 0]


def _residual_out(acc, x_ref, gp_ref, gn_ref, xo_ref, ho_ref, rows=slice(None)):
    xn = x_ref[rows, :] + _rms(acc, gp_ref[...])
    xo_ref[rows, :] = xn
    if ho_ref is not None:
        ho_ref[rows, :] = _rms(xn, gn_ref[...]).astype(ho_ref.dtype)


def _row_chains(tm):
    n = ROW_CHAINS if tm % (ROW_CHAINS * 128) == 0 else 1
    return [slice(c * (tm // n), (c + 1) * (tm // n)) for c in range(n)]


def _outproj_kernel(*refs, n_in):
    a_refs, w_refs = refs[:n_in], refs[n_in:2 * n_in]
    x_ref, gp_ref, gn_ref, xo_ref, ho_ref = refs[2 * n_in:]
    for rows in _row_chains(x_ref.shape[0]):
        acc = _dot(a_refs[0][rows, :], w_refs[0][...])
        for a_ref, w_ref in zip(a_refs[1:], w_refs[1:]):
            acc = acc + _dot(a_ref[rows, :], w_ref[...])
        _residual_out(acc, x_ref, gp_ref, gn_ref, xo_ref, ho_ref, rows)


def outproj(acts, w, x, g_post, g_next):
    m, d = x.shape
    tm = _pick(m, (512, 256, 128, 64))
    n_in = len(acts)
    row = lambda wd: pl.BlockSpec((tm, wd), lambda i: (i, 0))
    vec = pl.BlockSpec((1, d), lambda i: (0, 0))
    w_specs, r0 = [], 0
    for a in acts:
        ka = a.shape[1]
        assert r0 % ka == 0
        w_specs.append(pl.BlockSpec((ka, d), functools.partial(lambda blk, i: (blk, 0), r0 // ka)))
        r0 += ka
    assert r0 == w.shape[0]
    return pl.pallas_call(
        functools.partial(_outproj_kernel, n_in=n_in),
        grid=(m // tm,),
        in_specs=[row(a.shape[1]) for a in acts] + w_specs + [row(d), vec, vec],
        out_specs=[row(d), row(d)],
        out_shape=[jax.ShapeDtypeStruct((m, d), F32), jax.ShapeDtypeStruct((m, d), BF16)],
        compiler_params=_cparams(("parallel",), 56),
        name="outproj",
    )(*acts, *([w] * n_in), x, g_post.reshape(1, d), g_next.reshape(1, d))


def _xattn_kernel(h_ref, wq_ref, mk_ref, mv_ref, wo_ref, x_ref, gp_ref, gn_ref, xo_ref, ho_ref):
    mk, mv = mk_ref[0], mv_ref[0]
    t = mk.shape[0]
    chains = _row_chains(x_ref.shape[0])
    heads = [slice(hd * X_DH, (hd + 1) * X_DH) for hd in range(X_HEADS)]
    q = [(_dot(h_ref[rows, :], wq_ref[...]) * X_Q_PRESCALE).astype(BF16) for rows in chains]
    s = [[_dot_nt(qc[:, sl], mk[:, sl]) for sl in heads] for qc in q]
    p, inv_l = [], []
    for sc in s:
        pc, lc = [], []
        for sh in sc:
            e = jnp.exp2(sh - jnp.max(sh, axis=-1, keepdims=True))
            part = e[:, :LANES]
            for c in range(1, t // LANES):
                part = part + e[:, c * LANES:(c + 1) * LANES]
            pc.append(e.astype(BF16))
            lc.append(1.0 / jnp.sum(part, axis=-1, keepdims=True))
        p.append(pc)
        inv_l.append(lc)
    o = [jnp.concatenate([(_dot(pc[hd], mv[:, heads[hd]]) * lc[hd]).astype(BF16) for hd in range(X_HEADS)], axis=-1)
         for pc, lc in zip(p, inv_l)]
    acc = [_dot(oc, wo_ref[...]) for oc in o]
    for rows, ac in zip(chains, acc):
        _residual_out(ac, x_ref, gp_ref, gn_ref, xo_ref, ho_ref, rows)


def xattn_block(h, x, mem_k, mem_v, wq, wo, g_post, g_next, seq):
    m, d = x.shape
    tm = _pick(seq, (512, 256, 128, 64))
    per_b = seq // tm
    t, xw = mem_k.shape[1], mem_k.shape[2]
    full = lambda a: pl.BlockSpec(a.shape, lambda i: (0,) * a.ndim)
    row = lambda w: pl.BlockSpec((tm, w), lambda i: (i, 0))
    mem = pl.BlockSpec((1, t, xw), lambda i: (i // per_b, 0, 0))
    vec = pl.BlockSpec((1, d), lambda i: (0, 0))
    return pl.pallas_call(
        _xattn_kernel,
        grid=(m // tm,),
        in_specs=[row(d), full(wq), mem, mem, full(wo), row(d), vec, vec],
        out_specs=[row(d), row(d)],
        out_shape=[jax.ShapeDtypeStruct((m, d), F32), jax.ShapeDtypeStruct((m, d), BF16)],
        compiler_params=_cparams(("parallel",), 48),
        name="xattn_block",
    )(h, wq, mem_k, mem_v, wo, x, g_post.reshape(1, d), g_next.reshape(1, d))


def _mlp_kernel(h_ref, w1_ref, w2_ref, x_ref, gp_ref, gn_ref, xo_ref, *rest, emit_h):
    ho_ref, acc_scr = (rest[0], rest[1]) if emit_h else (None, rest[0])
    f = pl.program_id(1)
    @pl.when(f == 0)
    def _():
        acc_scr[...] = jnp.zeros(acc_scr.shape, F32)

    a = jnp.maximum(_dot(h_ref[...], w1_ref[...]), 0.0)
    acc_scr[...] += _dot((a * a).astype(BF16), w2_ref[...])

    @pl.when(f == pl.num_programs(1) - 1)
    def _():
        _residual_out(acc_scr[...], x_ref, gp_ref, gn_ref, xo_ref, ho_ref)


def mlp_block(h, x, w1, w2, layer, g_post, g_next):
    m, d = x.shape
    ff = w1.shape[2]
    tm = _pick(m, (512, 256, 128, 64))
    tf = _pick(ff, (1024, 512, 256, 128))
    emit_h = g_next is not None
    row = pl.BlockSpec((tm, d), lambda i, f: (i, 0))
    vec = pl.BlockSpec((1, d), lambda i, f: (0, 0))
    out_specs = [row] + ([row] if emit_h else [])
    out_shape = [jax.ShapeDtypeStruct((m, d), F32)] + ([jax.ShapeDtypeStruct((m, d), BF16)] if emit_h else [])
    g_n = (g_next if emit_h else g_post).reshape(1, d)
    res = pl.pallas_call(
        functools.partial(_mlp_kernel, emit_h=emit_h),
        grid=(m // tm, ff // tf),
        in_specs=[row, pl.BlockSpec((None, d, tf), lambda i, f: (layer, 0, f)),
                  pl.BlockSpec((None, tf, d), lambda i, f: (layer, f, 0)), row, vec, vec],
        out_specs=out_specs,
        out_shape=out_shape,
        scratch_shapes=[pltpu.VMEM((tm, d), F32)],
        compiler_params=_cparams(("parallel", "arbitrary"), 56),
        name="mlp_block",
    )(h, w1, w2, x, g_post.reshape(1, d), g_n)
    return (res[0], res[1]) if emit_h else (res[0], None)


def _rope_tables(pos):
    def angles(half):
        inv = jnp.power(ROPE_BASE, -jnp.arange(half, dtype=F32) / half)
        return pos.astype(F32)[:, None] * inv[None, :]
    a = angles(RET_DK // 2)
    c, s = jnp.cos(a), jnp.sin(a)
    t128 = (jnp.concatenate([c, c], axis=1), jnp.concatenate([-s, s], axis=1))
    a = angles(MLA_ROPE // 2)
    c, s = jnp.cos(a), jnp.sin(a)
    z = jnp.zeros((pos.shape[0], LANES - MLA_ROPE), F32)
    t64 = (jnp.concatenate([c, c, z], axis=1), jnp.concatenate([s, s, z], axis=1))
    return t128, t64


def _prep_weights(w_in_even, w_uq, w_ukv, w_out_even, w_in_odd, b_gates_odd, w_out_odd, w_xq, w_xo, w_mlp1, w_mlp2):
    d = w_in_even.shape[1]
    ret_w = RET_HEADS * RET_DK
    n_qkvg = 2 * ret_w + 2 * RET_HEADS * RET_DV
    n_lat = MLA_Q_LORA + MLA_KV_LORA + MLA_ROPE
    even = []
    for e in range(w_in_even.shape[0]):
        w = w_in_even[e].astype(BF16)
        wlat = jnp.pad(w[:, n_qkvg:n_qkvg + n_lat], ((0, 0), (0, LANES - MLA_ROPE)))
        uq = w_uq[e].reshape(MLA_Q_LORA, MLA_HEADS, MLA_NOPE + MLA_ROPE)
        uq_nope = uq[:, :, :MLA_NOPE].reshape(MLA_Q_LORA, MLA_HEADS * MLA_NOPE)
        uq_rope = jnp.pad(uq[:, :, MLA_NOPE:], ((0, 0), (0, 0), (0, LANES - MLA_ROPE)))
        uq_rope = uq_rope.reshape(MLA_Q_LORA, MLA_HEADS * LANES)
        ukv = w_ukv[e].reshape(MLA_KV_LORA, MLA_HEADS, MLA_NOPE + MLA_V)
        even.append(dict(
            w_in=w,
            w_lat=wlat,
            w_uq=jnp.concatenate([uq_nope, uq_rope], axis=1).astype(BF16),
            w_uk=ukv[:, :, :MLA_NOPE].reshape(MLA_KV_LORA, MLA_HEADS * MLA_NOPE).astype(BF16),
            w_uv=ukv[:, :, MLA_NOPE:].reshape(MLA_KV_LORA, MLA_HEADS * MLA_V).astype(BF16),
            w_out=w_out_even[e].astype(BF16),
        ))
    m_qk = M_HEADS * M_DK
    n_qkvo = 2 * m_qk + 2 * M_HEADS * M_DV
    odd = []
    for o in range(w_in_odd.shape[0]):
        w = w_in_odd[o].astype(BF16)
        odd.append(dict(
            w_in=w,
            w_gates=jnp.pad(w[:, n_qkvo:], ((0, 0), (0, LANES - 2 * M_HEADS))),
            b_gates=jnp.pad(b_gates_odd[o], (0, LANES - 2 * M_HEADS)),
            w_out=w_out_odd[o].astype(BF16),
        ))
    q_scale = jnp.concatenate([jnp.full((m_qk,), M_DK ** -0.5, F32), jnp.ones((n_qkvo - m_qk,), F32)])
    rk_scale = jnp.concatenate([jnp.ones((ret_w,), F32), jnp.full((ret_w,), RET_DK ** -0.5, F32)])
    shared = dict(w_xq=w_xq.astype(BF16), w_xo=w_xo.astype(BF16), w_mlp1=w_mlp1.astype(BF16),
                  w_mlp2=w_mlp2.astype(BF16), q_scale=q_scale, rk_scale=rk_scale, d=d,
                  n_qk=2 * ret_w, n_vg=n_qkvg - 2 * ret_w, n_qkvo=n_qkvo)
    return even, odd, shared


def _run_trunk(x, pos, mem_k, mem_v, ret_s0, c0, n0, m0, past, norm_g, gains, even_w, odd_w, shared):
    b, s, d = x.shape
    m = b * s
    depth = norm_g.shape[0]
    t128, t64 = _rope_tables(pos)
    if m % s or s % 64:
        raise ValueError("unsupported sequence length")
    if s < MIN_ROPE_TABLE_ROWS:
        t128 = tuple(jnp.tile(t, (b, 1)) for t in t128)
        t64 = tuple(jnp.tile(t, (b, 1)) for t in t64)
    xf = x.reshape(m, d)
    h = rms_rows(xf, norm_g[0, 0], BF16)
    n_even = len(even_w)
    ckv_slab = None
    krs, rets, cs, ns, ms = [], [], [], [], []
    for layer in range(depth):
        g = norm_g[layer]
        if layer % 2 == 0:
            e = layer // 2
            w = even_w[e]
            qk = matmul(h, w["w_in"], BF16, n_cols=shared["n_qk"], col_scale=shared["rk_scale"], rope=t128)
            vg = matmul(h, w["w_in"], BF16, col0=shared["n_qk"], n_cols=shared["n_vg"])
            ret, ret_s = retention(qk.reshape(b, s, -1), vg.reshape(b, s, -1), ret_s0[e], gains["ret_gn_g"][e])
            qcat, kcat, v, ckv_slab, kr = even_latent(h, w["w_lat"], gains["mla_q_norm_g"][e].reshape(1, -1),
                                                      gains["mla_kv_norm_g"][e].reshape(1, -1),
                                                      w["w_uq"], w["w_uk"], w["w_uv"], t64, e, n_even, ckv_slab)
            kcat = kcat.reshape(b, s, -1)
            v = v.reshape(b, s, -1)
            if past is None:
                q_off, sk = 0, s
            else:
                p_ckv, p_krp = past
                p_len = p_ckv.shape[2]
                q_off, sk = p_len, p_len + s
                pad = (-sk) % KV_PAD_MULTIPLE
                pk, pv = past_kv(p_ckv, p_krp, e, w["w_uk"], w["w_uv"], sk + pad)
                tail = lambda new: jnp.concatenate([new, jnp.zeros((b, pad, new.shape[-1]), BF16)], axis=1)
                kcat = lax.dynamic_update_slice(pk, tail(kcat), (0, p_len, 0))
                v = lax.dynamic_update_slice(pv, tail(v), (0, p_len, 0))
            att = mla_attention(qcat.reshape(b, s, -1), kcat, v, q_off=q_off, sk_valid=sk)
            xf, h = outproj([ret.reshape(m, -1), att.reshape(m, -1)], w["w_out"], xf, g[1], g[2])
            rets.append(ret_s)
            krs.append(kr.reshape(b, s, -1))
        else:
            o = layer // 2
            w = odd_w[o]
            qkvo = matmul(h, w["w_in"], BF16, n_cols=shared["n_qkvo"], col_scale=shared["q_scale"])
            gates = matmul(h, w["w_gates"], F32, col_bias=w["b_gates"])
            hh, c, n, mm = mlstm(qkvo.reshape(b, s, -1), gates.reshape(b, s, -1), c0[o], n0[o], m0[o],
                                 gains["mlstm_norm_g"][o])
            xf, h = outproj([hh.reshape(m, -1)], w["w_out"], xf, g[1], g[2])
            cs.append(c)
            ns.append(n)
            ms.append(mm)
        xf, h = xattn_block(h, xf, mem_k[layer], mem_v[layer], shared["w_xq"][layer], shared["w_xo"][layer],
                            g[3], g[4], s)
        g_next = norm_g[layer + 1, 0] if layer + 1 < depth else None
        xf, h = mlp_block(h, xf, shared["w_mlp1"], shared["w_mlp2"], layer, g[5], g_next)
    return (xf.reshape(b, s, d), ckv_slab.reshape(n_even, b, s, -1), jnp.stack(krs), jnp.stack(rets),
            jnp.stack(cs), jnp.stack(ns), jnp.stack(ms))


def kernel(x_prompt, x_sample, cache_mla_ckv, cache_mla_krope, state_ret, state_mlstm_C, state_mlstm_n, state_mlstm_m, cache_mem_k, cache_mem_v, mem_prompt, norm_g, mem_norm_g, w_in_even, mla_q_norm_g, mla_kv_norm_g, w_uq, w_ukv, ret_gn_g, w_out_even, w_in_odd, b_gates_odd, mlstm_norm_g, w_out_odd, w_xq, w_xk, w_xv, w_xo, w_mlp1, w_mlp2):
    even_w, odd_w, shared = _prep_weights(w_in_even, w_uq, w_ukv, w_out_even, w_in_odd, b_gates_odd, w_out_odd,
                                          w_xq, w_xo, w_mlp1, w_mlp2)
    gains = dict(ret_gn_g=ret_gn_g, mla_q_norm_g=mla_q_norm_g, mla_kv_norm_g=mla_kv_norm_g,
                 mlstm_norm_g=mlstm_norm_g)
    depth = norm_g.shape[0]
    b, s, d = x_prompt.shape
    n_even, n_odd = w_in_even.shape[0], w_in_odd.shape[0]
    xw = X_HEADS * X_DH

    bm, t, _ = mem_prompt.shape
    mem_flat = mem_prompt.reshape(bm * t, d)
    p_mem_k, p_mem_v = [], []
    for layer in range(depth):
        mn = rms_rows(mem_flat, mem_norm_g[layer], BF16)
        kv = matmul(mn, jnp.concatenate([w_xk[layer], w_xv[layer]], axis=1).astype(BF16), F32)
        p_mem_k.append(kv[:, :xw].reshape(bm, t, X_HEADS, X_DH))
        p_mem_v.append(kv[:, xw:].reshape(bm, t, X_HEADS, X_DH))
    p_mem_k, p_mem_v = jnp.stack(p_mem_k), jnp.stack(p_mem_v)

    pos_p = jnp.arange(s, dtype=jnp.int32)
    zeros = lambda *shape: jnp.zeros(shape, F32)
    y_prompt, p_ckv, p_kr, p_ret, p_c, p_n, p_m = _run_trunk(
        x_prompt, pos_p, p_mem_k.reshape(depth, bm, t, xw).astype(BF16), p_mem_v.reshape(depth, bm, t, xw).astype(BF16),
        zeros(n_even, b, RET_HEADS, RET_DK, RET_DV), zeros(n_odd, b, M_HEADS, M_DV, M_DK),
        zeros(n_odd, b, M_HEADS, M_DK), zeros(n_odd, b, M_HEADS), None, norm_g, gains, even_w, odd_w, shared)

    db, ds, _ = x_sample.shape
    past_len = cache_mla_ckv.shape[2]
    pos_s = past_len + jnp.arange(ds, dtype=jnp.int32)
    past = (cache_mla_ckv, jnp.pad(cache_mla_krope, ((0, 0), (0, 0), (0, 0), (0, LANES - MLA_ROPE))))
    y_sample, s_ckv, s_kr, s_ret, s_c, s_n, s_m = _run_trunk(
        x_sample, pos_s, cache_mem_k.reshape(depth, db, -1, xw).astype(BF16),
        cache_mem_v.reshape(depth, db, -1, xw).astype(BF16),
        state_ret, state_mlstm_C, state_mlstm_n, state_mlstm_m, past, norm_g, gains, even_w, odd_w, shared)

    return (y_prompt, y_sample, p_ckv, p_kr, p_ret, p_c, p_n, p_m, p_mem_k, p_mem_v,
            s_ckv, s_kr, s_ret, s_c, s_n, s_m)
```

```python
import functools

import jax
import jax.numpy as jnp
from jax import lax
from jax.experimental import pallas as pl
from jax.experimental.pallas import tpu as pltpu

F32 = jnp.float32
BF16 = jnp.bfloat16

RMS_EPS = 1e-6
ROPE_BASE = 10000.0
CHUNK = 64
CHUNK_SHIFT = 6
assert 1 << CHUNK_SHIFT == CHUNK

LOG2E = 1.4426950408889634
RET_HEADS, RET_DK, RET_DV = 8, 128, 128
MLA_HEADS, MLA_NOPE, MLA_ROPE, MLA_V = 8, 128, 64, 128
MLA_Q_LORA, MLA_KV_LORA = 768, 512
MLA_Q_PRESCALE = (MLA_NOPE + MLA_ROPE) ** -0.5 * LOG2E
M_HEADS, M_DK, M_DV = 8, 128, 256
X_HEADS, X_DH = 4, 128
X_Q_PRESCALE = X_DH ** -0.5 * LOG2E

LANES = 128
MXU_DIM = 256
V7X_VMEM_BYTES = 64 * 1024 * 1024
MIB = 1024 * 1024

MLA_QK_PAD = MXU_DIM
KV_PAD_MULTIPLE = MXU_DIM
KV_UNROLL = 2
MLA_HEADS_PER_STEP = 2
ATTN_SINGLE_STEP_ELEMS = 512 * 512
HEAD_GROUPS = (range(0, 4), range(4, 8))
ROW_CHAINS = 2
MIN_ROPE_TABLE_ROWS = 256


def _cparams(semantics, vmem_mib):
    assert vmem_mib * MIB < V7X_VMEM_BYTES
    return pltpu.CompilerParams(dimension_semantics=semantics, vmem_limit_bytes=vmem_mib * MIB)


def _pick(n, cands):
    for c in cands:
        if c <= n and n % c == 0:
            return c
    raise ValueError(f"no tile for {n} in {cands}")


def _rms(x, g):
    ms = jnp.mean(x * x, axis=-1, keepdims=True)
    return x * lax.rsqrt(ms + RMS_EPS) * g


def _sigmoid(x):
    return 1.0 / (1.0 + jnp.exp(-x))


def _dot(a, b):
    return jnp.dot(a, b, preferred_element_type=F32)


def _dot_nt(a, b):
    return lax.dot_general(a, b, (((1,), (1,)), ((), ())), preferred_element_type=F32)


def _dot_tn(a, b):
    return lax.dot_general(a, b, (((0,), (0,)), ((), ())), preferred_element_type=F32)


def _rope128(x, cos2, sin2):
    return x * cos2 + pltpu.roll(x, 64, 1) * sin2


def _rope64(x, cos_p, sin_p):
    return x * cos_p + (pltpu.roll(x, 32, 1) - pltpu.roll(x, 96, 1)) * sin_p


def _norm_kernel(x_ref, g_ref, o_ref):
    o_ref[...] = _rms(x_ref[...].astype(F32), g_ref[...]).astype(o_ref.dtype)


def rms_rows(x, g, out_dtype):
    m, k = x.shape
    tm = _pick(m, (512, 256, 128, 64, 32, 16, 8))
    return pl.pallas_call(
        _norm_kernel,
        grid=(m // tm,),
        in_specs=[pl.BlockSpec((tm, k), lambda i: (i, 0)), pl.BlockSpec((1, k), lambda i: (0, 0))],
        out_specs=pl.BlockSpec((tm, k), lambda i: (i, 0)),
        out_shape=jax.ShapeDtypeStruct((m, k), out_dtype),
        compiler_params=_cparams(("parallel",), 32),
        name="rms_rows",
    )(x, g.reshape(1, k).astype(F32))


def _mm_kernel(*refs, has_scale, has_bias, n_rope, tn):
    it = iter(refs)
    h_ref, w_ref = next(it), next(it)
    scale_ref = next(it) if has_scale else None
    bias_ref = next(it) if has_bias else None
    cos_ref, sin_ref = (next(it), next(it)) if n_rope else (None, None)
    o_ref = next(it)
    acc = _dot(h_ref[...], w_ref[...])
    if has_scale:
        acc = acc * scale_ref[...]
    if has_bias:
        acc = acc + bias_ref[...]
    if n_rope:
        c, s = cos_ref[...], sin_ref[...]
        for t in range(tn // LANES):
            sl = slice(t * LANES, (t + 1) * LANES)
            o_ref[:, sl] = _rope128(acc[:, sl], c, s).astype(o_ref.dtype)
    else:
        o_ref[...] = acc.astype(o_ref.dtype)


def matmul(h, w, out_dtype, *, col0=0, n_cols=None, col_scale=None, col_bias=None, rope=None):
    m, k = h.shape
    n = w.shape[1] - col0 if n_cols is None else n_cols
    tm = _pick(m, (2048, 1024, 512, 256, 128, 64))
    tn = _pick(n, (1024, 512, 256, 128))
    assert col0 % tn == 0 and col0 + n <= w.shape[1]
    jb = col0 // tn
    args = [h, w]
    in_specs = [pl.BlockSpec((tm, k), lambda i, j: (i, 0)), pl.BlockSpec((k, tn), lambda i, j: (0, j + jb))]
    for vec in (col_scale, col_bias):
        if vec is not None:
            args.append(vec.reshape(1, n).astype(F32))
            in_specs.append(pl.BlockSpec((1, tn), lambda i, j: (0, j)))
    if rope is not None:
        r = rope[0].shape[0]
        tm = _pick(m, tuple(c for c in (1024, 512, 256, 128, 64) if r % c == 0))
        in_specs[0] = pl.BlockSpec((tm, k), lambda i, j: (i, 0))
        nrb = r // tm
        for tab in rope:
            args.append(tab)
            in_specs.append(pl.BlockSpec((tm, LANES), lambda i, j: (i % nrb, 0)))
    kern = functools.partial(_mm_kernel, has_scale=col_scale is not None, has_bias=col_bias is not None,
                             n_rope=rope is not None, tn=tn)
    return pl.pallas_call(
        kern,
        grid=(m // tm, n // tn),
        in_specs=in_specs,
        out_specs=pl.BlockSpec((tm, tn), lambda i, j: (i, j)),
        out_shape=jax.ShapeDtypeStruct((m, n), out_dtype),
        compiler_params=_cparams(("parallel", "arbitrary"), 56),
        name="matmul",
    )(*args)


def _kv_expand(ckvn_bf16, krr_bf16, wuk_ref, wuv_ref, kcat_ref, v_ref):
    kn = _dot(ckvn_bf16, wuk_ref[...])
    for hd in range(MLA_HEADS):
        kcat_ref[:, hd * MLA_QK_PAD: hd * MLA_QK_PAD + MLA_NOPE] = (
            kn[:, hd * MLA_NOPE:(hd + 1) * MLA_NOPE].astype(BF16))
        kcat_ref[:, hd * MLA_QK_PAD + MLA_NOPE:(hd + 1) * MLA_QK_PAD] = krr_bf16
    v_ref[...] = _dot(ckvn_bf16, wuv_ref[...]).astype(BF16)


def _even_latent_kernel(h_ref, wlat_ref, gq_ref, gkv_ref, wuq_ref, wuk_ref, wuv_ref, cos_ref, sin_ref, *refs):
    qcat_ref, kcat_ref, v_ref, ckv_ref, kr_ref = refs[-5:]
    lat = _dot(h_ref[...], wlat_ref[...])
    cq = lat[:, :MLA_Q_LORA]
    ckv = lat[:, MLA_Q_LORA:MLA_Q_LORA + MLA_KV_LORA]
    krp = lat[:, MLA_Q_LORA + MLA_KV_LORA:]
    c, s = cos_ref[...], sin_ref[...]
    q = _dot(_rms(cq, gq_ref[...]).astype(BF16), wuq_ref[...]) * MLA_Q_PRESCALE
    nope_w = MLA_HEADS * MLA_NOPE
    for hd in range(MLA_HEADS):
        qcat_ref[:, hd * MLA_QK_PAD: hd * MLA_QK_PAD + MLA_NOPE] = (
            q[:, hd * MLA_NOPE:(hd + 1) * MLA_NOPE].astype(BF16))
        qr = q[:, nope_w + hd * LANES: nope_w + (hd + 1) * LANES]
        qcat_ref[:, hd * MLA_QK_PAD + MLA_NOPE:(hd + 1) * MLA_QK_PAD] = _rope64(qr, c, s).astype(BF16)
    ckvn = _rms(ckv, gkv_ref[...])
    ckv_ref[...] = ckvn
    krr = _rope64(krp, c, s)
    kr_ref[...] = krr[:, :MLA_ROPE]
    _kv_expand(ckvn.astype(BF16), krr.astype(BF16), wuk_ref, wuv_ref, kcat_ref, v_ref)


def even_latent(h, wlat, gq, gkv, wuq, wuk, wuv, rope64, e, n_even, ckv_slab):
    m, d = h.shape
    r = rope64[0].shape[0]
    tm = _pick(m, tuple(c for c in (512, 256, 128, 64) if r % c == 0))
    nrb = r // tm
    full = lambda a: pl.BlockSpec(a.shape, lambda i: (0,) * a.ndim)
    row = lambda w: pl.BlockSpec((tm, w), lambda i: (i, 0))
    tab = pl.BlockSpec((tm, LANES), lambda i: (i % nrb, 0))
    qk_w = MLA_HEADS * MLA_QK_PAD
    v_w = MLA_HEADS * MLA_V
    args = [h, wlat, gq, gkv, wuq, wuk, wuv, rope64[0], rope64[1]]
    in_specs = [row(d), full(wlat), full(gq), full(gkv), full(wuq), full(wuk), full(wuv), tab, tab]
    aliases = {}
    if ckv_slab is not None:
        aliases = {len(args): 3}
        args.append(ckv_slab)
        in_specs.append(pl.BlockSpec(memory_space=pl.ANY))
    return pl.pallas_call(
        _even_latent_kernel,
        grid=(m // tm,),
        in_specs=in_specs,
        out_specs=[row(qk_w), row(qk_w), row(v_w),
                   pl.BlockSpec((None, tm, MLA_KV_LORA), lambda i: (e, i, 0)), row(MLA_ROPE)],
        out_shape=[jax.ShapeDtypeStruct((m, qk_w), BF16), jax.ShapeDtypeStruct((m, qk_w), BF16),
                   jax.ShapeDtypeStruct((m, v_w), BF16), jax.ShapeDtypeStruct((n_even, m, MLA_KV_LORA), F32),
                   jax.ShapeDtypeStruct((m, MLA_ROPE), F32)],
        input_output_aliases=aliases,
        compiler_params=_cparams(("parallel",), 48),
        name="even_latent",
    )(*args)


def _past_kv_kernel(ckv_ref, krp_ref, wuk_ref, wuv_ref, kcat_ref, v_ref):
    _kv_expand(ckv_ref[...].astype(BF16), krp_ref[...].astype(BF16), wuk_ref, wuv_ref, kcat_ref, v_ref)


def past_kv(ckvn, kr_pad, e, wuk, wuv, skp):
    _, b, p_len, _ = ckvn.shape
    tm = _pick(p_len, (512, 256, 128, 64))
    full = lambda a: pl.BlockSpec(a.shape, lambda bi, i: (0,) * a.ndim)
    src = lambda w: pl.BlockSpec((None, None, tm, w), lambda bi, i: (e, bi, i, 0))
    dst = lambda w: pl.BlockSpec((None, tm, w), lambda bi, i: (bi, i, 0))
    qk_w = MLA_HEADS * MLA_QK_PAD
    v_w = MLA_HEADS * MLA_V
    return pl.pallas_call(
        _past_kv_kernel,
        grid=(b, p_len // tm),
        in_specs=[src(MLA_KV_LORA), src(LANES), full(wuk), full(wuv)],
        out_specs=[dst(qk_w), dst(v_w)],
        out_shape=[jax.ShapeDtypeStruct((b, skp, qk_w), BF16), jax.ShapeDtypeStruct((b, skp, v_w), BF16)],
        compiler_params=_cparams(("parallel", "parallel"), 32),
        name="past_kv",
    )(ckvn, kr_pad, wuk, wuv)


def _mla_kernel(q_ref, k_ref, v_ref, o_ref, m_scr, l_scr, acc_scr, a_scr, p_scr, *, tq, tk, hb, q_off, sk_valid):
    qi = pl.program_id(2)
    q0 = qi * tq
    cq_lo = (q_off + q0) // CHUNK
    cq_hi = (q_off + q0 + tq - 1) // CHUNK
    n_full = jnp.minimum(lax.div((cq_lo + 1) * CHUNK, tk), sk_valid // tk)
    n_vis = lax.div(jnp.minimum((cq_hi + 1) * CHUNK, sk_valid) + tk - 1, tk)
    m_scr[...] = jnp.full(m_scr.shape, -1e30, F32)
    l_scr[...] = jnp.zeros(l_scr.shape, F32)
    acc_scr[...] = jnp.zeros(acc_scr.shape, F32)

    def apply_prev(j, ki_prev):
        v = v_ref[0, pl.ds(pl.multiple_of(ki_prev * tk, tk), tk), j * MLA_V:(j + 1) * MLA_V]
        acc_scr[j] = a_scr[j] * acc_scr[j] + _dot(p_scr[j], v)

    def step(ki, masked, first=False):
        k0 = pl.multiple_of(ki * tk, tk)
        if masked:
            qpos = q_off + q0 + lax.broadcasted_iota(jnp.int32, (tq, tk), 0)
            kpos = k0 + lax.broadcasted_iota(jnp.int32, (tq, tk), 1)
            vis = jnp.logical_and((kpos >> CHUNK_SHIFT) <= (qpos >> CHUNK_SHIFT), kpos < sk_valid)
        for j in range(hb):
            q = q_ref[0, :, j * MLA_QK_PAD:(j + 1) * MLA_QK_PAD]
            k = k_ref[0, pl.ds(k0, tk), j * MLA_QK_PAD:(j + 1) * MLA_QK_PAD]
            s = _dot_nt(q, k)
            if not first:
                apply_prev(j, ki - 1)
            if masked:
                s = jnp.where(vis, s, -jnp.inf)
            m_prev = m_scr[j]
            m_new = jnp.maximum(m_prev, jnp.max(s, axis=-1, keepdims=True))
            alpha = jnp.exp2(m_prev - m_new)
            p = [jnp.exp2(s[:, c * LANES:(c + 1) * LANES] - m_new) for c in range(tk // LANES)]
            l_scr[j] = alpha * l_scr[j] + sum(p[1:], p[0])
            p_scr[j] = jnp.concatenate([pc.astype(BF16) for pc in p], axis=-1)
            a_scr[j] = alpha
            m_scr[j] = m_new

    def body_full(ki, carry):
        step(ki, False)
        return carry

    def body_masked(ki, carry):
        step(ki, True)
        return carry

    def body_full_group(kg, carry):
        for u in range(KV_UNROLL):
            step(1 + KV_UNROLL * kg + u, False)
        return carry

    step(0, True, first=True)
    n_full1 = jnp.maximum(n_full, 1)
    n_groups = lax.div(n_full1 - 1, KV_UNROLL)
    lax.fori_loop(0, n_groups, body_full_group, 0)
    lax.fori_loop(1 + KV_UNROLL * n_groups, n_full1, body_full, 0)
    lax.fori_loop(n_full1, n_vis, body_masked, 0)
    for j in range(hb):
        apply_prev(j, n_vis - 1)
        l = jnp.sum(l_scr[j], axis=-1, keepdims=True)
        o_ref[0, :, j * MLA_V:(j + 1) * MLA_V] = (acc_scr[j] / l).astype(o_ref.dtype)


def mla_attention(qcat, kcat, v, *, q_off, sk_valid):
    b, sq, _ = qcat.shape
    skp = kcat.shape[1]
    tq = _pick(sq, (512, 256, 128, 64))
    tk = skp if tq * skp <= ATTN_SINGLE_STEP_ELEMS else _pick(skp, (512, 256, 128))
    hb = MLA_HEADS if tk == skp else MLA_HEADS_PER_STEP
    kern = functools.partial(_mla_kernel, tq=tq, tk=tk, hb=hb, q_off=q_off, sk_valid=sk_valid)
    return pl.pallas_call(
        kern,
        grid=(b, MLA_HEADS // hb, sq // tq),
        in_specs=[pl.BlockSpec((1, tq, hb * MLA_QK_PAD), lambda bi, h, qi: (bi, qi, h)),
                  pl.BlockSpec((1, skp, hb * MLA_QK_PAD), lambda bi, h, qi: (bi, 0, h)),
                  pl.BlockSpec((1, skp, hb * MLA_V), lambda bi, h, qi: (bi, 0, h))],
        out_specs=pl.BlockSpec((1, tq, hb * MLA_V), lambda bi, h, qi: (bi, qi, h)),
        out_shape=jax.ShapeDtypeStruct((b, sq, MLA_HEADS * MLA_V), BF16),
        scratch_shapes=[pltpu.VMEM((hb, tq, LANES), F32), pltpu.VMEM((hb, tq, LANES), F32),
                        pltpu.VMEM((hb, tq, MLA_V), F32), pltpu.VMEM((hb, tq, LANES), F32),
                        pltpu.VMEM((hb, tq, tk), BF16)],
        compiler_params=_cparams(("parallel", "parallel", "arbitrary"), 48),
        name="mla_attention",
    )(qcat, kcat, v)


def _retention_kernel(lg_ref, qk_ref, vg_ref, s0_ref, gn_ref, out_ref, sfin_ref,
                      state_scr, decay_scr, qdec_scr, kdec_scr, *, lc):
    ci = pl.program_id(1)
    heads = range(RET_HEADS)

    @pl.when(ci == 0)
    def _():
        state_scr[...] = s0_ref[0]
        ti = lax.broadcasted_iota(jnp.int32, (lc, lc), 0)
        si = lax.broadcasted_iota(jnp.int32, (lc, lc), 1)
        dpos = jnp.maximum(ti - si, 0).astype(F32)
        trow = lax.broadcasted_iota(jnp.int32, (lc, LANES), 0).astype(F32)
        for h in heads:
            lg = lg_ref[h:h + 1, :]
            decay_scr[h] = jnp.where(ti >= si, jnp.exp(lg[:, 0:1] * dpos), 0.0)
            qdec_scr[h] = jnp.exp(lg * (trow + 1.0))
            kdec_scr[h] = jnp.exp(lg * (lc - 1.0 - trow))

    qk_w = RET_HEADS * RET_DK
    v_w = RET_HEADS * RET_DV
    q = [qk_ref[0, :, h * RET_DK:(h + 1) * RET_DK] for h in heads]
    k = [qk_ref[0, :, qk_w + h * RET_DK: qk_w + (h + 1) * RET_DK] for h in heads]
    v = [vg_ref[0, :, h * RET_DV:(h + 1) * RET_DV] for h in heads]
    st = [state_scr[h] for h in heads]
    sc = [_dot_nt(q[h], k[h]) for h in heads]
    inter = [_dot(q[h], st[h].astype(BF16)) for h in heads]
    scd = [(sc[h] * decay_scr[h]).astype(BF16) for h in heads]
    kd = [(k[h].astype(F32) * kdec_scr[h]).astype(BF16) for h in heads]
    intra = [_dot(scd[h], v[h]) for h in heads]
    upd = [_dot_tn(kd[h], v[h]) for h in heads]
    for h in heads:
        state_scr[h] = jnp.exp(lg_ref[h:h + 1, :] * lc) * st[h] + upd[h]
        out = intra[h] + qdec_scr[h] * inter[h]
        g = vg_ref[0, :, v_w + h * RET_DV: v_w + (h + 1) * RET_DV].astype(F32)
        xc = out - jnp.mean(out, axis=-1, keepdims=True)
        y = xc * lax.rsqrt(jnp.mean(xc * xc, axis=-1, keepdims=True) + RMS_EPS)
        y = y * gn_ref[:, h * RET_DV:(h + 1) * RET_DV]
        out_ref[0, :, h * RET_DV:(h + 1) * RET_DV] = (g * _sigmoid(g) * y).astype(out_ref.dtype)

    @pl.when(ci == pl.num_programs(1) - 1)
    def _():
        sfin_ref[0] = state_scr[...]


def retention(qk, vg, s0, gn_g):
    b, s, _ = qk.shape
    lc = _pick(s, (256, 128, 64))
    log_g = jnp.log1p(-jnp.exp2(-5.0 - jnp.arange(RET_HEADS, dtype=F32)))
    lg_tab = jnp.broadcast_to(log_g[:, None], (RET_HEADS, LANES))
    v_w = RET_HEADS * RET_DV
    kern = functools.partial(_retention_kernel, lc=lc)
    st_spec = pl.BlockSpec((1, RET_HEADS, RET_DK, RET_DV), lambda bi, ci: (bi, 0, 0, 0))
    return pl.pallas_call(
        kern,
        grid=(b, s // lc),
        in_specs=[pl.BlockSpec((RET_HEADS, LANES), lambda bi, ci: (0, 0)),
                  pl.BlockSpec((1, lc, qk.shape[2]), lambda bi, ci: (bi, ci, 0)),
                  pl.BlockSpec((1, lc, vg.shape[2]), lambda bi, ci: (bi, ci, 0)),
                  st_spec,
                  pl.BlockSpec((1, v_w), lambda bi, ci: (0, 0))],
        out_specs=[pl.BlockSpec((1, lc, v_w), lambda bi, ci: (bi, ci, 0)), st_spec],
        out_shape=[jax.ShapeDtypeStruct((b, s, v_w), BF16),
                   jax.ShapeDtypeStruct((b, RET_HEADS, RET_DK, RET_DV), F32)],
        scratch_shapes=[pltpu.VMEM((RET_HEADS, RET_DK, RET_DV), F32), pltpu.VMEM((RET_HEADS, lc, lc), F32),
                        pltpu.VMEM((RET_HEADS, lc, LANES), F32), pltpu.VMEM((RET_HEADS, lc, LANES), F32)],
        compiler_params=_cparams(("parallel", "arbitrary"), 40),
        name="retention",
    )(lg_tab, qk, vg, s0, gn_g.reshape(1, v_w).astype(F32))


def _split3(x):
    hi = x.astype(BF16)
    r1 = x - hi.astype(F32)
    mid = r1.astype(BF16)
    lo = (r1 - mid.astype(F32)).astype(BF16)
    return hi, mid, lo


def _log_sigmoid(x):
    return jnp.minimum(x, 0.0) - jnp.log(1.0 + jnp.exp(-jnp.abs(x)))


def _widen(x, width):
    if width < LANES:
        return x[:, :width]
    return x if width == LANES else jnp.concatenate([x] * (width // LANES), axis=1)


def _mlstm_kernel(qkvo_ref, gates_ref, gates_t_ref, c0_ref, n0_ref, m0_ref, ng_ref,
                  out_ref, cfin_ref, nfin_ref, mfin_ref, c_scr, n_scr, m_scr, *, lc):
    ci = pl.program_id(1)

    @pl.when(ci == 0)
    def _():
        c_scr[...] = c0_ref[0]
        n_scr[...] = n0_ref[0]
        m_scr[...] = m0_ref[0]

    gates = gates_ref[0]
    gates_t = gates_t_ref[0]
    ti = lax.broadcasted_iota(jnp.int32, (lc, lc), 0)
    si = lax.broadcasted_iota(jnp.int32, (lc, lc), 1)
    tril = si <= ti
    ones_l = jnp.where(tril, 1.0, 0.0).astype(BF16)
    ones_u = jnp.where(ti <= si, 1.0, 0.0).astype(BF16)
    b_col = sum(_dot(ones_l, part) for part in _split3(_log_sigmoid(gates)))
    b_row = sum(_dot(part, ones_u) for part in _split3(_log_sigmoid(gates_t)))[M_HEADS:2 * M_HEADS]
    i_row = gates_t[0:M_HEADS]
    qk_w = M_HEADS * M_DK
    v_w = M_HEADS * M_DV
    for heads in HEAD_GROUPS:
        q = {h: qkvo_ref[0, :, h * M_DK:(h + 1) * M_DK] for h in heads}
        k = {h: qkvo_ref[0, :, qk_w + h * M_DK: qk_w + (h + 1) * M_DK] for h in heads}
        v = {h: qkvo_ref[0, :, 2 * qk_w + h * M_DV: 2 * qk_w + (h + 1) * M_DV] for h in heads}
        c_st = {h: c_scr[h] for h in heads}
        n_st = {h: n_scr[h:h + 1, :] for h in heads}
        m_prev = {h: m_scr[h:h + 1, :] for h in heads}
        qk = {h: _dot_nt(q[h], k[h]) for h in heads}
        qc = {h: _dot(q[h], c_st[h].astype(BF16)) for h in heads}
        bt = {h: jnp.broadcast_to(b_col[:, M_HEADS + h:M_HEADS + h + 1], (lc, LANES)) for h in heads}
        i_t = {h: jnp.broadcast_to(gates[:, h:h + 1], (lc, LANES)) for h in heads}
        m_t, w_inter, sc = {}, {}, {}
        for h in heads:
            log_intra = jnp.where(tril, _widen(bt[h], lc) - b_row[h:h + 1, :] + i_row[h:h + 1, :], -jnp.inf)
            log_inter = bt[h] + m_prev[h]
            m_t[h] = jnp.maximum(log_inter, jnp.max(log_intra, axis=-1, keepdims=True))
            w_inter[h] = jnp.exp(log_inter - m_t[h])
            sc[h] = qk[h] * jnp.exp(log_intra - _widen(m_t[h], lc))
        num_intra = {h: _dot(sc[h].astype(BF16), v[h]) for h in heads}
        kw = {}
        for h in heads:
            m_new = m_t[h][lc - 1:lc, :]
            b_last = bt[h][lc - 1:lc, :]
            w_state = jnp.exp(b_last + m_prev[h] - m_new)
            kw[h] = k[h].astype(F32) * jnp.exp(b_last - bt[h] + i_t[h] - m_new)
            n_scr[h:h + 1, :] = w_state * n_st[h] + jnp.sum(kw[h], axis=0, keepdims=True)
            m_scr[h:h + 1, :] = m_new
            c_scr[h] = _widen(w_state, M_DV) * c_st[h] + _dot_tn(kw[h].astype(BF16), v[h])
        for h in heads:
            num = num_intra[h] + _widen(w_inter[h], M_DV) * qc[h]
            den = (jnp.sum(sc[h], axis=-1, keepdims=True)
                   + w_inter[h] * jnp.sum(q[h].astype(F32) * n_st[h], axis=-1, keepdims=True))
            hh = num * _widen(1.0 / jnp.maximum(jnp.abs(den), jnp.exp(-m_t[h])), M_DV)
            y = hh * lax.rsqrt(jnp.mean(hh * hh, axis=-1, keepdims=True) + RMS_EPS)
            y = y * ng_ref[:, h * M_DV:(h + 1) * M_DV]
            og = qkvo_ref[0, :, 2 * qk_w + v_w + h * M_DV: 2 * qk_w + v_w + (h + 1) * M_DV].astype(F32)
            out_ref[0, :, h * M_DV:(h + 1) * M_DV] = (y * _sigmoid(og)).astype(out_ref.dtype)

    @pl.when(ci == pl.num_programs(1) - 1)
    def _():
        cfin_ref[0] = c_scr[...]
        nfin_ref[0] = n_scr[...]
        mfin_ref[0] = m_scr[...]


def mlstm(qkvo, gates, c0, n0, m0, norm_g):
    b, s, w = qkvo.shape
    lc = _pick(s, (256, 128, 64))
    v_w = M_HEADS * M_DV
    gates_t = jnp.swapaxes(gates[:, :, :2 * M_HEADS], 1, 2)
    kern = functools.partial(_mlstm_kernel, lc=lc)
    c_spec = pl.BlockSpec((1, M_HEADS, M_DK, M_DV), lambda bi, ci: (bi, 0, 0, 0))
    n_spec = pl.BlockSpec((1, M_HEADS, M_DK), lambda bi, ci: (bi, 0, 0))
    m_spec = pl.BlockSpec((1, M_HEADS, LANES), lambda bi, ci: (bi, 0, 0))
    hh, c_t, n, m = pl.pallas_call(
        kern,
        grid=(b, s // lc),
        in_specs=[pl.BlockSpec((1, lc, w), lambda bi, ci: (bi, ci, 0)),
                  pl.BlockSpec((1, lc, LANES), lambda bi, ci: (bi, ci, 0)),
                  pl.BlockSpec((1, 2 * M_HEADS, lc), lambda bi, ci: (bi, 0, ci)),
                  c_spec, n_spec, m_spec,
                  pl.BlockSpec((1, v_w), lambda bi, ci: (0, 0))],
        out_specs=[pl.BlockSpec((1, lc, v_w), lambda bi, ci: (bi, ci, 0)), c_spec, n_spec, m_spec],
        out_shape=[jax.ShapeDtypeStruct((b, s, v_w), BF16),
                   jax.ShapeDtypeStruct((b, M_HEADS, M_DK, M_DV), F32),
                   jax.ShapeDtypeStruct((b, M_HEADS, M_DK), F32),
                   jax.ShapeDtypeStruct((b, M_HEADS, LANES), F32)],
        scratch_shapes=[pltpu.VMEM((M_HEADS, M_DK, M_DV), F32), pltpu.VMEM((M_HEADS, M_DK), F32),
                        pltpu.VMEM((M_HEADS, LANES), F32)],
        compiler_params=_cparams(("parallel", "arbitrary"), 48),
        name="mlstm",
    )(qkvo, gates, gates_t, jnp.swapaxes(c0, 2, 3), n0, jnp.broadcast_to(m0[..., None], m0.shape + (LANES,)),
      norm_g.reshape(1, v_w).astype(F32))
    return hh, jnp.swapaxes(c_t, 2, 3), n, m[..., 0]


def _residual_out(acc, x_ref, gp_ref, gn_ref, xo_ref, ho_ref, rows=slice(None)):
    xn = x_ref[rows, :] + _rms(acc, gp_ref[...])
    xo_ref[rows, :] = xn
    if ho_ref is not None:
        ho_ref[rows, :] = _rms(xn, gn_ref[...]).astype(ho_ref.dtype)


def _row_chains(tm):
    n = ROW_CHAINS if tm % (ROW_CHAINS * 128) == 0 else 1
    return [slice(c * (tm // n), (c + 1) * (tm // n)) for c in range(n)]


def _outproj_kernel(*refs, n_in):
    a_refs, w_refs = refs[:n_in], refs[n_in:2 * n_in]
    x_ref, gp_ref, gn_ref, xo_ref, ho_ref = refs[2 * n_in:]
    for rows in _row_chains(x_ref.shape[0]):
        acc = _dot(a_refs[0][rows, :], w_refs[0][...])
        for a_ref, w_ref in zip(a_refs[1:], w_refs[1:]):
            acc = acc + _dot(a_ref[rows, :], w_ref[...])
        _residual_out(acc, x_ref, gp_ref, gn_ref, xo_ref, ho_ref, rows)


def outproj(acts, w, x, g_post, g_next):
    m, d = x.shape
    tm = _pick(m, (512, 256, 128, 64))
    n_in = len(acts)
    row = lambda wd: pl.BlockSpec((tm, wd), lambda i: (i, 0))
    vec = pl.BlockSpec((1, d), lambda i: (0, 0))
    w_specs, r0 = [], 0
    for a in acts:
        ka = a.shape[1]
        assert r0 % ka == 0
        w_specs.append(pl.BlockSpec((ka, d), functools.partial(lambda blk, i: (blk, 0), r0 // ka)))
        r0 += ka
    assert r0 == w.shape[0]
    return pl.pallas_call(
        functools.partial(_outproj_kernel, n_in=n_in),
        grid=(m // tm,),
        in_specs=[row(a.shape[1]) for a in acts] + w_specs + [row(d), vec, vec],
        out_specs=[row(d), row(d)],
        out_shape=[jax.ShapeDtypeStruct((m, d), F32), jax.ShapeDtypeStruct((m, d), BF16)],
        compiler_params=_cparams(("parallel",), 56),
        name="outproj",
    )(*acts, *([w] * n_in), x, g_post.reshape(1, d), g_next.reshape(1, d))


def _xattn_kernel(h_ref, wq_ref, mk_ref, mv_ref, wo_ref, x_ref, gp_ref, gn_ref, xo_ref, ho_ref):
    mk, mv = mk_ref[0], mv_ref[0]
    t = mk.shape[0]
    chains = _row_chains(x_ref.shape[0])
    heads = [slice(hd * X_DH, (hd + 1) * X_DH) for hd in range(X_HEADS)]
    q = [(_dot(h_ref[rows, :], wq_ref[...]) * X_Q_PRESCALE).astype(BF16) for rows in chains]
    s = [[_dot_nt(qc[:, sl], mk[:, sl]) for sl in heads] for qc in q]
    p, inv_l = [], []
    for sc in s:
        pc, lc = [], []
        for sh in sc:
            e = jnp.exp2(sh - jnp.max(sh, axis=-1, keepdims=True))
            part = e[:, :LANES]
            for c in range(1, t // LANES):
                part = part + e[:, c * LANES:(c + 1) * LANES]
            pc.append(e.astype(BF16))
            lc.append(1.0 / jnp.sum(part, axis=-1, keepdims=True))
        p.append(pc)
        inv_l.append(lc)
    o = [jnp.concatenate([(_dot(pc[hd], mv[:, heads[hd]]) * lc[hd]).astype(BF16) for hd in range(X_HEADS)], axis=-1)
         for pc, lc in zip(p, inv_l)]
    acc = [_dot(oc, wo_ref[...]) for oc in o]
    for rows, ac in zip(chains, acc):
        _residual_out(ac, x_ref, gp_ref, gn_ref, xo_ref, ho_ref, rows)


def xattn_block(h, x, mem_k, mem_v, wq, wo, g_post, g_next, seq):
    m, d = x.shape
    tm = _pick(seq, (512, 256, 128, 64))
    per_b = seq // tm
    t, xw = mem_k.shape[1], mem_k.shape[2]
    full = lambda a: pl.BlockSpec(a.shape, lambda i: (0,) * a.ndim)
    row = lambda w: pl.BlockSpec((tm, w), lambda i: (i, 0))
    mem = pl.BlockSpec((1, t, xw), lambda i: (i // per_b, 0, 0))
    vec = pl.BlockSpec((1, d), lambda i: (0, 0))
    return pl.pallas_call(
        _xattn_kernel,
        grid=(m // tm,),
        in_specs=[row(d), full(wq), mem, mem, full(wo), row(d), vec, vec],
        out_specs=[row(d), row(d)],
        out_shape=[jax.ShapeDtypeStruct((m, d), F32), jax.ShapeDtypeStruct((m, d), BF16)],
        compiler_params=_cparams(("parallel",), 48),
        name="xattn_block",
    )(h, wq, mem_k, mem_v, wo, x, g_post.reshape(1, d), g_next.reshape(1, d))


def _mlp_kernel(h_ref, w1_ref, w2_ref, x_ref, gp_ref, gn_ref, xo_ref, *rest, emit_h):
    ho_ref, acc_scr = (rest[0], rest[1]) if emit_h else (None, rest[0])
    f = pl.program_id(1)
    @pl.when(f == 0)
    def _():
        acc_scr[...] = jnp.zeros(acc_scr.shape, F32)

    a = jnp.maximum(_dot(h_ref[...], w1_ref[...]), 0.0)
    acc_scr[...] += _dot((a * a).astype(BF16), w2_ref[...])

    @pl.when(f == pl.num_programs(1) - 1)
    def _():
        _residual_out(acc_scr[...], x_ref, gp_ref, gn_ref, xo_ref, ho_ref)


def mlp_block(h, x, w1, w2, layer, g_post, g_next):
    m, d = x.shape
    ff = w1.shape[2]
    tm = _pick(m, (512, 256, 128, 64))
    tf = _pick(ff, (1024, 512, 256, 128))
    emit_h = g_next is not None
    row = pl.BlockSpec((tm, d), lambda i, f: (i, 0))
    vec = pl.BlockSpec((1, d), lambda i, f: (0, 0))
    out_specs = [row] + ([row] if emit_h else [])
    out_shape = [jax.ShapeDtypeStruct((m, d), F32)] + ([jax.ShapeDtypeStruct((m, d), BF16)] if emit_h else [])
    g_n = (g_next if emit_h else g_post).reshape(1, d)
    res = pl.pallas_call(
        functools.partial(_mlp_kernel, emit_h=emit_h),
        grid=(m // tm, ff // tf),
        in_specs=[row, pl.BlockSpec((None, d, tf), lambda i, f: (layer, 0, f)),
                  pl.BlockSpec((None, tf, d), lambda i, f: (layer, f, 0)), row, vec, vec],
        out_specs=out_specs,
        out_shape=out_shape,
        scratch_shapes=[pltpu.VMEM((tm, d), F32)],
        compiler_params=_cparams(("parallel", "arbitrary"), 56),
        name="mlp_block",
    )(h, w1, w2, x, g_post.reshape(1, d), g_n)
    return (res[0], res[1]) if emit_h else (res[0], None)


def _rope_tables(pos):
    def angles(half):
        inv = jnp.power(ROPE_BASE, -jnp.arange(half, dtype=F32) / half)
        return pos.astype(F32)[:, None] * inv[None, :]
    a = angles(RET_DK // 2)
    c, s = jnp.cos(a), jnp.sin(a)
    t128 = (jnp.concatenate([c, c], axis=1), jnp.concatenate([-s, s], axis=1))
    a = angles(MLA_ROPE // 2)
    c, s = jnp.cos(a), jnp.sin(a)
    z = jnp.zeros((pos.shape[0], LANES - MLA_ROPE), F32)
    t64 = (jnp.concatenate([c, c, z], axis=1), jnp.concatenate([s, s, z], axis=1))
    return t128, t64


def _prep_weights(w_in_even, w_uq, w_ukv, w_out_even, w_in_odd, b_gates_odd, w_out_odd, w_xq, w_xo, w_mlp1, w_mlp2):
    d = w_in_even.shape[1]
    ret_w = RET_HEADS * RET_DK
    n_qkvg = 2 * ret_w + 2 * RET_HEADS * RET_DV
    n_lat = MLA_Q_LORA + MLA_KV_LORA + MLA_ROPE
    even = []
    for e in range(w_in_even.shape[0]):
        w = w_in_even[e].astype(BF16)
        wlat = jnp.pad(w[:, n_qkvg:n_qkvg + n_lat], ((0, 0), (0, LANES - MLA_ROPE)))
        uq = w_uq[e].reshape(MLA_Q_LORA, MLA_HEADS, MLA_NOPE + MLA_ROPE)
        uq_nope = uq[:, :, :MLA_NOPE].reshape(MLA_Q_LORA, MLA_HEADS * MLA_NOPE)
        uq_rope = jnp.pad(uq[:, :, MLA_NOPE:], ((0, 0), (0, 0), (0, LANES - MLA_ROPE)))
        uq_rope = uq_rope.reshape(MLA_Q_LORA, MLA_HEADS * LANES)
        ukv = w_ukv[e].reshape(MLA_KV_LORA, MLA_HEADS, MLA_NOPE + MLA_V)
        even.append(dict(
            w_in=w,
            w_lat=wlat,
            w_uq=jnp.concatenate([uq_nope, uq_rope], axis=1).astype(BF16),
            w_uk=ukv[:, :, :MLA_NOPE].reshape(MLA_KV_LORA, MLA_HEADS * MLA_NOPE).astype(BF16),
            w_uv=ukv[:, :, MLA_NOPE:].reshape(MLA_KV_LORA, MLA_HEADS * MLA_V).astype(BF16),
            w_out=w_out_even[e].astype(BF16),
        ))
    m_qk = M_HEADS * M_DK
    n_qkvo = 2 * m_qk + 2 * M_HEADS * M_DV
    odd = []
    for o in range(w_in_odd.shape[0]):
        w = w_in_odd[o].astype(BF16)
        odd.append(dict(
            w_in=w,
            w_gates=jnp.pad(w[:, n_qkvo:], ((0, 0), (0, LANES - 2 * M_HEADS))),
            b_gates=jnp.pad(b_gates_odd[o], (0, LANES - 2 * M_HEADS)),
            w_out=w_out_odd[o].astype(BF16),
        ))
    q_scale = jnp.concatenate([jnp.full((m_qk,), M_DK ** -0.5, F32), jnp.ones((n_qkvo - m_qk,), F32)])
    rk_scale = jnp.concatenate([jnp.ones((ret_w,), F32), jnp.full((ret_w,), RET_DK ** -0.5, F32)])
    shared = dict(w_xq=w_xq.astype(BF16), w_xo=w_xo.astype(BF16), w_mlp1=w_mlp1.astype(BF16),
                  w_mlp2=w_mlp2.astype(BF16), q_scale=q_scale, rk_scale=rk_scale, d=d,
                  n_qk=2 * ret_w, n_vg=n_qkvg - 2 * ret_w, n_qkvo=n_qkvo)
    return even, odd, shared


def _run_trunk(x, pos, mem_k, mem_v, ret_s0, c0, n0, m0, past, norm_g, gains, even_w, odd_w, shared):
    b, s, d = x.shape
    m = b * s
    depth = norm_g.shape[0]
    t128, t64 = _rope_tables(pos)
    if m % s or s % 64:
        raise ValueError("unsupported sequence length")
    if s < MIN_ROPE_TABLE_ROWS:
        t128 = tuple(jnp.tile(t, (b, 1)) for t in t128)
        t64 = tuple(jnp.tile(t, (b, 1)) for t in t64)
    xf = x.reshape(m, d)
    h = rms_rows(xf, norm_g[0, 0], BF16)
    n_even = len(even_w)
    ckv_slab = None
    krs, rets, cs, ns, ms = [], [], [], [], []
    for layer in range(depth):
        g = norm_g[layer]
        if layer % 2 == 0:
            e = layer // 2
            w = even_w[e]
            qk = matmul(h, w["w_in"], BF16, n_cols=shared["n_qk"], col_scale=shared["rk_scale"], rope=t128)
            vg = matmul(h, w["w_in"], BF16, col0=shared["n_qk"], n_cols=shared["n_vg"])
            ret, ret_s = retention(qk.reshape(b, s, -1), vg.reshape(b, s, -1), ret_s0[e], gains["ret_gn_g"][e])
            qcat, kcat, v, ckv_slab, kr = even_latent(h, w["w_lat"], gains["mla_q_norm_g"][e].reshape(1, -1),
                                                      gains["mla_kv_norm_g"][e].reshape(1, -1),
                                                      w["w_uq"], w["w_uk"], w["w_uv"], t64, e, n_even, ckv_slab)
            kcat = kcat.reshape(b, s, -1)
            v = v.reshape(b, s, -1)
            if past is None:
                q_off, sk = 0, s
            else:
                p_ckv, p_krp = past
                p_len = p_ckv.shape[2]
                q_off, sk = p_len, p_len + s
                pad = (-sk) % KV_PAD_MULTIPLE
                pk, pv = past_kv(p_ckv, p_krp, e, w["w_uk"], w["w_uv"], sk + pad)
                tail = lambda new: jnp.concatenate([new, jnp.zeros((b, pad, new.shape[-1]), BF16)], axis=1)
                kcat = lax.dynamic_update_slice(pk, tail(kcat), (0, p_len, 0))
                v = lax.dynamic_update_slice(pv, tail(v), (0, p_len, 0))
            att = mla_attention(qcat.reshape(b, s, -1), kcat, v, q_off=q_off, sk_valid=sk)
            xf, h = outproj([ret.reshape(m, -1), att.reshape(m, -1)], w["w_out"], xf, g[1], g[2])
            rets.append(ret_s)
            krs.append(kr.reshape(b, s, -1))
        else:
            o = layer // 2
            w = odd_w[o]
            qkvo = matmul(h, w["w_in"], BF16, n_cols=shared["n_qkvo"], col_scale=shared["q_scale"])
            gates = matmul(h, w["w_gates"], F32, col_bias=w["b_gates"])
            hh, c, n, mm = mlstm(qkvo.reshape(b, s, -1), gates.reshape(b, s, -1), c0[o], n0[o], m0[o],
                                 gains["mlstm_norm_g"][o])
            xf, h = outproj([hh.reshape(m, -1)], w["w_out"], xf, g[1], g[2])
            cs.append(c)
            ns.append(n)
            ms.append(mm)
        xf, h = xattn_block(h, xf, mem_k[layer], mem_v[layer], shared["w_xq"][layer], shared["w_xo"][layer],
                            g[3], g[4], s)
        g_next = norm_g[layer + 1, 0] if layer + 1 < depth else None
        xf, h = mlp_block(h, xf, shared["w_mlp1"], shared["w_mlp2"], layer, g[5], g_next)
    return (xf.reshape(b, s, d), ckv_slab.reshape(n_even, b, s, -1), jnp.stack(krs), jnp.stack(rets),
            jnp.stack(cs), jnp.stack(ns), jnp.stack(ms))


def kernel(x_prompt, x_sample, cache_mla_ckv, cache_mla_krope, state_ret, state_mlstm_C, state_mlstm_n, state_mlstm_m, cache_mem_k, cache_mem_v, mem_prompt, norm_g, mem_norm_g, w_in_even, mla_q_norm_g, mla_kv_norm_g, w_uq, w_ukv, ret_gn_g, w_out_even, w_in_odd, b_gates_odd, mlstm_norm_g, w_out_odd, w_xq, w_xk, w_xv, w_xo, w_mlp1, w_mlp2):
    even_w, odd_w, shared = _prep_weights(w_in_even, w_uq, w_ukv, w_out_even, w_in_odd, b_gates_odd, w_out_odd,
                                          w_xq, w_xo, w_mlp1, w_mlp2)
    gains = dict(ret_gn_g=ret_gn_g, mla_q_norm_g=mla_q_norm_g, mla_kv_norm_g=mla_kv_norm_g,
                 mlstm_norm_g=mlstm_norm_g)
    depth = norm_g.shape[0]
    b, s, d = x_prompt.shape
    n_even, n_odd = w_in_even.shape[0], w_in_odd.shape[0]
    xw = X_HEADS * X_DH

    bm, t, _ = mem_prompt.shape
    mem_flat = mem_prompt.reshape(bm * t, d)
    p_mem_k, p_mem_v = [], []
    for layer in range(depth):
        mn = rms_rows(mem_flat, mem_norm_g[layer], BF16)
        kv = matmul(mn, jnp.concatenate([w_xk[layer], w_xv[layer]], axis=1).astype(BF16), F32)
        p_mem_k.append(kv[:, :xw].reshape(bm, t, X_HEADS, X_DH))
        p_mem_v.append(kv[:, xw:].reshape(bm, t, X_HEADS, X_DH))
    p_mem_k, p_mem_v = jnp.stack(p_mem_k), jnp.stack(p_mem_v)

    pos_p = jnp.arange(s, dtype=jnp.int32)
    zeros = lambda *shape: jnp.zeros(shape, F32)
    y_prompt, p_ckv, p_kr, p_ret, p_c, p_n, p_m = _run_trunk(
        x_prompt, pos_p, p_mem_k.reshape(depth, bm, t, xw).astype(BF16), p_mem_v.reshape(depth, bm, t, xw).astype(BF16),
        zeros(n_even, b, RET_HEADS, RET_DK, RET_DV), zeros(n_odd, b, M_HEADS, M_DV, M_DK),
        zeros(n_odd, b, M_HEADS, M_DK), zeros(n_odd, b, M_HEADS), None, norm_g, gains, even_w, odd_w, shared)

    db, ds, _ = x_sample.shape
    past_len = cache_mla_ckv.shape[2]
    pos_s = past_len + jnp.arange(ds, dtype=jnp.int32)
    past = (cache_mla_ckv, jnp.pad(cache_mla_krope, ((0, 0), (0, 0), (0, 0), (0, LANES - MLA_ROPE))))
    y_sample, s_ckv, s_kr, s_ret, s_c, s_n, s_m = _run_trunk(
        x_sample, pos_s, cache_mem_k.reshape(depth, db, -1, xw).astype(BF16),
        cache_mem_v.reshape(depth, db, -1, xw).astype(BF16),
        state_ret, state_mlstm_C, state_mlstm_n, state_mlstm_m, past, norm_g, gains, even_w, odd_w, shared)

    return (y_prompt, y_sample, p_ckv, p_kr, p_ret, p_c, p_n, p_m, p_mem_k, p_mem_v,
            s_ckv, s_kr, s_ret, s_c, s_n, s_m)
```

```python
import functools

import jax
import jax.numpy as jnp
from jax import lax
from jax.experimental import pallas as pl
from jax.experimental.pallas import tpu as pltpu

F32 = jnp.float32
BF16 = jnp.bfloat16

RMS_EPS = 1e-6
ROPE_BASE = 10000.0
CHUNK = 64
CHUNK_SHIFT = 6
assert 1 << CHUNK_SHIFT == CHUNK

LOG2E = 1.4426950408889634
RET_HEADS, RET_DK, RET_DV = 8, 128, 128
MLA_HEADS, MLA_NOPE, MLA_ROPE, MLA_V = 8, 128, 64, 128
MLA_Q_LORA, MLA_KV_LORA = 768, 512
MLA_Q_PRESCALE = (MLA_NOPE + MLA_ROPE) ** -0.5 * LOG2E
M_HEADS, M_DK, M_DV = 8, 128, 256
X_HEADS, X_DH = 4, 128
X_Q_PRESCALE = X_DH ** -0.5 * LOG2E

LANES = 128
MXU_DIM = 256
V7X_VMEM_BYTES = 64 * 1024 * 1024
MIB = 1024 * 1024

MLA_QK_PAD = MXU_DIM
KV_PAD_MULTIPLE = MXU_DIM
KV_UNROLL = 2
MLA_HEADS_PER_STEP = 2
ATTN_SINGLE_STEP_ELEMS = 512 * 512
HEAD_GROUPS = (range(0, 4), range(4, 8))
ROW_CHAINS = 2
MIN_ROPE_TABLE_ROWS = 256


def _cparams(semantics, vmem_mib):
    assert vmem_mib * MIB < V7X_VMEM_BYTES
    return pltpu.CompilerParams(dimension_semantics=semantics, vmem_limit_bytes=vmem_mib * MIB)


def _pick(n, cands):
    for c in cands:
        if c <= n and n % c == 0:
            return c
    raise ValueError(f"no tile for {n} in {cands}")


def _rms(x, g):
    ms = jnp.mean(x * x, axis=-1, keepdims=True)
    return x * lax.rsqrt(ms + RMS_EPS) * g


def _sigmoid(x):
    return 1.0 / (1.0 + jnp.exp(-x))


def _dot(a, b):
    return jnp.dot(a, b, preferred_element_type=F32)


def _dot_nt(a, b):
    return lax.dot_general(a, b, (((1,), (1,)), ((), ())), preferred_element_type=F32)


def _dot_tn(a, b):
    return lax.dot_general(a, b, (((0,), (0,)), ((), ())), preferred_element_type=F32)


def _rope128(x, cos2, sin2):
    return x * cos2 + pltpu.roll(x, 64, 1) * sin2


def _rope64(x, cos_p, sin_p):
    return x * cos_p + (pltpu.roll(x, 32, 1) - pltpu.roll(x, 96, 1)) * sin_p


def _norm_kernel(x_ref, g_ref, o_ref):
    o_ref[...] = _rms(x_ref[...].astype(F32), g_ref[...]).astype(o_ref.dtype)


def rms_rows(x, g, out_dtype):
    m, k = x.shape
    tm = _pick(m, (512, 256, 128, 64, 32, 16, 8))
    return pl.pallas_call(
        _norm_kernel,
        grid=(m // tm,),
        in_specs=[pl.BlockSpec((tm, k), lambda i: (i, 0)), pl.BlockSpec((1, k), lambda i: (0, 0))],
        out_specs=pl.BlockSpec((tm, k), lambda i: (i, 0)),
        out_shape=jax.ShapeDtypeStruct((m, k), out_dtype),
        compiler_params=_cparams(("parallel",), 32),
        name="rms_rows",
    )(x, g.reshape(1, k).astype(F32))


def _mm_kernel(*refs, has_scale, has_bias, n_rope, tn):
    it = iter(refs)
    h_ref, w_ref = next(it), next(it)
    scale_ref = next(it) if has_scale else None
    bias_ref = next(it) if has_bias else None
    cos_ref, sin_ref = (next(it), next(it)) if n_rope else (None, None)
    o_ref = next(it)
    acc = _dot(h_ref[...], w_ref[...])
    if has_scale:
        acc = acc * scale_ref[...]
    if has_bias:
        acc = acc + bias_ref[...]
    if n_rope:
        c, s = cos_ref[...], sin_ref[...]
        for t in range(tn // LANES):
            sl = slice(t * LANES, (t + 1) * LANES)
            o_ref[:, sl] = _rope128(acc[:, sl], c, s).astype(o_ref.dtype)
    else:
        o_ref[...] = acc.astype(o_ref.dtype)


def matmul(h, w, out_dtype, *, col0=0, n_cols=None, col_scale=None, col_bias=None, rope=None):
    m, k = h.shape
    n = w.shape[1] - col0 if n_cols is None else n_cols
    tm = _pick(m, (2048, 1024, 512, 256, 128, 64))
    tn = _pick(n, (1024, 512, 256, 128))
    assert col0 % tn == 0 and col0 + n <= w.shape[1]
    jb = col0 // tn
    args = [h, w]
    in_specs = [pl.BlockSpec((tm, k), lambda i, j: (i, 0)), pl.BlockSpec((k, tn), lambda i, j: (0, j + jb))]
    for vec in (col_scale, col_bias):
        if vec is not None:
            args.append(vec.reshape(1, n).astype(F32))
            in_specs.append(pl.BlockSpec((1, tn), lambda i, j: (0, j)))
    if rope is not None:
        r = rope[0].shape[0]
        tm = _pick(m, tuple(c for c in (1024, 512, 256, 128, 64) if r % c == 0))
        in_specs[0] = pl.BlockSpec((tm, k), lambda i, j: (i, 0))
        nrb = r // tm
        for tab in rope:
            args.append(tab)
            in_specs.append(pl.BlockSpec((tm, LANES), lambda i, j: (i % nrb, 0)))
    kern = functools.partial(_mm_kernel, has_scale=col_scale is not None, has_bias=col_bias is not None,
                             n_rope=rope is not None, tn=tn)
    return pl.pallas_call(
        kern,
        grid=(m // tm, n // tn),
        in_specs=in_specs,
        out_specs=pl.BlockSpec((tm, tn), lambda i, j: (i, j)),
        out_shape=jax.ShapeDtypeStruct((m, n), out_dtype),
        compiler_params=_cparams(("parallel", "arbitrary"), 56),
        name="matmul",
    )(*args)


def _kv_expand(ckvn_bf16, krr_bf16, wuk_ref, wuv_ref, kcat_ref, v_ref):
    kn = _dot(ckvn_bf16, wuk_ref[...])
    for hd in range(MLA_HEADS):
        kcat_ref[:, hd * MLA_QK_PAD: hd * MLA_QK_PAD + MLA_NOPE] = (
            kn[:, hd * MLA_NOPE:(hd + 1) * MLA_NOPE].astype(BF16))
        kcat_ref[:, hd * MLA_QK_PAD + MLA_NOPE:(hd + 1) * MLA_QK_PAD] = krr_bf16
    v_ref[...] = _dot(ckvn_bf16, wuv_ref[...]).astype(BF16)


def _even_latent_kernel(h_ref, wlat_ref, gq_ref, gkv_ref, wuq_ref, wuk_ref, wuv_ref, cos_ref, sin_ref, *refs):
    qcat_ref, kcat_ref, v_ref, ckv_ref, kr_ref = refs[-5:]
    lat = _dot(h_ref[...], wlat_ref[...])
    cq = lat[:, :MLA_Q_LORA]
    ckv = lat[:, MLA_Q_LORA:MLA_Q_LORA + MLA_KV_LORA]
    krp = lat[:, MLA_Q_LORA + MLA_KV_LORA:]
    c, s = cos_ref[...], sin_ref[...]
    q = _dot(_rms(cq, gq_ref[...]).astype(BF16), wuq_ref[...]) * MLA_Q_PRESCALE
    nope_w = MLA_HEADS * MLA_NOPE
    for hd in range(MLA_HEADS):
        qcat_ref[:, hd * MLA_QK_PAD: hd * MLA_QK_PAD + MLA_NOPE] = (
            q[:, hd * MLA_NOPE:(hd + 1) * MLA_NOPE].astype(BF16))
        qr = q[:, nope_w + hd * LANES: nope_w + (hd + 1) * LANES]
        qcat_ref[:, hd * MLA_QK_PAD + MLA_NOPE:(hd + 1) * MLA_QK_PAD] = _rope64(qr, c, s).astype(BF16)
    ckvn = _rms(ckv, gkv_ref[...])
    ckv_ref[...] = ckvn
    krr = _rope64(krp, c, s)
    kr_ref[...] = krr[:, :MLA_ROPE]
    _kv_expand(ckvn.astype(BF16), krr.astype(BF16), wuk_ref, wuv_ref, kcat_ref, v_ref)


def even_latent(h, wlat, gq, gkv, wuq, wuk, wuv, rope64, e, n_even, ckv_slab):
    m, d = h.shape
    r = rope64[0].shape[0]
    tm = _pick(m, tuple(c for c in (512, 256, 128, 64) if r % c == 0))
    nrb = r // tm
    full = lambda a: pl.BlockSpec(a.shape, lambda i: (0,) * a.ndim)
    row = lambda w: pl.BlockSpec((tm, w), lambda i: (i, 0))
    tab = pl.BlockSpec((tm, LANES), lambda i: (i % nrb, 0))
    qk_w = MLA_HEADS * MLA_QK_PAD
    v_w = MLA_HEADS * MLA_V
    args = [h, wlat, gq, gkv, wuq, wuk, wuv, rope64[0], rope64[1]]
    in_specs = [row(d), full(wlat), full(gq), full(gkv), full(wuq), full(wuk), full(wuv), tab, tab]
    aliases = {}
    if ckv_slab is not None:
        aliases = {len(args): 3}
        args.append(ckv_slab)
        in_specs.append(pl.BlockSpec(memory_space=pl.ANY))
    return pl.pallas_call(
        _even_latent_kernel,
        grid=(m // tm,),
        in_specs=in_specs,
        out_specs=[row(qk_w), row(qk_w), row(v_w),
                   pl.BlockSpec((None, tm, MLA_KV_LORA), lambda i: (e, i, 0)), row(MLA_ROPE)],
        out_shape=[jax.ShapeDtypeStruct((m, qk_w), BF16), jax.ShapeDtypeStruct((m, qk_w), BF16),
                   jax.ShapeDtypeStruct((m, v_w), BF16), jax.ShapeDtypeStruct((n_even, m, MLA_KV_LORA), F32),
                   jax.ShapeDtypeStruct((m, MLA_ROPE), F32)],
        input_output_aliases=aliases,
        compiler_params=_cparams(("parallel",), 48),
        name="even_latent",
    )(*args)


def _past_kv_kernel(ckv_ref, krp_ref, wuk_ref, wuv_ref, kcat_ref, v_ref):
    _kv_expand(ckv_ref[...].astype(BF16), krp_ref[...].astype(BF16), wuk_ref, wuv_ref, kcat_ref, v_ref)


def past_kv(ckvn, kr_pad, e, wuk, wuv, skp):
    _, b, p_len, _ = ckvn.shape
    tm = _pick(p_len, (512, 256, 128, 64))
    full = lambda a: pl.BlockSpec(a.shape, lambda bi, i: (0,) * a.ndim)
    src = lambda w: pl.BlockSpec((None, None, tm, w), lambda bi, i: (e, bi, i, 0))
    dst = lambda w: pl.BlockSpec((None, tm, w), lambda bi, i: (bi, i, 0))
    qk_w = MLA_HEADS * MLA_QK_PAD
    v_w = MLA_HEADS * MLA_V
    return pl.pallas_call(
        _past_kv_kernel,
        grid=(b, p_len // tm),
        in_specs=[src(MLA_KV_LORA), src(LANES), full(wuk), full(wuv)],
        out_specs=[dst(qk_w), dst(v_w)],
        out_shape=[jax.ShapeDtypeStruct((b, skp, qk_w), BF16), jax.ShapeDtypeStruct((b, skp, v_w), BF16)],
        compiler_params=_cparams(("parallel", "parallel"), 32),
        name="past_kv",
    )(ckvn, kr_pad, wuk, wuv)


def _mla_kernel(q_ref, k_ref, v_ref, o_ref, m_scr, l_scr, acc_scr, a_scr, p_scr, *, tq, tk, hb, q_off, sk_valid):
    qi = pl.program_id(2)
    q0 = qi * tq
    cq_lo = (q_off + q0) // CHUNK
    cq_hi = (q_off + q0 + tq - 1) // CHUNK
    n_full = jnp.minimum(lax.div((cq_lo + 1) * CHUNK, tk), sk_valid // tk)
    n_vis = lax.div(jnp.minimum((cq_hi + 1) * CHUNK, sk_valid) + tk - 1, tk)
    m_scr[...] = jnp.full(m_scr.shape, -1e30, F32)
    l_scr[...] = jnp.zeros(l_scr.shape, F32)
    acc_scr[...] = jnp.zeros(acc_scr.shape, F32)

    def apply_prev(j, ki_prev):
        v = v_ref[0, pl.ds(pl.multiple_of(ki_prev * tk, tk), tk), j * MLA_V:(j + 1) * MLA_V]
        acc_scr[j] = a_scr[j] * acc_scr[j] + _dot(p_scr[j], v)

    def step(ki, masked, first=False):
        k0 = pl.multiple_of(ki * tk, tk)
        if masked:
            qpos = q_off + q0 + lax.broadcasted_iota(jnp.int32, (tq, tk), 0)
            kpos = k0 + lax.broadcasted_iota(jnp.int32, (tq, tk), 1)
            vis = jnp.logical_and((kpos >> CHUNK_SHIFT) <= (qpos >> CHUNK_SHIFT), kpos < sk_valid)
        for j in range(hb):
            q = q_ref[0, :, j * MLA_QK_PAD:(j + 1) * MLA_QK_PAD]
            k = k_ref[0, pl.ds(k0, tk), j * MLA_QK_PAD:(j + 1) * MLA_QK_PAD]
            s = _dot_nt(q, k)
            if not first:
                apply_prev(j, ki - 1)
            if masked:
                s = jnp.where(vis, s, -jnp.inf)
            m_prev = m_scr[j]
            m_new = jnp.maximum(m_prev, jnp.max(s, axis=-1, keepdims=True))
            alpha = jnp.exp2(m_prev - m_new)
            p = [jnp.exp2(s[:, c * LANES:(c + 1) * LANES] - m_new) for c in range(tk // LANES)]
            l_scr[j] = alpha * l_scr[j] + sum(p[1:], p[0])
            p_scr[j] = jnp.concatenate([pc.astype(BF16) for pc in p], axis=-1)
            a_scr[j] = alpha
            m_scr[j] = m_new

    def body_full(ki, carry):
        step(ki, False)
        return carry

    def body_masked(ki, carry):
        step(ki, True)
        return carry

    def body_full_group(kg, carry):
        for u in range(KV_UNROLL):
            step(1 + KV_UNROLL * kg + u, False)
        return carry

    step(0, True, first=True)
    n_full1 = jnp.maximum(n_full, 1)
    n_groups = lax.div(n_full1 - 1, KV_UNROLL)
    lax.fori_loop(0, n_groups, body_full_group, 0)
    lax.fori_loop(1 + KV_UNROLL * n_groups, n_full1, body_full, 0)
    lax.fori_loop(n_full1, n_vis, body_masked, 0)
    for j in range(hb):
        apply_prev(j, n_vis - 1)
        l = jnp.sum(l_scr[j], axis=-1, keepdims=True)
        o_ref[0, :, j * MLA_V:(j + 1) * MLA_V] = (acc_scr[j] / l).astype(o_ref.dtype)


def mla_attention(qcat, kcat, v, *, q_off, sk_valid):
    b, sq, _ = qcat.shape
    skp = kcat.shape[1]
    tq = _pick(sq, (512, 256, 128, 64))
    tk = skp if tq * skp <= ATTN_SINGLE_STEP_ELEMS else _pick(skp, (512, 256, 128))
    hb = MLA_HEADS if tk == skp else MLA_HEADS_PER_STEP
    kern = functools.partial(_mla_kernel, tq=tq, tk=tk, hb=hb, q_off=q_off, sk_valid=sk_valid)
    return pl.pallas_call(
        kern,
        grid=(b, MLA_HEADS // hb, sq // tq),
        in_specs=[pl.BlockSpec((1, tq, hb * MLA_QK_PAD), lambda bi, h, qi: (bi, qi, h)),
                  pl.BlockSpec((1, skp, hb * MLA_QK_PAD), lambda bi, h, qi: (bi, 0, h)),
                  pl.BlockSpec((1, skp, hb * MLA_V), lambda bi, h, qi: (bi, 0, h))],
        out_specs=pl.BlockSpec((1, tq, hb * MLA_V), lambda bi, h, qi: (bi, qi, h)),
        out_shape=jax.ShapeDtypeStruct((b, sq, MLA_HEADS * MLA_V), BF16),
        scratch_shapes=[pltpu.VMEM((hb, tq, LANES), F32), pltpu.VMEM((hb, tq, LANES), F32),
                        pltpu.VMEM((hb, tq, MLA_V), F32), pltpu.VMEM((hb, tq, LANES), F32),
                        pltpu.VMEM((hb, tq, tk), BF16)],
        compiler_params=_cparams(("parallel", "parallel", "arbitrary"), 48),
        name="mla_attention",
    )(qcat, kcat, v)


def _retention_kernel(lg_ref, qk_ref, vg_ref, s0_ref, gn_ref, out_ref, sfin_ref,
                      state_scr, decay_scr, qdec_scr, kdec_scr, *, lc):
    ci = pl.program_id(1)
    heads = range(RET_HEADS)

    @pl.when(ci == 0)
    def _():
        state_scr[...] = s0_ref[0]
        ti = lax.broadcasted_iota(jnp.int32, (lc, lc), 0)
        si = lax.broadcasted_iota(jnp.int32, (lc, lc), 1)
        dpos = jnp.maximum(ti - si, 0).astype(F32)
        trow = lax.broadcasted_iota(jnp.int32, (lc, LANES), 0).astype(F32)
        for h in heads:
            lg = lg_ref[h:h + 1, :]
            decay_scr[h] = jnp.where(ti >= si, jnp.exp(lg[:, 0:1] * dpos), 0.0)
            qdec_scr[h] = jnp.exp(lg * (trow + 1.0))
            kdec_scr[h] = jnp.exp(lg * (lc - 1.0 - trow))

    qk_w = RET_HEADS * RET_DK
    v_w = RET_HEADS * RET_DV
    q = [qk_ref[0, :, h * RET_DK:(h + 1) * RET_DK] for h in heads]
    k = [qk_ref[0, :, qk_w + h * RET_DK: qk_w + (h + 1) * RET_DK] for h in heads]
    v = [vg_ref[0, :, h * RET_DV:(h + 1) * RET_DV] for h in heads]
    st = [state_scr[h] for h in heads]
    sc = [_dot_nt(q[h], k[h]) for h in heads]
    inter = [_dot(q[h], st[h].astype(BF16)) for h in heads]
    scd = [(sc[h] * decay_scr[h]).astype(BF16) for h in heads]
    kd = [(k[h].astype(F32) * kdec_scr[h]).astype(BF16) for h in heads]
    intra = [_dot(scd[h], v[h]) for h in heads]
    upd = [_dot_tn(kd[h], v[h]) for h in heads]
    for h in heads:
        state_scr[h] = jnp.exp(lg_ref[h:h + 1, :] * lc) * st[h] + upd[h]
        out = intra[h] + qdec_scr[h] * inter[h]
        g = vg_ref[0, :, v_w + h * RET_DV: v_w + (h + 1) * RET_DV].astype(F32)
        xc = out - jnp.mean(out, axis=-1, keepdims=True)
        y = xc * lax.rsqrt(jnp.mean(xc * xc, axis=-1, keepdims=True) + RMS_EPS)
        y = y * gn_ref[:, h * RET_DV:(h + 1) * RET_DV]
        out_ref[0, :, h * RET_DV:(h + 1) * RET_DV] = (g * _sigmoid(g) * y).astype(out_ref.dtype)

    @pl.when(ci == pl.num_programs(1) - 1)
    def _():
        sfin_ref[0] = state_scr[...]


def retention(qk, vg, s0, gn_g):
    b, s, _ = qk.shape
    lc = _pick(s, (256, 128, 64))
    log_g = jnp.log1p(-jnp.exp2(-5.0 - jnp.arange(RET_HEADS, dtype=F32)))
    lg_tab = jnp.broadcast_to(log_g[:, None], (RET_HEADS, LANES))
    v_w = RET_HEADS * RET_DV
    kern = functools.partial(_retention_kernel, lc=lc)
    st_spec = pl.BlockSpec((1, RET_HEADS, RET_DK, RET_DV), lambda bi, ci: (bi, 0, 0, 0))
    return pl.pallas_call(
        kern,
        grid=(b, s // lc),
        in_specs=[pl.BlockSpec((RET_HEADS, LANES), lambda bi, ci: (0, 0)),
                  pl.BlockSpec((1, lc, qk.shape[2]), lambda bi, ci: (bi, ci, 0)),
                  pl.BlockSpec((1, lc, vg.shape[2]), lambda bi, ci: (bi, ci, 0)),
                  st_spec,
                  pl.BlockSpec((1, v_w), lambda bi, ci: (0, 0))],
        out_specs=[pl.BlockSpec((1, lc, v_w), lambda bi, ci: (bi, ci, 0)), st_spec],
        out_shape=[jax.ShapeDtypeStruct((b, s, v_w), BF16),
                   jax.ShapeDtypeStruct((b, RET_HEADS, RET_DK, RET_DV), F32)],
        scratch_shapes=[pltpu.VMEM((RET_HEADS, RET_DK, RET_DV), F32), pltpu.VMEM((RET_HEADS, lc, lc), F32),
                        pltpu.VMEM((RET_HEADS, lc, LANES), F32), pltpu.VMEM((RET_HEADS, lc, LANES), F32)],
        compiler_params=_cparams(("parallel", "arbitrary"), 40),
        name="retention",
    )(lg_tab, qk, vg, s0, gn_g.reshape(1, v_w).astype(F32))


def _split3(x):
    hi = x.astype(BF16)
    r1 = x - hi.astype(F32)
    mid = r1.astype(BF16)
    lo = (r1 - mid.astype(F32)).astype(BF16)
    return hi, mid, lo


def _log_sigmoid(x):
    return jnp.minimum(x, 0.0) - jnp.log(1.0 + jnp.exp(-jnp.abs(x)))


def _widen(x, width):
    if width < LANES:
        return x[:, :width]
    return x if width == LANES else jnp.concatenate([x] * (width // LANES), axis=1)


def _mlstm_kernel(qkvo_ref, gates_ref, gates_t_ref, c0_ref, n0_ref, m0_ref, ng_ref,
                  out_ref, cfin_ref, nfin_ref, mfin_ref, c_scr, n_scr, m_scr, *, lc):
    ci = pl.program_id(1)

    @pl.when(ci == 0)
    def _():
        c_scr[...] = c0_ref[0]
        n_scr[...] = n0_ref[0]
        m_scr[...] = m0_ref[0]

    gates = gates_ref[0]
    gates_t = gates_t_ref[0]
    ti = lax.broadcasted_iota(jnp.int32, (lc, lc), 0)
    si = lax.broadcasted_iota(jnp.int32, (lc, lc), 1)
    tril = si <= ti
    ones_l = jnp.where(tril, 1.0, 0.0).astype(BF16)
    ones_u = jnp.where(ti <= si, 1.0, 0.0).astype(BF16)
    b_col = sum(_dot(ones_l, part) for part in _split3(_log_sigmoid(gates)))
    b_row = sum(_dot(part, ones_u) for part in _split3(_log_sigmoid(gates_t)))[M_HEADS:2 * M_HEADS]
    i_row = gates_t[0:M_HEADS]
    qk_w = M_HEADS * M_DK
    v_w = M_HEADS * M_DV
    for heads in HEAD_GROUPS:
        q = {h: qkvo_ref[0, :, h * M_DK:(h + 1) * M_DK] for h in heads}
        k = {h: qkvo_ref[0, :, qk_w + h * M_DK: qk_w + (h + 1) * M_DK] for h in heads}
        v = {h: qkvo_ref[0, :, 2 * qk_w + h * M_DV: 2 * qk_w + (h + 1) * M_DV] for h in heads}
        c_st = {h: c_scr[h] for h in heads}
        n_st = {h: n_scr[h:h + 1, :] for h in heads}
        m_prev = {h: m_scr[h:h + 1, :] for h in heads}
        qk = {h: _dot_nt(q[h], k[h]) for h in heads}
        qc = {h: _dot(q[h], c_st[h].astype(BF16)) for h in heads}
        bt = {h: jnp.broadcast_to(b_col[:, M_HEADS + h:M_HEADS + h + 1], (lc, LANES)) for h in heads}
        i_t = {h: jnp.broadcast_to(gates[:, h:h + 1], (lc, LANES)) for h in heads}
        m_t, w_inter, sc = {}, {}, {}
        for h in heads:
            log_intra = jnp.where(tril, _widen(bt[h], lc) - b_row[h:h + 1, :] + i_row[h:h + 1, :], -jnp.inf)
            log_inter = bt[h] + m_prev[h]
            m_t[h] = jnp.maximum(log_inter, jnp.max(log_intra, axis=-1, keepdims=True))
            w_inter[h] = jnp.exp(log_inter - m_t[h])
            sc[h] = qk[h] * jnp.exp(log_intra - _widen(m_t[h], lc))
        num_intra = {h: _dot(sc[h].astype(BF16), v[h]) for h in heads}
        kw = {}
        for h in heads:
            m_new = m_t[h][lc - 1:lc, :]
            b_last = bt[h][lc - 1:lc, :]
            w_state = jnp.exp(b_last + m_prev[h] - m_new)
            kw[h] = k[h].astype(F32) * jnp.exp(b_last - bt[h] + i_t[h] - m_new)
            n_scr[h:h + 1, :] = w_state * n_st[h] + jnp.sum(kw[h], axis=0, keepdims=True)
            m_scr[h:h + 1, :] = m_new
            c_scr[h] = _widen(w_state, M_DV) * c_st[h] + _dot_tn(kw[h].astype(BF16), v[h])
        for h in heads:
            num = num_intra[h] + _widen(w_inter[h], M_DV) * qc[h]
            den = (jnp.sum(sc[h], axis=-1, keepdims=True)
                   + w_inter[h] * jnp.sum(q[h].astype(F32) * n_st[h], axis=-1, keepdims=True))
            hh = num * _widen(1.0 / jnp.maximum(jnp.abs(den), jnp.exp(-m_t[h])), M_DV)
            y = hh * lax.rsqrt(jnp.mean(hh * hh, axis=-1, keepdims=True) + RMS_EPS)
            y = y * ng_ref[:, h * M_DV:(h + 1) * M_DV]
            og = qkvo_ref[0, :, 2 * qk_w + v_w + h * M_DV: 2 * qk_w + v_w + (h + 1) * M_DV].astype(F32)
            out_ref[0, :, h * M_DV:(h + 1) * M_DV] = (y * _sigmoid(og)).astype(out_ref.dtype)

    @pl.when(ci == pl.num_programs(1) - 1)
    def _():
        cfin_ref[0] = c_scr[...]
        nfin_ref[0] = n_scr[...]
        mfin_ref[0] = m_scr[...]


def mlstm(qkvo, gates, c0, n0, m0, norm_g):
    b, s, w = qkvo.shape
    lc = _pick(s, (256, 128, 64))
    v_w = M_HEADS * M_DV
    gates_t = jnp.swapaxes(gates[:, :, :2 * M_HEADS], 1, 2)
    kern = functools.partial(_mlstm_kernel, lc=lc)
    c_spec = pl.BlockSpec((1, M_HEADS, M_DK, M_DV), lambda bi, ci: (bi, 0, 0, 0))
    n_spec = pl.BlockSpec((1, M_HEADS, M_DK), lambda bi, ci: (bi, 0, 0))
    m_spec = pl.BlockSpec((1, M_HEADS, LANES), lambda bi, ci: (bi, 0, 0))
    hh, c_t, n, m = pl.pallas_call(
        kern,
        grid=(b, s // lc),
        in_specs=[pl.BlockSpec((1, lc, w), lambda bi, ci: (bi, ci, 0)),
                  pl.BlockSpec((1, lc, LANES), lambda bi, ci: (bi, ci, 0)),
                  pl.BlockSpec((1, 2 * M_HEADS, lc), lambda bi, ci: (bi, 0, ci)),
                  c_spec, n_spec, m_spec,
                  pl.BlockSpec((1, v_w), lambda bi, ci: (0, 0))],
        out_specs=[pl.BlockSpec((1, lc, v_w), lambda bi, ci: (bi, ci, 0)), c_spec, n_spec, m_spec],
        out_shape=[jax.ShapeDtypeStruct((b, s, v_w), BF16),
                   jax.ShapeDtypeStruct((b, M_HEADS, M_DK, M_DV), F32),
                   jax.ShapeDtypeStruct((b, M_HEADS, M_DK), F32),
                   jax.ShapeDtypeStruct((b, M_HEADS, LANES), F32)],
        scratch_shapes=[pltpu.VMEM((M_HEADS, M_DK, M_DV), F32), pltpu.VMEM((M_HEADS, M_DK), F32),
                        pltpu.VMEM((M_HEADS, LANES), F32)],
        compiler_params=_cparams(("parallel", "arbitrary"), 48),
        name="mlstm",
    )(qkvo, gates, gates_t, jnp.swapaxes(c0, 2, 3), n0, jnp.broadcast_to(m0[..., None], m0.shape + (LANES,)),
      norm_g.reshape(1, v_w).astype(F32))
    return hh, jnp.swapaxes(c_t, 2, 3), n, m[..., 0]


def _residual_out(acc, x_ref, gp_ref, gn_ref, xo_ref, ho_ref, rows=slice(None)):
    xn = x_ref[rows, :] + _rms(acc, gp_ref[...])
    xo_ref[rows, :] = xn
    if ho_ref is not None:
        ho_ref[rows, :] = _rms(xn, gn_ref[...]).astype(ho_ref.dtype)


def _row_chains(tm):
    n = ROW_CHAINS if tm % (ROW_CHAINS * 128) == 0 else 1
    return [slice(c * (tm // n), (c + 1) * (tm // n)) for c in range(n)]


def _outproj_kernel(*refs, n_in):
    a_refs, w_refs = refs[:n_in], refs[n_in:2 * n_in]
    x_ref, gp_ref, gn_ref, xo_ref, ho_ref = refs[2 * n_in:]
    for rows in _row_chains(x_ref.shape[0]):
        acc = _dot(a_refs[0][rows, :], w_refs[0][...])
        for a_ref, w_ref in zip(a_refs[1:], w_refs[1:]):
            acc = acc + _dot(a_ref[rows, :], w_ref[...])
        _residual_out(acc, x_ref, gp_ref, gn_ref, xo_ref, ho_ref, rows)


def outproj(acts, w, x, g_post, g_next):
    m, d = x.shape
    tm = _pick(m, (512, 256, 128, 64))
    n_in = len(acts)
    row = lambda wd: pl.BlockSpec((tm, wd), lambda i: (i, 0))
    vec = pl.BlockSpec((1, d), lambda i: (0, 0))
    w_specs, r0 = [], 0
    for a in acts:
        ka = a.shape[1]
        assert r0 % ka == 0
        w_specs.append(pl.BlockSpec((ka, d), functools.partial(lambda blk, i: (blk, 0), r0 // ka)))
        r0 += ka
    assert r0 == w.shape[0]
    return pl.pallas_call(
        functools.partial(_outproj_kernel, n_in=n_in),
        grid=(m // tm,),
        in_specs=[row(a.shape[1]) for a in acts] + w_specs + [row(d), vec, vec],
        out_specs=[row(d), row(d)],
        out_shape=[jax.ShapeDtypeStruct((m, d), F32), jax.ShapeDtypeStruct((m, d), BF16)],
        compiler_params=_cparams(("parallel",), 56),
        name="outproj",
    )(*acts, *([w] * n_in), x, g_post.reshape(1, d), g_next.reshape(1, d))


def _xattn_kernel(h_ref, wq_ref, mk_ref, mv_ref, wo_ref, x_ref, gp_ref, gn_ref, xo_ref, ho_ref):
    mk, mv = mk_ref[0], mv_ref[0]
    t = mk.shape[0]
    chains = _row_chains(x_ref.shape[0])
    heads = [slice(hd * X_DH, (hd + 1) * X_DH) for hd in range(X_HEADS)]
    q = [(_dot(h_ref[rows, :], wq_ref[...]) * X_Q_PRESCALE).astype(BF16) for rows in chains]
    s = [[_dot_nt(qc[:, sl], mk[:, sl]) for sl in heads] for qc in q]
    p, inv_l = [], []
    for sc in s:
        pc, lc = [], []
        for sh in sc:
            e = jnp.exp2(sh - jnp.max(sh, axis=-1, keepdims=True))
            part = e[:, :LANES]
            for c in range(1, t // LANES):
                part = part + e[:, c * LANES:(c + 1) * LANES]
            pc.append(e.astype(BF16))
            lc.append(1.0 / jnp.sum(part, axis=-1, keepdims=True))
        p.append(pc)
        inv_l.append(lc)
    o = [jnp.concatenate([(_dot(pc[hd], mv[:, heads[hd]]) * lc[hd]).astype(BF16) for hd in range(X_HEADS)], axis=-1)
         for pc, lc in zip(p, inv_l)]
    acc = [_dot(oc, wo_ref[...]) for oc in o]
    for rows, ac in zip(chains, acc):
        _residual_out(ac, x_ref, gp_ref, gn_ref, xo_ref, ho_ref, rows)


def xattn_block(h, x, mem_k, mem_v, wq, wo, g_post, g_next, seq):
    m, d = x.shape
    tm = _pick(seq, (512, 256, 128, 64))
    per_b = seq // tm
    t, xw = mem_k.shape[1], mem_k.shape[2]
    full = lambda a: pl.BlockSpec(a.shape, lambda i: (0,) * a.ndim)
    row = lambda w: pl.BlockSpec((tm, w), lambda i: (i, 0))
    mem = pl.BlockSpec((1, t, xw), lambda i: (i // per_b, 0, 0))
    vec = pl.BlockSpec((1, d), lambda i: (0, 0))
    return pl.pallas_call(
        _xattn_kernel,
        grid=(m // tm,),
        in_specs=[row(d), full(wq), mem, mem, full(wo), row(d), vec, vec],
        out_specs=[row(d), row(d)],
        out_shape=[jax.ShapeDtypeStruct((m, d), F32), jax.ShapeDtypeStruct((m, d), BF16)],
        compiler_params=_cparams(("parallel",), 48),
        name="xattn_block",
    )(h, wq, mem_k, mem_v, wo, x, g_post.reshape(1, d), g_next.reshape(1, d))


def _mlp_kernel(h_ref, w1_ref, w2_ref, x_ref, gp_ref, gn_ref, xo_ref, *rest, emit_h):
    ho_ref, acc_scr = (rest[0], rest[1]) if emit_h else (None, rest[0])
    f = pl.program_id(1)
    last = pl.num_programs(1) - 1

    def partial_product():
        a = jnp.maximum(_dot(h_ref[...], w1_ref[...]), 0.0)
        return _dot((a * a).astype(BF16), w2_ref[...])

    @pl.when(f == 0)
    def _():
        acc_scr[...] = partial_product()

    @pl.when(jnp.logical_and(f > 0, f < last))
    def _():
        acc_scr[...] += partial_product()

    @pl.when(jnp.logical_and(f > 0, f == last))
    def _():
        _residual_out(acc_scr[...] + partial_product(), x_ref, gp_ref, gn_ref, xo_ref, ho_ref)


def mlp_block(h, x, w1, w2, layer, g_post, g_next):
    m, d = x.shape
    ff = w1.shape[2]
    tm = _pick(m, (512, 256, 128, 64))
    tf = _pick(ff, (1024, 512, 256, 128))
    assert ff // tf >= 2
    emit_h = g_next is not None
    row = pl.BlockSpec((tm, d), lambda i, f: (i, 0))
    vec = pl.BlockSpec((1, d), lambda i, f: (0, 0))
    out_specs = [row] + ([row] if emit_h else [])
    out_shape = [jax.ShapeDtypeStruct((m, d), F32)] + ([jax.ShapeDtypeStruct((m, d), BF16)] if emit_h else [])
    g_n = (g_next if emit_h else g_post).reshape(1, d)
    res = pl.pallas_call(
        functools.partial(_mlp_kernel, emit_h=emit_h),
        grid=(m // tm, ff // tf),
        in_specs=[row, pl.BlockSpec((None, d, tf), lambda i, f: (layer, 0, f)),
                  pl.BlockSpec((None, tf, d), lambda i, f: (layer, f, 0)), row, vec, vec],
        out_specs=out_specs,
        out_shape=out_shape,
        scratch_shapes=[pltpu.VMEM((tm, d), F32)],
        compiler_params=_cparams(("parallel", "arbitrary"), 56),
        name="mlp_block",
    )(h, w1, w2, x, g_post.reshape(1, d), g_n)
    return (res[0], res[1]) if emit_h else (res[0], None)


def _rope_tables(pos):
    def angles(half):
        inv = jnp.power(ROPE_BASE, -jnp.arange(half, dtype=F32) / half)
        return pos.astype(F32)[:, None] * inv[None, :]
    a = angles(RET_DK // 2)
    c, s = jnp.cos(a), jnp.sin(a)
    t128 = (jnp.concatenate([c, c], axis=1), jnp.concatenate([-s, s], axis=1))
    a = angles(MLA_ROPE // 2)
    c, s = jnp.cos(a), jnp.sin(a)
    z = jnp.zeros((pos.shape[0], LANES - MLA_ROPE), F32)
    t64 = (jnp.concatenate([c, c, z], axis=1), jnp.concatenate([s, s, z], axis=1))
    return t128, t64


def _prep_weights(w_in_even, w_uq, w_ukv, w_out_even, w_in_odd, b_gates_odd, w_out_odd, w_xq, w_xo, w_mlp1, w_mlp2):
    d = w_in_even.shape[1]
    ret_w = RET_HEADS * RET_DK
    n_qkvg = 2 * ret_w + 2 * RET_HEADS * RET_DV
    n_lat = MLA_Q_LORA + MLA_KV_LORA + MLA_ROPE
    even = []
    for e in range(w_in_even.shape[0]):
        w = w_in_even[e].astype(BF16)
        wlat = jnp.pad(w[:, n_qkvg:n_qkvg + n_lat], ((0, 0), (0, LANES - MLA_ROPE)))
        uq = w_uq[e].reshape(MLA_Q_LORA, MLA_HEADS, MLA_NOPE + MLA_ROPE)
        uq_nope = uq[:, :, :MLA_NOPE].reshape(MLA_Q_LORA, MLA_HEADS * MLA_NOPE)
        uq_rope = jnp.pad(uq[:, :, MLA_NOPE:], ((0, 0), (0, 0), (0, LANES - MLA_ROPE)))
        uq_rope = uq_rope.reshape(MLA_Q_LORA, MLA_HEADS * LANES)
        ukv = w_ukv[e].reshape(MLA_KV_LORA, MLA_HEADS, MLA_NOPE + MLA_V)
        even.append(dict(
            w_in=w,
            w_lat=wlat,
            w_uq=jnp.concatenate([uq_nope, uq_rope], axis=1).astype(BF16),
            w_uk=ukv[:, :, :MLA_NOPE].reshape(MLA_KV_LORA, MLA_HEADS * MLA_NOPE).astype(BF16),
            w_uv=ukv[:, :, MLA_NOPE:].reshape(MLA_KV_LORA, MLA_HEADS * MLA_V).astype(BF16),
            w_out=w_out_even[e].astype(BF16),
        ))
    m_qk = M_HEADS * M_DK
    n_qkvo = 2 * m_qk + 2 * M_HEADS * M_DV
    odd = []
    for o in range(w_in_odd.shape[0]):
        w = w_in_odd[o].astype(BF16)
        odd.append(dict(
            w_in=w,
            w_gates=jnp.pad(w[:, n_qkvo:], ((0, 0), (0, LANES - 2 * M_HEADS))),
            b_gates=jnp.pad(b_gates_odd[o], (0, LANES - 2 * M_HEADS)),
            w_out=w_out_odd[o].astype(BF16),
        ))
    q_scale = jnp.concatenate([jnp.full((m_qk,), M_DK ** -0.5, F32), jnp.ones((n_qkvo - m_qk,), F32)])
    rk_scale = jnp.concatenate([jnp.ones((ret_w,), F32), jnp.full((ret_w,), RET_DK ** -0.5, F32)])
    shared = dict(w_xq=w_xq.astype(BF16), w_xo=w_xo.astype(BF16), w_mlp1=w_mlp1.astype(BF16),
                  w_mlp2=w_mlp2.astype(BF16), q_scale=q_scale, rk_scale=rk_scale, d=d,
                  n_qk=2 * ret_w, n_vg=n_qkvg - 2 * ret_w, n_qkvo=n_qkvo)
    return even, odd, shared


def _run_trunk(x, pos, mem_k, mem_v, ret_s0, c0, n0, m0, past, norm_g, gains, even_w, odd_w, shared):
    b, s, d = x.shape
    m = b * s
    depth = norm_g.shape[0]
    t128, t64 = _rope_tables(pos)
    if m % s or s % 64:
        raise ValueError("unsupported sequence length")
    if s < MIN_ROPE_TABLE_ROWS:
        t128 = tuple(jnp.tile(t, (b, 1)) for t in t128)
        t64 = tuple(jnp.tile(t, (b, 1)) for t in t64)
    xf = x.reshape(m, d)
    h = rms_rows(xf, norm_g[0, 0], BF16)
    n_even = len(even_w)
    ckv_slab = None
    krs, rets, cs, ns, ms = [], [], [], [], []
    for layer in range(depth):
        g = norm_g[layer]
        if layer % 2 == 0:
            e = layer // 2
            w = even_w[e]
            qk = matmul(h, w["w_in"], BF16, n_cols=shared["n_qk"], col_scale=shared["rk_scale"], rope=t128)
            vg = matmul(h, w["w_in"], BF16, col0=shared["n_qk"], n_cols=shared["n_vg"])
            ret, ret_s = retention(qk.reshape(b, s, -1), vg.reshape(b, s, -1), ret_s0[e], gains["ret_gn_g"][e])
            qcat, kcat, v, ckv_slab, kr = even_latent(h, w["w_lat"], gains["mla_q_norm_g"][e].reshape(1, -1),
                                                      gains["mla_kv_norm_g"][e].reshape(1, -1),
                                                      w["w_uq"], w["w_uk"], w["w_uv"], t64, e, n_even, ckv_slab)
            kcat = kcat.reshape(b, s, -1)
            v = v.reshape(b, s, -1)
            if past is None:
                q_off, sk = 0, s
            else:
                p_ckv, p_krp = past
                p_len = p_ckv.shape[2]
                q_off, sk = p_len, p_len + s
                pad = (-sk) % KV_PAD_MULTIPLE
                pk, pv = past_kv(p_ckv, p_krp, e, w["w_uk"], w["w_uv"], sk + pad)
                tail = lambda new: jnp.concatenate([new, jnp.zeros((b, pad, new.shape[-1]), BF16)], axis=1)
                kcat = lax.dynamic_update_slice(pk, tail(kcat), (0, p_len, 0))
                v = lax.dynamic_update_slice(pv, tail(v), (0, p_len, 0))
            att = mla_attention(qcat.reshape(b, s, -1), kcat, v, q_off=q_off, sk_valid=sk)
            xf, h = outproj([ret.reshape(m, -1), att.reshape(m, -1)], w["w_out"], xf, g[1], g[2])
            rets.append(ret_s)
            krs.append(kr.reshape(b, s, -1))
        else:
            o = layer // 2
            w = odd_w[o]
            qkvo = matmul(h, w["w_in"], BF16, n_cols=shared["n_qkvo"], col_scale=shared["q_scale"])
            gates = matmul(h, w["w_gates"], F32, col_bias=w["b_gates"])
            hh, c, n, mm = mlstm(qkvo.reshape(b, s, -1), gates.reshape(b, s, -1), c0[o], n0[o], m0[o],
                                 gains["mlstm_norm_g"][o])
            xf, h = outproj([hh.reshape(m, -1)], w["w_out"], xf, g[1], g[2])
            cs.append(c)
            ns.append(n)
            ms.append(mm)
        xf, h = xattn_block(h, xf, mem_k[layer], mem_v[layer], shared["w_xq"][layer], shared["w_xo"][layer],
                            g[3], g[4], s)
        g_next = norm_g[layer + 1, 0] if layer + 1 < depth else None
        xf, h = mlp_block(h, xf, shared["w_mlp1"], shared["w_mlp2"], layer, g[5], g_next)
    return (xf.reshape(b, s, d), ckv_slab.reshape(n_even, b, s, -1), jnp.stack(krs), jnp.stack(rets),
            jnp.stack(cs), jnp.stack(ns), jnp.stack(ms))


def kernel(x_prompt, x_sample, cache_mla_ckv, cache_mla_krope, state_ret, state_mlstm_C, state_mlstm_n, state_mlstm_m, cache_mem_k, cache_mem_v, mem_prompt, norm_g, mem_norm_g, w_in_even, mla_q_norm_g, mla_kv_norm_g, w_uq, w_ukv, ret_gn_g, w_out_even, w_in_odd, b_gates_odd, mlstm_norm_g, w_out_odd, w_xq, w_xk, w_xv, w_xo, w_mlp1, w_mlp2):
    even_w, odd_w, shared = _prep_weights(w_in_even, w_uq, w_ukv, w_out_even, w_in_odd, b_gates_odd, w_out_odd,
                                          w_xq, w_xo, w_mlp1, w_mlp2)
    gains = dict(ret_gn_g=ret_gn_g, mla_q_norm_g=mla_q_norm_g, mla_kv_norm_g=mla_kv_norm_g,
                 mlstm_norm_g=mlstm_norm_g)
    depth = norm_g.shape[0]
    b, s, d = x_prompt.shape
    n_even, n_odd = w_in_even.shape[0], w_in_odd.shape[0]
    xw = X_HEADS * X_DH

    bm, t, _ = mem_prompt.shape
    mem_flat = mem_prompt.reshape(bm * t, d)
    p_mem_k, p_mem_v = [], []
    for layer in range(depth):
        mn = rms_rows(mem_flat, mem_norm_g[layer], BF16)
        kv = matmul(mn, jnp.concatenate([w_xk[layer], w_xv[layer]], axis=1).astype(BF16), F32)
        p_mem_k.append(kv[:, :xw].reshape(bm, t, X_HEADS, X_DH))
        p_mem_v.append(kv[:, xw:].reshape(bm, t, X_HEADS, X_DH))
    p_mem_k, p_mem_v = jnp.stack(p_mem_k), jnp.stack(p_mem_v)

    pos_p = jnp.arange(s, dtype=jnp.int32)
    zeros = lambda *shape: jnp.zeros(shape, F32)
    y_prompt, p_ckv, p_kr, p_ret, p_c, p_n, p_m = _run_trunk(
        x_prompt, pos_p, p_mem_k.reshape(depth, bm, t, xw).astype(BF16), p_mem_v.reshape(depth, bm, t, xw).astype(BF16),
        zeros(n_even, b, RET_HEADS, RET_DK, RET_DV), zeros(n_odd, b, M_HEADS, M_DV, M_DK),
        zeros(n_odd, b, M_HEADS, M_DK), zeros(n_odd, b, M_HEADS), None, norm_g, gains, even_w, odd_w, shared)

    db, ds, _ = x_sample.shape
    past_len = cache_mla_ckv.shape[2]
    pos_s = past_len + jnp.arange(ds, dtype=jnp.int32)
    past = (cache_mla_ckv, jnp.pad(cache_mla_krope, ((0, 0), (0, 0), (0, 0), (0, LANES - MLA_ROPE))))
    y_sample, s_ckv, s_kr, s_ret, s_c, s_n, s_m = _run_trunk(
        x_sample, pos_s, cache_mem_k.reshape(depth, db, -1, xw).astype(BF16),
        cache_mem_v.reshape(depth, db, -1, xw).astype(BF16),
        state_ret, state_mlstm_C, state_mlstm_n, state_mlstm_m, past, norm_g, gains, even_w, odd_w, shared)

    return (y_prompt, y_sample, p_ckv, p_kr, p_ret, p_c, p_n, p_m, p_mem_k, p_mem_v,
            s_ckv, s_kr, s_ret, s_c, s_n, s_m)
```

```python
import functools

import jax
import jax.numpy as jnp
from jax import lax
from jax.experimental import pallas as pl
from jax.experimental.pallas import tpu as pltpu

F32 = jnp.float32
BF16 = jnp.bfloat16

RMS_EPS = 1e-6
ROPE_BASE = 10000.0
CHUNK = 64
CHUNK_SHIFT = 6
assert 1 << CHUNK_SHIFT == CHUNK

LOG2E = 1.4426950408889634
RET_HEADS, RET_DK, RET_DV = 8, 128, 128
MLA_HEADS, MLA_NOPE, MLA_ROPE, MLA_V = 8, 128, 64, 128
MLA_Q_LORA, MLA_KV_LORA = 768, 512
MLA_Q_PRESCALE = (MLA_NOPE + MLA_ROPE) ** -0.5 * LOG2E
M_HEADS, M_DK, M_DV = 8, 128, 256
X_HEADS, X_DH = 4, 128
X_Q_PRESCALE = X_DH ** -0.5 * LOG2E

LANES = 128
MXU_DIM = 256
V7X_VMEM_BYTES = 64 * 1024 * 1024
MIB = 1024 * 1024

MLA_QK_PAD = MXU_DIM
KV_PAD_MULTIPLE = MXU_DIM
KV_UNROLL = 2
MLA_HEADS_PER_STEP = 2
ATTN_Q_BLOCKS_PER_STEP = 2
ATTN_SINGLE_STEP_ELEMS = 512 * 512
HEAD_GROUPS = (range(0, 4), range(4, 8))
ROW_CHAINS = 2
MIN_ROPE_TABLE_ROWS = 256


def _cparams(semantics, vmem_mib):
    assert vmem_mib * MIB < V7X_VMEM_BYTES
    return pltpu.CompilerParams(dimension_semantics=semantics, vmem_limit_bytes=vmem_mib * MIB)


def _pick(n, cands):
    for c in cands:
        if c <= n and n % c == 0:
            return c
    raise ValueError(f"no tile for {n} in {cands}")


def _rms(x, g):
    ms = jnp.mean(x * x, axis=-1, keepdims=True)
    return x * lax.rsqrt(ms + RMS_EPS) * g


def _sigmoid(x):
    return 1.0 / (1.0 + jnp.exp(-x))


def _dot(a, b):
    return jnp.dot(a, b, preferred_element_type=F32)


def _dot_nt(a, b):
    return lax.dot_general(a, b, (((1,), (1,)), ((), ())), preferred_element_type=F32)


def _dot_tn(a, b):
    return lax.dot_general(a, b, (((0,), (0,)), ((), ())), preferred_element_type=F32)


def _rope128(x, cos2, sin2):
    return x * cos2 + pltpu.roll(x, 64, 1) * sin2


def _rope64(x, cos_p, sin_p):
    return x * cos_p + (pltpu.roll(x, 32, 1) - pltpu.roll(x, 96, 1)) * sin_p


def _norm_kernel(x_ref, g_ref, o_ref):
    o_ref[...] = _rms(x_ref[...].astype(F32), g_ref[...]).astype(o_ref.dtype)


def rms_rows(x, g, out_dtype):
    m, k = x.shape
    tm = _pick(m, (512, 256, 128, 64, 32, 16, 8))
    return pl.pallas_call(
        _norm_kernel,
        grid=(m // tm,),
        in_specs=[pl.BlockSpec((tm, k), lambda i: (i, 0)), pl.BlockSpec((1, k), lambda i: (0, 0))],
        out_specs=pl.BlockSpec((tm, k), lambda i: (i, 0)),
        out_shape=jax.ShapeDtypeStruct((m, k), out_dtype),
        compiler_params=_cparams(("parallel",), 32),
        name="rms_rows",
    )(x, g.reshape(1, k).astype(F32))


def _mm_kernel(*refs, has_scale, has_bias, n_rope, tn):
    it = iter(refs)
    h_ref, w_ref = next(it), next(it)
    scale_ref = next(it) if has_scale else None
    bias_ref = next(it) if has_bias else None
    cos_ref, sin_ref = (next(it), next(it)) if n_rope else (None, None)
    o_ref = next(it)
    acc = _dot(h_ref[...], w_ref[...])
    if has_scale:
        acc = acc * scale_ref[...]
    if has_bias:
        acc = acc + bias_ref[...]
    if n_rope:
        c, s = cos_ref[...], sin_ref[...]
        for t in range(tn // LANES):
            sl = slice(t * LANES, (t + 1) * LANES)
            o_ref[:, sl] = _rope128(acc[:, sl], c, s).astype(o_ref.dtype)
    else:
        o_ref[...] = acc.astype(o_ref.dtype)


def matmul(h, w, out_dtype, *, col0=0, n_cols=None, col_scale=None, col_bias=None, rope=None):
    m, k = h.shape
    n = w.shape[1] - col0 if n_cols is None else n_cols
    tm = _pick(m, (2048, 1024, 512, 256, 128, 64))
    tn = _pick(n, (1024, 512, 256, 128))
    assert col0 % tn == 0 and col0 + n <= w.shape[1]
    jb = col0 // tn
    args = [h, w]
    in_specs = [pl.BlockSpec((tm, k), lambda i, j: (i, 0)), pl.BlockSpec((k, tn), lambda i, j: (0, j + jb))]
    for vec in (col_scale, col_bias):
        if vec is not None:
            args.append(vec.reshape(1, n).astype(F32))
            in_specs.append(pl.BlockSpec((1, tn), lambda i, j: (0, j)))
    if rope is not None:
        r = rope[0].shape[0]
        tm = _pick(m, tuple(c for c in (1024, 512, 256, 128, 64) if r % c == 0))
        in_specs[0] = pl.BlockSpec((tm, k), lambda i, j: (i, 0))
        nrb = r // tm
        for tab in rope:
            args.append(tab)
            in_specs.append(pl.BlockSpec((tm, LANES), lambda i, j: (i % nrb, 0)))
    kern = functools.partial(_mm_kernel, has_scale=col_scale is not None, has_bias=col_bias is not None,
                             n_rope=rope is not None, tn=tn)
    return pl.pallas_call(
        kern,
        grid=(m // tm, n // tn),
        in_specs=in_specs,
        out_specs=pl.BlockSpec((tm, tn), lambda i, j: (i, j)),
        out_shape=jax.ShapeDtypeStruct((m, n), out_dtype),
        compiler_params=_cparams(("parallel", "arbitrary"), 56),
        name="matmul",
    )(*args)


def _kv_expand(ckvn_bf16, krr_bf16, wuk_ref, wuv_ref, kcat_ref, v_ref):
    kn = _dot(ckvn_bf16, wuk_ref[...])
    for hd in range(MLA_HEADS):
        kcat_ref[:, hd * MLA_QK_PAD: hd * MLA_QK_PAD + MLA_NOPE] = (
            kn[:, hd * MLA_NOPE:(hd + 1) * MLA_NOPE].astype(BF16))
        kcat_ref[:, hd * MLA_QK_PAD + MLA_NOPE:(hd + 1) * MLA_QK_PAD] = krr_bf16
    v_ref[...] = _dot(ckvn_bf16, wuv_ref[...]).astype(BF16)


def _even_latent_kernel(h_ref, wlat_ref, gq_ref, gkv_ref, wuq_ref, wuk_ref, wuv_ref, cos_ref, sin_ref, *refs):
    qcat_ref, kcat_ref, v_ref, ckv_ref, kr_ref = refs[-5:]
    lat = _dot(h_ref[...], wlat_ref[...])
    cq = lat[:, :MLA_Q_LORA]
    ckv = lat[:, MLA_Q_LORA:MLA_Q_LORA + MLA_KV_LORA]
    krp = lat[:, MLA_Q_LORA + MLA_KV_LORA:]
    c, s = cos_ref[...], sin_ref[...]
    q = _dot(_rms(cq, gq_ref[...]).astype(BF16), wuq_ref[...]) * MLA_Q_PRESCALE
    nope_w = MLA_HEADS * MLA_NOPE
    for hd in range(MLA_HEADS):
        qcat_ref[:, hd * MLA_QK_PAD: hd * MLA_QK_PAD + MLA_NOPE] = (
            q[:, hd * MLA_NOPE:(hd + 1) * MLA_NOPE].astype(BF16))
        qr = q[:, nope_w + hd * LANES: nope_w + (hd + 1) * LANES]
        qcat_ref[:, hd * MLA_QK_PAD + MLA_NOPE:(hd + 1) * MLA_QK_PAD] = _rope64(qr, c, s).astype(BF16)
    ckvn = _rms(ckv, gkv_ref[...])
    ckv_ref[...] = ckvn
    krr = _rope64(krp, c, s)
    kr_ref[...] = krr[:, :MLA_ROPE]
    _kv_expand(ckvn.astype(BF16), krr.astype(BF16), wuk_ref, wuv_ref, kcat_ref, v_ref)


def even_latent(h, wlat, gq, gkv, wuq, wuk, wuv, rope64, e, n_even, ckv_slab):
    m, d = h.shape
    r = rope64[0].shape[0]
    tm = _pick(m, tuple(c for c in (512, 256, 128, 64) if r % c == 0))
    nrb = r // tm
    full = lambda a: pl.BlockSpec(a.shape, lambda i: (0,) * a.ndim)
    row = lambda w: pl.BlockSpec((tm, w), lambda i: (i, 0))
    tab = pl.BlockSpec((tm, LANES), lambda i: (i % nrb, 0))
    qk_w = MLA_HEADS * MLA_QK_PAD
    v_w = MLA_HEADS * MLA_V
    args = [h, wlat, gq, gkv, wuq, wuk, wuv, rope64[0], rope64[1]]
    in_specs = [row(d), full(wlat), full(gq), full(gkv), full(wuq), full(wuk), full(wuv), tab, tab]
    aliases = {}
    if ckv_slab is not None:
        aliases = {len(args): 3}
        args.append(ckv_slab)
        in_specs.append(pl.BlockSpec(memory_space=pl.ANY))
    return pl.pallas_call(
        _even_latent_kernel,
        grid=(m // tm,),
        in_specs=in_specs,
        out_specs=[row(qk_w), row(qk_w), row(v_w),
                   pl.BlockSpec((None, tm, MLA_KV_LORA), lambda i: (e, i, 0)), row(MLA_ROPE)],
        out_shape=[jax.ShapeDtypeStruct((m, qk_w), BF16), jax.ShapeDtypeStruct((m, qk_w), BF16),
                   jax.ShapeDtypeStruct((m, v_w), BF16), jax.ShapeDtypeStruct((n_even, m, MLA_KV_LORA), F32),
                   jax.ShapeDtypeStruct((m, MLA_ROPE), F32)],
        input_output_aliases=aliases,
        compiler_params=_cparams(("parallel",), 48),
        name="even_latent",
    )(*args)


def _past_kv_kernel(ckv_ref, krp_ref, wuk_ref, wuv_ref, kcat_ref, v_ref):
    _kv_expand(ckv_ref[...].astype(BF16), krp_ref[...].astype(BF16), wuk_ref, wuv_ref, kcat_ref, v_ref)


def past_kv(ckvn, kr_pad, e, wuk, wuv, skp):
    _, b, p_len, _ = ckvn.shape
    tm = _pick(p_len, (512, 256, 128, 64))
    full = lambda a: pl.BlockSpec(a.shape, lambda bi, i: (0,) * a.ndim)
    src = lambda w: pl.BlockSpec((None, None, tm, w), lambda bi, i: (e, bi, i, 0))
    dst = lambda w: pl.BlockSpec((None, tm, w), lambda bi, i: (bi, i, 0))
    qk_w = MLA_HEADS * MLA_QK_PAD
    v_w = MLA_HEADS * MLA_V
    return pl.pallas_call(
        _past_kv_kernel,
        grid=(b, p_len // tm),
        in_specs=[src(MLA_KV_LORA), src(LANES), full(wuk), full(wuv)],
        out_specs=[dst(qk_w), dst(v_w)],
        out_shape=[jax.ShapeDtypeStruct((b, skp, qk_w), BF16), jax.ShapeDtypeStruct((b, skp, v_w), BF16)],
        compiler_params=_cparams(("parallel", "parallel"), 32),
        name="past_kv",
    )(ckvn, kr_pad, wuk, wuv)


def _mla_kernel(q_ref, k_ref, v_ref, o_ref, *scratch, tq, qsub, **kw):
    for sub in range(qsub):
        rows = slice(sub * tq, (sub + 1) * tq)
        _mla_block(q_ref.at[:, rows, :], k_ref, v_ref, o_ref.at[:, rows, :], *scratch,
                   qi=pl.program_id(2) * qsub + sub, tq=tq, **kw)


def _mla_block(q_ref, k_ref, v_ref, o_ref, m_scr, l_scr, acc_scr, a_scr, p_scr, *, qi, tq, tk, hb, q_off, sk_valid):
    q0 = qi * tq
    cq_lo = (q_off + q0) // CHUNK
    cq_hi = (q_off + q0 + tq - 1) // CHUNK
    n_full = jnp.minimum(lax.div((cq_lo + 1) * CHUNK, tk), sk_valid // tk)
    n_vis = lax.div(jnp.minimum((cq_hi + 1) * CHUNK, sk_valid) + tk - 1, tk)
    m_scr[...] = jnp.full(m_scr.shape, -1e30, F32)
    l_scr[...] = jnp.zeros(l_scr.shape, F32)
    acc_scr[...] = jnp.zeros(acc_scr.shape, F32)

    def apply_prev(j, ki_prev):
        v = v_ref[0, pl.ds(pl.multiple_of(ki_prev * tk, tk), tk), j * MLA_V:(j + 1) * MLA_V]
        acc_scr[j] = a_scr[j] * acc_scr[j] + _dot(p_scr[j], v)

    def step(ki, masked, first=False):
        k0 = pl.multiple_of(ki * tk, tk)
        if masked:
            qpos = q_off + q0 + lax.broadcasted_iota(jnp.int32, (tq, tk), 0)
            kpos = k0 + lax.broadcasted_iota(jnp.int32, (tq, tk), 1)
            vis = jnp.logical_and((kpos >> CHUNK_SHIFT) <= (qpos >> CHUNK_SHIFT), kpos < sk_valid)
        for j in range(hb):
            q = q_ref[0, :, j * MLA_QK_PAD:(j + 1) * MLA_QK_PAD]
            k = k_ref[0, pl.ds(k0, tk), j * MLA_QK_PAD:(j + 1) * MLA_QK_PAD]
            s = _dot_nt(q, k)
            if not first:
                apply_prev(j, ki - 1)
            if masked:
                s = jnp.where(vis, s, -jnp.inf)
            m_prev = m_scr[j]
            m_new = jnp.maximum(m_prev, jnp.max(s, axis=-1, keepdims=True))
            alpha = jnp.exp2(m_prev - m_new)
            p = [jnp.exp2(s[:, c * LANES:(c + 1) * LANES] - m_new) for c in range(tk // LANES)]
            l_scr[j] = alpha * l_scr[j] + sum(p[1:], p[0])
            p_scr[j] = jnp.concatenate([pc.astype(BF16) for pc in p], axis=-1)
            a_scr[j] = alpha
            m_scr[j] = m_new

    def body_full(ki, carry):
        step(ki, False)
        return carry

    def body_masked(ki, carry):
        step(ki, True)
        return carry

    def body_full_group(kg, carry):
        for u in range(KV_UNROLL):
            step(1 + KV_UNROLL * kg + u, False)
        return carry

    step(0, True, first=True)
    n_full1 = jnp.maximum(n_full, 1)
    n_groups = lax.div(n_full1 - 1, KV_UNROLL)
    lax.fori_loop(0, n_groups, body_full_group, 0)
    lax.fori_loop(1 + KV_UNROLL * n_groups, n_full1, body_full, 0)
    lax.fori_loop(n_full1, n_vis, body_masked, 0)
    for j in range(hb):
        apply_prev(j, n_vis - 1)
        l = jnp.sum(l_scr[j], axis=-1, keepdims=True)
        o_ref[0, :, j * MLA_V:(j + 1) * MLA_V] = (acc_scr[j] / l).astype(o_ref.dtype)


def mla_attention(qcat, kcat, v, *, q_off, sk_valid):
    b, sq, _ = qcat.shape
    skp = kcat.shape[1]
    tq = _pick(sq, (512, 256, 128, 64))
    tk = skp if tq * skp <= ATTN_SINGLE_STEP_ELEMS else _pick(skp, (512, 256, 128))
    hb = MLA_HEADS if tk == skp else MLA_HEADS_PER_STEP
    qsub = ATTN_Q_BLOCKS_PER_STEP if (sq // tq) % ATTN_Q_BLOCKS_PER_STEP == 0 else 1
    kern = functools.partial(_mla_kernel, tq=tq, qsub=qsub, tk=tk, hb=hb, q_off=q_off, sk_valid=sk_valid)
    return pl.pallas_call(
        kern,
        grid=(b, MLA_HEADS // hb, sq // (tq * qsub)),
        in_specs=[pl.BlockSpec((1, tq * qsub, hb * MLA_QK_PAD), lambda bi, h, qi: (bi, qi, h)),
                  pl.BlockSpec((1, skp, hb * MLA_QK_PAD), lambda bi, h, qi: (bi, 0, h)),
                  pl.BlockSpec((1, skp, hb * MLA_V), lambda bi, h, qi: (bi, 0, h))],
        out_specs=pl.BlockSpec((1, tq * qsub, hb * MLA_V), lambda bi, h, qi: (bi, qi, h)),
        out_shape=jax.ShapeDtypeStruct((b, sq, MLA_HEADS * MLA_V), BF16),
        scratch_shapes=[pltpu.VMEM((hb, tq, LANES), F32), pltpu.VMEM((hb, tq, LANES), F32),
                        pltpu.VMEM((hb, tq, MLA_V), F32), pltpu.VMEM((hb, tq, LANES), F32),
                        pltpu.VMEM((hb, tq, tk), BF16)],
        compiler_params=_cparams(("parallel", "parallel", "arbitrary"), 48),
        name="mla_attention",
    )(qcat, kcat, v)


def _retention_kernel(lg_ref, qk_ref, vg_ref, s0_ref, gn_ref, out_ref, sfin_ref,
                      state_scr, decay_scr, qdec_scr, kdec_scr, *, lc):
    ci = pl.program_id(1)
    heads = range(RET_HEADS)

    @pl.when(ci == 0)
    def _():
        state_scr[...] = s0_ref[0]
        ti = lax.broadcasted_iota(jnp.int32, (lc, lc), 0)
        si = lax.broadcasted_iota(jnp.int32, (lc, lc), 1)
        dpos = jnp.maximum(ti - si, 0).astype(F32)
        trow = lax.broadcasted_iota(jnp.int32, (lc, LANES), 0).astype(F32)
        for h in heads:
            lg = lg_ref[h:h + 1, :]
            decay_scr[h] = jnp.where(ti >= si, jnp.exp(lg[:, 0:1] * dpos), 0.0)
            qdec_scr[h] = jnp.exp(lg * (trow + 1.0))
            kdec_scr[h] = jnp.exp(lg * (lc - 1.0 - trow))

    qk_w = RET_HEADS * RET_DK
    v_w = RET_HEADS * RET_DV
    q = [qk_ref[0, :, h * RET_DK:(h + 1) * RET_DK] for h in heads]
    k = [qk_ref[0, :, qk_w + h * RET_DK: qk_w + (h + 1) * RET_DK] for h in heads]
    v = [vg_ref[0, :, h * RET_DV:(h + 1) * RET_DV] for h in heads]
    st = [state_scr[h] for h in heads]
    sc = [_dot_nt(q[h], k[h]) for h in heads]
    inter = [_dot(q[h], st[h].astype(BF16)) for h in heads]
    scd = [(sc[h] * decay_scr[h]).astype(BF16) for h in heads]
    kd = [(k[h].astype(F32) * kdec_scr[h]).astype(BF16) for h in heads]
    intra = [_dot(scd[h], v[h]) for h in heads]
    upd = [_dot_tn(kd[h], v[h]) for h in heads]
    for h in heads:
        state_scr[h] = jnp.exp(lg_ref[h:h + 1, :] * lc) * st[h] + upd[h]
        out = intra[h] + qdec_scr[h] * inter[h]
        g = vg_ref[0, :, v_w + h * RET_DV: v_w + (h + 1) * RET_DV].astype(F32)
        xc = out - jnp.mean(out, axis=-1, keepdims=True)
        y = xc * lax.rsqrt(jnp.mean(xc * xc, axis=-1, keepdims=True) + RMS_EPS)
        y = y * gn_ref[:, h * RET_DV:(h + 1) * RET_DV]
        out_ref[0, :, h * RET_DV:(h + 1) * RET_DV] = (g * _sigmoid(g) * y).astype(out_ref.dtype)

    @pl.when(ci == pl.num_programs(1) - 1)
    def _():
        sfin_ref[0] = state_scr[...]


def retention(qk, vg, s0, gn_g):
    b, s, _ = qk.shape
    lc = _pick(s, (256, 128, 64))
    log_g = jnp.log1p(-jnp.exp2(-5.0 - jnp.arange(RET_HEADS, dtype=F32)))
    lg_tab = jnp.broadcast_to(log_g[:, None], (RET_HEADS, LANES))
    v_w = RET_HEADS * RET_DV
    kern = functools.partial(_retention_kernel, lc=lc)
    st_spec = pl.BlockSpec((1, RET_HEADS, RET_DK, RET_DV), lambda bi, ci: (bi, 0, 0, 0))
    return pl.pallas_call(
        kern,
        grid=(b, s // lc),
        in_specs=[pl.BlockSpec((RET_HEADS, LANES), lambda bi, ci: (0, 0)),
                  pl.BlockSpec((1, lc, qk.shape[2]), lambda bi, ci: (bi, ci, 0)),
                  pl.BlockSpec((1, lc, vg.shape[2]), lambda bi, ci: (bi, ci, 0)),
                  st_spec,
                  pl.BlockSpec((1, v_w), lambda bi, ci: (0, 0))],
        out_specs=[pl.BlockSpec((1, lc, v_w), lambda bi, ci: (bi, ci, 0)), st_spec],
        out_shape=[jax.ShapeDtypeStruct((b, s, v_w), BF16),
                   jax.ShapeDtypeStruct((b, RET_HEADS, RET_DK, RET_DV), F32)],
        scratch_shapes=[pltpu.VMEM((RET_HEADS, RET_DK, RET_DV), F32), pltpu.VMEM((RET_HEADS, lc, lc), F32),
                        pltpu.VMEM((RET_HEADS, lc, LANES), F32), pltpu.VMEM((RET_HEADS, lc, LANES), F32)],
        compiler_params=_cparams(("parallel", "arbitrary"), 40),
        name="retention",
    )(lg_tab, qk, vg, s0, gn_g.reshape(1, v_w).astype(F32))


def _split3(x):
    hi = x.astype(BF16)
    r1 = x - hi.astype(F32)
    mid = r1.astype(BF16)
    lo = (r1 - mid.astype(F32)).astype(BF16)
    return hi, mid, lo


def _log_sigmoid(x):
    return jnp.minimum(x, 0.0) - jnp.log(1.0 + jnp.exp(-jnp.abs(x)))


def _widen(x, width):
    if width < LANES:
        return x[:, :width]
    return x if width == LANES else jnp.concatenate([x] * (width // LANES), axis=1)


def _mlstm_kernel(qkvo_ref, gates_ref, gates_t_ref, c0_ref, n0_ref, m0_ref, ng_ref,
                  out_ref, cfin_ref, nfin_ref, mfin_ref, c_scr, n_scr, m_scr, *, lc):
    ci = pl.program_id(1)

    @pl.when(ci == 0)
    def _():
        c_scr[...] = c0_ref[0]
        n_scr[...] = n0_ref[0]
        m_scr[...] = m0_ref[0]

    gates = gates_ref[0]
    gates_t = gates_t_ref[0]
    ti = lax.broadcasted_iota(jnp.int32, (lc, lc), 0)
    si = lax.broadcasted_iota(jnp.int32, (lc, lc), 1)
    tril = si <= ti
    ones_l = jnp.where(tril, 1.0, 0.0).astype(BF16)
    ones_u = jnp.where(ti <= si, 1.0, 0.0).astype(BF16)
    b_col = sum(_dot(ones_l, part) for part in _split3(_log_sigmoid(gates)))
    b_row = sum(_dot(part, ones_u) for part in _split3(_log_sigmoid(gates_t)))[M_HEADS:2 * M_HEADS]
    i_row = gates_t[0:M_HEADS]
    qk_w = M_HEADS * M_DK
    v_w = M_HEADS * M_DV
    for heads in HEAD_GROUPS:
        q = {h: qkvo_ref[0, :, h * M_DK:(h + 1) * M_DK] for h in heads}
        k = {h: qkvo_ref[0, :, qk_w + h * M_DK: qk_w + (h + 1) * M_DK] for h in heads}
        v = {h: qkvo_ref[0, :, 2 * qk_w + h * M_DV: 2 * qk_w + (h + 1) * M_DV] for h in heads}
        c_st = {h: c_scr[h] for h in heads}
        n_st = {h: n_scr[h:h + 1, :] for h in heads}
        m_prev = {h: m_scr[h:h + 1, :] for h in heads}
        qk = {h: _dot_nt(q[h], k[h]) for h in heads}
        qc = {h: _dot(q[h], c_st[h].astype(BF16)) for h in heads}
        bt = {h: jnp.broadcast_to(b_col[:, M_HEADS + h:M_HEADS + h + 1], (lc, LANES)) for h in heads}
        i_t = {h: jnp.broadcast_to(gates[:, h:h + 1], (lc, LANES)) for h in heads}
        m_t, w_inter, sc = {}, {}, {}
        for h in heads:
            log_intra = jnp.where(tril, _widen(bt[h], lc) - b_row[h:h + 1, :] + i_row[h:h + 1, :], -jnp.inf)
            log_inter = bt[h] + m_prev[h]
            m_t[h] = jnp.maximum(log_inter, jnp.max(log_intra, axis=-1, keepdims=True))
            w_inter[h] = jnp.exp(log_inter - m_t[h])
            sc[h] = qk[h] * jnp.exp(log_intra - _widen(m_t[h], lc))
        num_intra = {h: _dot(sc[h].astype(BF16), v[h]) for h in heads}
        kw = {}
        for h in heads:
            m_new = m_t[h][lc - 1:lc, :]
            b_last = bt[h][lc - 1:lc, :]
            w_state = jnp.exp(b_last + m_prev[h] - m_new)
            kw[h] = k[h].astype(F32) * jnp.exp(b_last - bt[h] + i_t[h] - m_new)
            n_scr[h:h + 1, :] = w_state * n_st[h] + jnp.sum(kw[h], axis=0, keepdims=True)
            m_scr[h:h + 1, :] = m_new
            c_scr[h] = _widen(w_state, M_DV) * c_st[h] + _dot_tn(kw[h].astype(BF16), v[h])
        for h in heads:
            num = num_intra[h] + _widen(w_inter[h], M_DV) * qc[h]
            den = (jnp.sum(sc[h], axis=-1, keepdims=True)
                   + w_inter[h] * jnp.sum(q[h].astype(F32) * n_st[h], axis=-1, keepdims=True))
            hh = num * _widen(1.0 / jnp.maximum(jnp.abs(den), jnp.exp(-m_t[h])), M_DV)
            y = hh * lax.rsqrt(jnp.mean(hh * hh, axis=-1, keepdims=True) + RMS_EPS)
            y = y * ng_ref[:, h * M_DV:(h + 1) * M_DV]
            og = qkvo_ref[0, :, 2 * qk_w + v_w + h * M_DV: 2 * qk_w + v_w + (h + 1) * M_DV].astype(F32)
            out_ref[0, :, h * M_DV:(h + 1) * M_DV] = (y * _sigmoid(og)).astype(out_ref.dtype)

    @pl.when(ci == pl.num_programs(1) - 1)
    def _():
        cfin_ref[0] = c_scr[...]
        nfin_ref[0] = n_scr[...]
        mfin_ref[0] = m_scr[...]


def mlstm(qkvo, gates, c0, n0, m0, norm_g):
    b, s, w = qkvo.shape
    lc = _pick(s, (256, 128, 64))
    v_w = M_HEADS * M_DV
    gates_t = jnp.swapaxes(gates[:, :, :2 * M_HEADS], 1, 2)
    kern = functools.partial(_mlstm_kernel, lc=lc)
    c_spec = pl.BlockSpec((1, M_HEADS, M_DK, M_DV), lambda bi, ci: (bi, 0, 0, 0))
    n_spec = pl.BlockSpec((1, M_HEADS, M_DK), lambda bi, ci: (bi, 0, 0))
    m_spec = pl.BlockSpec((1, M_HEADS, LANES), lambda bi, ci: (bi, 0, 0))
    hh, c_t, n, m = pl.pallas_call(
        kern,
        grid=(b, s // lc),
        in_specs=[pl.BlockSpec((1, lc, w), lambda bi, ci: (bi, ci, 0)),
                  pl.BlockSpec((1, lc, LANES), lambda bi, ci: (bi, ci, 0)),
                  pl.BlockSpec((1, 2 * M_HEADS, lc), lambda bi, ci: (bi, 0, ci)),
                  c_spec, n_spec, m_spec,
                  pl.BlockSpec((1, v_w), lambda bi, ci: (0, 0))],
        out_specs=[pl.BlockSpec((1, lc, v_w), lambda bi, ci: (bi, ci, 0)), c_spec, n_spec, m_spec],
        out_shape=[jax.ShapeDtypeStruct((b, s, v_w), BF16),
                   jax.ShapeDtypeStruct((b, M_HEADS, M_DK, M_DV), F32),
                   jax.ShapeDtypeStruct((b, M_HEADS, M_DK), F32),
                   jax.ShapeDtypeStruct((b, M_HEADS, LANES), F32)],
        scratch_shapes=[pltpu.VMEM((M_HEADS, M_DK, M_DV), F32), pltpu.VMEM((M_HEADS, M_DK), F32),
                        pltpu.VMEM((M_HEADS, LANES), F32)],
        compiler_params=_cparams(("parallel", "arbitrary"), 48),
        name="mlstm",
    )(qkvo, gates, gates_t, jnp.swapaxes(c0, 2, 3), n0, jnp.broadcast_to(m0[..., None], m0.shape + (LANES,)),
      norm_g.reshape(1, v_w).astype(F32))
    return hh, jnp.swapaxes(c_t, 2, 3), n, m[..., 0]


def _residual_out(acc, x_ref, gp_ref, gn_ref, xo_ref, ho_ref, rows=slice(None)):
    xn = x_ref[rows, :] + _rms(acc, gp_ref[...])
    xo_ref[rows, :] = xn
    if ho_ref is not None:
        ho_ref[rows, :] = _rms(xn, gn_ref[...]).astype(ho_ref.dtype)


def _row_chains(tm):
    n = ROW_CHAINS if tm % (ROW_CHAINS * 128) == 0 else 1
    return [slice(c * (tm // n), (c + 1) * (tm // n)) for c in range(n)]


def _outproj_kernel(*refs, n_in):
    a_refs, w_refs = refs[:n_in], refs[n_in:2 * n_in]
    x_ref, gp_ref, gn_ref, xo_ref, ho_ref = refs[2 * n_in:]
    for rows in _row_chains(x_ref.shape[0]):
        acc = _dot(a_refs[0][rows, :], w_refs[0][...])
        for a_ref, w_ref in zip(a_refs[1:], w_refs[1:]):
            acc = acc + _dot(a_ref[rows, :], w_ref[...])
        _residual_out(acc, x_ref, gp_ref, gn_ref, xo_ref, ho_ref, rows)


def outproj(acts, w, x, g_post, g_next):
    m, d = x.shape
    tm = _pick(m, (512, 256, 128, 64))
    n_in = len(acts)
    row = lambda wd: pl.BlockSpec((tm, wd), lambda i: (i, 0))
    vec = pl.BlockSpec((1, d), lambda i: (0, 0))
    w_specs, r0 = [], 0
    for a in acts:
        ka = a.shape[1]
        assert r0 % ka == 0
        w_specs.append(pl.BlockSpec((ka, d), functools.partial(lambda blk, i: (blk, 0), r0 // ka)))
        r0 += ka
    assert r0 == w.shape[0]
    return pl.pallas_call(
        functools.partial(_outproj_kernel, n_in=n_in),
        grid=(m // tm,),
        in_specs=[row(a.shape[1]) for a in acts] + w_specs + [row(d), vec, vec],
        out_specs=[row(d), row(d)],
        out_shape=[jax.ShapeDtypeStruct((m, d), F32), jax.ShapeDtypeStruct((m, d), BF16)],
        compiler_params=_cparams(("parallel",), 56),
        name="outproj",
    )(*acts, *([w] * n_in), x, g_post.reshape(1, d), g_next.reshape(1, d))


def _xattn_kernel(h_ref, wq_ref, mk_ref, mv_ref, wo_ref, x_ref, gp_ref, gn_ref, xo_ref, ho_ref):
    mk, mv = mk_ref[0], mv_ref[0]
    t = mk.shape[0]
    chains = _row_chains(x_ref.shape[0])
    heads = [slice(hd * X_DH, (hd + 1) * X_DH) for hd in range(X_HEADS)]
    q = [(_dot(h_ref[rows, :], wq_ref[...]) * X_Q_PRESCALE).astype(BF16) for rows in chains]
    s = [[_dot_nt(qc[:, sl], mk[:, sl]) for sl in heads] for qc in q]
    p, inv_l = [], []
    for sc in s:
        pc, lc = [], []
        for sh in sc:
            e = jnp.exp2(sh - jnp.max(sh, axis=-1, keepdims=True))
            part = e[:, :LANES]
            for c in range(1, t // LANES):
                part = part + e[:, c * LANES:(c + 1) * LANES]
            pc.append(e.astype(BF16))
            lc.append(1.0 / jnp.sum(part, axis=-1, keepdims=True))
        p.append(pc)
        inv_l.append(lc)
    o = [jnp.concatenate([(_dot(pc[hd], mv[:, heads[hd]]) * lc[hd]).astype(BF16) for hd in range(X_HEADS)], axis=-1)
         for pc, lc in zip(p, inv_l)]
    acc = [_dot(oc, wo_ref[...]) for oc in o]
    for rows, ac in zip(chains, acc):
        _residual_out(ac, x_ref, gp_ref, gn_ref, xo_ref, ho_ref, rows)


def xattn_block(h, x, mem_k, mem_v, wq, wo, g_post, g_next, seq):
    m, d = x.shape
    tm = _pick(seq, (512, 256, 128, 64))
    per_b = seq // tm
    t, xw = mem_k.shape[1], mem_k.shape[2]
    full = lambda a: pl.BlockSpec(a.shape, lambda i: (0,) * a.ndim)
    row = lambda w: pl.BlockSpec((tm, w), lambda i: (i, 0))
    mem = pl.BlockSpec((1, t, xw), lambda i: (i // per_b, 0, 0))
    vec = pl.BlockSpec((1, d), lambda i: (0, 0))
    return pl.pallas_call(
        _xattn_kernel,
        grid=(m // tm,),
        in_specs=[row(d), full(wq), mem, mem, full(wo), row(d), vec, vec],
        out_specs=[row(d), row(d)],
        out_shape=[jax.ShapeDtypeStruct((m, d), F32), jax.ShapeDtypeStruct((m, d), BF16)],
        compiler_params=_cparams(("parallel",), 48),
        name="xattn_block",
    )(h, wq, mem_k, mem_v, wo, x, g_post.reshape(1, d), g_next.reshape(1, d))


def _mlp_kernel(h_ref, w1_ref, w2_ref, x_ref, gp_ref, gn_ref, xo_ref, *rest, emit_h):
    ho_ref, acc_scr = (rest[0], rest[1]) if emit_h else (None, rest[0])
    f = pl.program_id(1)
    last = pl.num_programs(1) - 1

    def partial_product():
        a = jnp.maximum(_dot(h_ref[...], w1_ref[...]), 0.0)
        return _dot((a * a).astype(BF16), w2_ref[...])

    @pl.when(f == 0)
    def _():
        acc_scr[...] = partial_product()

    @pl.when(jnp.logical_and(f > 0, f < last))
    def _():
        acc_scr[...] += partial_product()

    @pl.when(jnp.logical_and(f > 0, f == last))
    def _():
        _residual_out(acc_scr[...] + partial_product(), x_ref, gp_ref, gn_ref, xo_ref, ho_ref)


def mlp_block(h, x, w1, w2, layer, g_post, g_next):
    m, d = x.shape
    ff = w1.shape[2]
    tm = _pick(m, (512, 256, 128, 64))
    tf = _pick(ff, (1024, 512, 256, 128))
    assert ff // tf >= 2
    emit_h = g_next is not None
    row = pl.BlockSpec((tm, d), lambda i, f: (i, 0))
    vec = pl.BlockSpec((1, d), lambda i, f: (0, 0))
    out_specs = [row] + ([row] if emit_h else [])
    out_shape = [jax.ShapeDtypeStruct((m, d), F32)] + ([jax.ShapeDtypeStruct((m, d), BF16)] if emit_h else [])
    g_n = (g_next if emit_h else g_post).reshape(1, d)
    res = pl.pallas_call(
        functools.partial(_mlp_kernel, emit_h=emit_h),
        grid=(m // tm, ff // tf),
        in_specs=[row, pl.BlockSpec((None, d, tf), lambda i, f: (layer, 0, f)),
                  pl.BlockSpec((None, tf, d), lambda i, f: (layer, f, 0)), row, vec, vec],
        out_specs=out_specs,
        out_shape=out_shape,
        scratch_shapes=[pltpu.VMEM((tm, d), F32)],
        compiler_params=_cparams(("parallel", "arbitrary"), 56),
        name="mlp_block",
    )(h, w1, w2, x, g_post.reshape(1, d), g_n)
    return (res[0], res[1]) if emit_h else (res[0], None)


def _rope_tables(pos):
    def angles(half):
        inv = jnp.power(ROPE_BASE, -jnp.arange(half, dtype=F32) / half)
        return pos.astype(F32)[:, None] * inv[None, :]
    a = angles(RET_DK // 2)
    c, s = jnp.cos(a), jnp.sin(a)
    t128 = (jnp.concatenate([c, c], axis=1), jnp.concatenate([-s, s], axis=1))
    a = angles(MLA_ROPE // 2)
    c, s = jnp.cos(a), jnp.sin(a)
    z = jnp.zeros((pos.shape[0], LANES - MLA_ROPE), F32)
    t64 = (jnp.concatenate([c, c, z], axis=1), jnp.concatenate([s, s, z], axis=1))
    return t128, t64


def _prep_weights(w_in_even, w_uq, w_ukv, w_out_even, w_in_odd, b_gates_odd, w_out_odd, w_xq, w_xo, w_mlp1, w_mlp2):
    d = w_in_even.shape[1]
    ret_w = RET_HEADS * RET_DK
    n_qkvg = 2 * ret_w + 2 * RET_HEADS * RET_DV
    n_lat = MLA_Q_LORA + MLA_KV_LORA + MLA_ROPE
    even = []
    for e in range(w_in_even.shape[0]):
        w = w_in_even[e].astype(BF16)
        wlat = jnp.pad(w[:, n_qkvg:n_qkvg + n_lat], ((0, 0), (0, LANES - MLA_ROPE)))
        uq = w_uq[e].reshape(MLA_Q_LORA, MLA_HEADS, MLA_NOPE + MLA_ROPE)
        uq_nope = uq[:, :, :MLA_NOPE].reshape(MLA_Q_LORA, MLA_HEADS * MLA_NOPE)
        uq_rope = jnp.pad(uq[:, :, MLA_NOPE:], ((0, 0), (0, 0), (0, LANES - MLA_ROPE)))
        uq_rope = uq_rope.reshape(MLA_Q_LORA, MLA_HEADS * LANES)
        ukv = w_ukv[e].reshape(MLA_KV_LORA, MLA_HEADS, MLA_NOPE + MLA_V)
        even.append(dict(
            w_in=w,
            w_lat=wlat,
            w_uq=jnp.concatenate([uq_nope, uq_rope], axis=1).astype(BF16),
            w_uk=ukv[:, :, :MLA_NOPE].reshape(MLA_KV_LORA, MLA_HEADS * MLA_NOPE).astype(BF16),
            w_uv=ukv[:, :, MLA_NOPE:].reshape(MLA_KV_LORA, MLA_HEADS * MLA_V).astype(BF16),
            w_out=w_out_even[e].astype(BF16),
        ))
    m_qk = M_HEADS * M_DK
    n_qkvo = 2 * m_qk + 2 * M_HEADS * M_DV
    odd = []
    for o in range(w_in_odd.shape[0]):
        w = w_in_odd[o].astype(BF16)
        odd.append(dict(
            w_in=w,
            w_gates=jnp.pad(w[:, n_qkvo:], ((0, 0), (0, LANES - 2 * M_HEADS))),
            b_gates=jnp.pad(b_gates_odd[o], (0, LANES - 2 * M_HEADS)),
            w_out=w_out_odd[o].astype(BF16),
        ))
    q_scale = jnp.concatenate([jnp.full((m_qk,), M_DK ** -0.5, F32), jnp.ones((n_qkvo - m_qk,), F32)])
    rk_scale = jnp.concatenate([jnp.ones((ret_w,), F32), jnp.full((ret_w,), RET_DK ** -0.5, F32)])
    shared = dict(w_xq=w_xq.astype(BF16), w_xo=w_xo.astype(BF16), w_mlp1=w_mlp1.astype(BF16),
                  w_mlp2=w_mlp2.astype(BF16), q_scale=q_scale, rk_scale=rk_scale, d=d,
                  n_qk=2 * ret_w, n_vg=n_qkvg - 2 * ret_w, n_qkvo=n_qkvo)
    return even, odd, shared


def _run_trunk(x, pos, mem_k, mem_v, ret_s0, c0, n0, m0, past, norm_g, gains, even_w, odd_w, shared):
    b, s, d = x.shape
    m = b * s
    depth = norm_g.shape[0]
    t128, t64 = _rope_tables(pos)
    if m % s or s % 64:
        raise ValueError("unsupported sequence length")
    if s < MIN_ROPE_TABLE_ROWS:
        t128 = tuple(jnp.tile(t, (b, 1)) for t in t128)
        t64 = tuple(jnp.tile(t, (b, 1)) for t in t64)
    xf = x.reshape(m, d)
    h = rms_rows(xf, norm_g[0, 0], BF16)
    n_even = len(even_w)
    ckv_slab = None
    krs, rets, cs, ns, ms = [], [], [], [], []
    for layer in range(depth):
        g = norm_g[layer]
        if layer % 2 == 0:
            e = layer // 2
            w = even_w[e]
            qk = matmul(h, w["w_in"], BF16, n_cols=shared["n_qk"], col_scale=shared["rk_scale"], rope=t128)
            vg = matmul(h, w["w_in"], BF16, col0=shared["n_qk"], n_cols=shared["n_vg"])
            ret, ret_s = retention(qk.reshape(b, s, -1), vg.reshape(b, s, -1), ret_s0[e], gains["ret_gn_g"][e])
            qcat, kcat, v, ckv_slab, kr = even_latent(h, w["w_lat"], gains["mla_q_norm_g"][e].reshape(1, -1),
                                                      gains["mla_kv_norm_g"][e].reshape(1, -1),
                                                      w["w_uq"], w["w_uk"], w["w_uv"], t64, e, n_even, ckv_slab)
            kcat = kcat.reshape(b, s, -1)
            v = v.reshape(b, s, -1)
            if past is None:
                q_off, sk = 0, s
            else:
                p_ckv, p_krp = past
                p_len = p_ckv.shape[2]
                q_off, sk = p_len, p_len + s
                pad = (-sk) % KV_PAD_MULTIPLE
                pk, pv = past_kv(p_ckv, p_krp, e, w["w_uk"], w["w_uv"], sk + pad)
                tail = lambda new: jnp.concatenate([new, jnp.zeros((b, pad, new.shape[-1]), BF16)], axis=1)
                kcat = lax.dynamic_update_slice(pk, tail(kcat), (0, p_len, 0))
                v = lax.dynamic_update_slice(pv, tail(v), (0, p_len, 0))
            att = mla_attention(qcat.reshape(b, s, -1), kcat, v, q_off=q_off, sk_valid=sk)
            xf, h = outproj([ret.reshape(m, -1), att.reshape(m, -1)], w["w_out"], xf, g[1], g[2])
            rets.append(ret_s)
            krs.append(kr.reshape(b, s, -1))
        else:
            o = layer // 2
            w = odd_w[o]
            qkvo = matmul(h, w["w_in"], BF16, n_cols=shared["n_qkvo"], col_scale=shared["q_scale"])
            gates = matmul(h, w["w_gates"], F32, col_bias=w["b_gates"])
            hh, c, n, mm = mlstm(qkvo.reshape(b, s, -1), gates.reshape(b, s, -1), c0[o], n0[o], m0[o],
                                 gains["mlstm_norm_g"][o])
            xf, h = outproj([hh.reshape(m, -1)], w["w_out"], xf, g[1], g[2])
            cs.append(c)
            ns.append(n)
            ms.append(mm)
        xf, h = xattn_block(h, xf, mem_k[layer], mem_v[layer], shared["w_xq"][layer], shared["w_xo"][layer],
                            g[3], g[4], s)
        g_next = norm_g[layer + 1, 0] if layer + 1 < depth else None
        xf, h = mlp_block(h, xf, shared["w_mlp1"], shared["w_mlp2"], layer, g[5], g_next)
    return (xf.reshape(b, s, d), ckv_slab.reshape(n_even, b, s, -1), jnp.stack(krs), jnp.stack(rets),
            jnp.stack(cs), jnp.stack(ns), jnp.stack(ms))


def kernel(x_prompt, x_sample, cache_mla_ckv, cache_mla_krope, state_ret, state_mlstm_C, state_mlstm_n, state_mlstm_m, cache_mem_k, cache_mem_v, mem_prompt, norm_g, mem_norm_g, w_in_even, mla_q_norm_g, mla_kv_norm_g, w_uq, w_ukv, ret_gn_g, w_out_even, w_in_odd, b_gates_odd, mlstm_norm_g, w_out_odd, w_xq, w_xk, w_xv, w_xo, w_mlp1, w_mlp2):
    even_w, odd_w, shared = _prep_weights(w_in_even, w_uq, w_ukv, w_out_even, w_in_odd, b_gates_odd, w_out_odd,
                                          w_xq, w_xo, w_mlp1, w_mlp2)
    gains = dict(ret_gn_g=ret_gn_g, mla_q_norm_g=mla_q_norm_g, mla_kv_norm_g=mla_kv_norm_g,
                 mlstm_norm_g=mlstm_norm_g)
    depth = norm_g.shape[0]
    b, s, d = x_prompt.shape
    n_even, n_odd = w_in_even.shape[0], w_in_odd.shape[0]
    xw = X_HEADS * X_DH

    bm, t, _ = mem_prompt.shape
    mem_flat = mem_prompt.reshape(bm * t, d)
    p_mem_k, p_mem_v = [], []
    for layer in range(depth):
        mn = rms_rows(mem_flat, mem_norm_g[layer], BF16)
        kv = matmul(mn, jnp.concatenate([w_xk[layer], w_xv[layer]], axis=1).astype(BF16), F32)
        p_mem_k.append(kv[:, :xw].reshape(bm, t, X_HEADS, X_DH))
        p_mem_v.append(kv[:, xw:].reshape(bm, t, X_HEADS, X_DH))
    p_mem_k, p_mem_v = jnp.stack(p_mem_k), jnp.stack(p_mem_v)

    pos_p = jnp.arange(s, dtype=jnp.int32)
    zeros = lambda *shape: jnp.zeros(shape, F32)
    y_prompt, p_ckv, p_kr, p_ret, p_c, p_n, p_m = _run_trunk(
        x_prompt, pos_p, p_mem_k.reshape(depth, bm, t, xw).astype(BF16), p_mem_v.reshape(depth, bm, t, xw).astype(BF16),
        zeros(n_even, b, RET_HEADS, RET_DK, RET_DV), zeros(n_odd, b, M_HEADS, M_DV, M_DK),
        zeros(n_odd, b, M_HEADS, M_DK), zeros(n_odd, b, M_HEADS), None, norm_g, gains, even_w, odd_w, shared)

    db, ds, _ = x_sample.shape
    past_len = cache_mla_ckv.shape[2]
    pos_s = past_len + jnp.arange(ds, dtype=jnp.int32)
    past = (cache_mla_ckv, jnp.pad(cache_mla_krope, ((0, 0), (0, 0), (0, 0), (0, LANES - MLA_ROPE))))
    y_sample, s_ckv, s_kr, s_ret, s_c, s_n, s_m = _run_trunk(
        x_sample, pos_s, cache_mem_k.reshape(depth, db, -1, xw).astype(BF16),
        cache_mem_v.reshape(depth, db, -1, xw).astype(BF16),
        state_ret, state_mlstm_C, state_mlstm_n, state_mlstm_m, past, norm_g, gains, even_w, odd_w, shared)

    return (y_prompt, y_sample, p_ckv, p_kr, p_ret, p_c, p_n, p_m, p_mem_k, p_mem_v,
            s_ckv, s_kr, s_ret, s_c, s_n, s_m)
```
